```python
import jax, jax.numpy as jnp
from jax import lax
import numpy as np

D_MODEL = 1024
BATCH = 8
SEQ = 2048
DEPTH = 1
DEC_BATCH = 128
DEC_SEQ = 1
PAST_LEN = 16384
PAGE_SIZE = 128

PLE_DIM = 256
D_FF = 2816
CHUNK = 128
A_GROUPS = 4
A_GROUP_DIM = 128
A_WIDTH = A_GROUPS * A_GROUP_DIM
B_HEADS = 4
B_DK = 128
B_DV = 256
B_KW = B_HEADS * B_DK
B_VW = B_HEADS * B_DV
GATE_RANK = 16
GATE_NORM = 16.0
GLA_BLOCK = 32
EPS = 1e-6
IN_SIZES = (A_WIDTH, A_WIDTH, B_KW, B_KW, B_VW, B_VW, GATE_RANK, D_MODEL, D_MODEL)
IN_COLS = sum(IN_SIZES)
IN_SPLITS = tuple(int(s) for s in np.cumsum(IN_SIZES)[:-1])

kernel_name = "hybrid_gmlp_gla_macaron_step"


def rmsnorm(x, g):
    xf = x.astype(jnp.float32)
    y = xf * lax.rsqrt(jnp.mean(xf * xf, axis=-1, keepdims=True) + EPS)
    return (y * g.astype(jnp.float32)).astype(x.dtype)


def layernorm(x, g, b):
    xf = x.astype(jnp.float32)
    mu = jnp.mean(xf, axis=-1, keepdims=True)
    var = jnp.mean(jnp.square(xf - mu), axis=-1, keepdims=True)
    y = (xf - mu) * lax.rsqrt(var + EPS)
    return (y * g.astype(jnp.float32) + b.astype(jnp.float32)).astype(x.dtype)


def swiglu(h, w_in, w_out):
    gate, up = jnp.split(h @ w_in, 2, axis=-1)
    return (jax.nn.silu(gate) * up) @ w_out


def gla_recurrence(q, k, v, log_a, s0, block):
    n, t, h, _ = q.shape
    dv = v.shape[-1]
    nb = t // block

    def to_blocks(a):
        return jnp.moveaxis(a.astype(jnp.float32).reshape(n, nb, block, h, a.shape[-1]), 1, 0)

    causal = jnp.tril(jnp.ones((block, block), dtype=bool))

    def step(s, blk):
        qb, kb, vb, la = blk
        b = jnp.cumsum(la, axis=1)
        b_last = b[:, -1]
        q_dec = qb * jnp.exp(b)
        scores = jnp.einsum('nthk,nshk->nhts', q_dec, kb * jnp.exp(-b))
        scores = jnp.where(causal, scores, 0.0)
        o = jnp.einsum('nhts,nshv->nthv', scores, vb) + jnp.einsum('nthk,nhkv->nthv', q_dec, s)
        k_state = kb * jnp.exp(b_last[:, None] - b)
        s = s * jnp.exp(b_last)[..., None] + jnp.einsum('nshk,nshv->nhkv', k_state, vb)
        return s, o

    s_fin, os = lax.scan(step, s0.astype(jnp.float32), tuple(map(to_blocks, (q, k, v, log_a))))
    o = jnp.moveaxis(os, 0, 1).reshape(n, t, h, dv)
    return o, s_fin


def layer(x, p, s0, pos, chunk_len, gla_block,
          g_ffn1, w_ffn1_in, w_ffn1_out, g_mix, w_in, ln_v_g, ln_v_b, w_spatial, b_spatial,
          w_gate_up, b_gate, g_gla_out, w_proj_a, w_proj_b, w_out,
          g_ffn2, w_ffn2_in, w_ffn2_out, g_ple, w_ple_gate, w_ple):
    n, t, _ = x.shape
    x = x + 0.5 * swiglu(rmsnorm(x, g_ffn1), w_ffn1_in, w_ffn1_out)

    h = rmsnorm(x, g_mix)
    u_a, v_a, q_b, k_b, v_b, g_b, lr_b, gate_a, gate_b = jnp.split(h @ w_in, IN_SPLITS, axis=-1)

    u_a = jax.nn.gelu(u_a)
    v_n = layernorm(jax.nn.gelu(v_a), ln_v_g, ln_v_b)
    mask = pos[:, None] >= pos[None, :]
    ws = jnp.where(mask, w_spatial[:, pos[:, None], pos[None, :]], 0.0).astype(x.dtype)
    bias = b_spatial[:, pos].T[:, :, None].astype(x.dtype)
    vb = v_n.reshape(n, t // chunk_len, chunk_len, A_GROUPS, A_GROUP_DIM)
    mixed = jnp.einsum('gts,ncsgd->nctgd', ws, vb) + bias
    y_a = u_a * mixed.reshape(n, t, A_WIDTH)

    q = q_b.reshape(n, t, B_HEADS, B_DK) * (B_DK ** -0.5)
    k = k_b.reshape(n, t, B_HEADS, B_DK)
    v = v_b.reshape(n, t, B_HEADS, B_DV)
    logit = (lr_b @ w_gate_up + b_gate).astype(jnp.float32)
    log_a = (jax.nn.log_sigmoid(logit) / GATE_NORM).reshape(n, t, B_HEADS, B_DK)
    o, s_new = gla_recurrence(q, k, v, log_a, s0, gla_block)
    o = rmsnorm(o, g_gla_out).astype(x.dtype).reshape(n, t, B_VW)
    y_b = o * jax.nn.silu(g_b)

    mix = jax.nn.sigmoid(gate_a) * (y_a @ w_proj_a) + jax.nn.sigmoid(gate_b) * (y_b @ w_proj_b)
    x = x + mix @ w_out

    x = x + 0.5 * swiglu(rmsnorm(x, g_ffn2), w_ffn2_in, w_ffn2_out)

    x = x + (p @ w_ple) * jax.nn.sigmoid(rmsnorm(x, g_ple) @ w_ple_gate)
    return x, s_new.astype(s0.dtype), v_n


def setup_inputs(seed: int = 0) -> dict:
    key = jax.random.key(seed)
    ks = iter(jax.random.split(key, 40))

    def nrm(shape, scale):
        return jax.random.normal(next(ks), shape, jnp.float32) * scale

    def gain(shape):
        return 1.0 + nrm(shape, 0.02)

    L = DEPTH
    return {
        "x_prompt": nrm((BATCH, SEQ, D_MODEL), 1.0),
        "x_sample": nrm((DEC_BATCH, DEC_SEQ, D_MODEL), 1.0),
        "p_prompt": nrm((L, BATCH, SEQ, PLE_DIM), 1.0),
        "p_sample": nrm((L, DEC_BATCH, DEC_SEQ, PLE_DIM), 1.0),
        "state_gla": nrm((L, DEC_BATCH, B_HEADS, B_DK, B_DV), 0.1),
        "g_ffn1": gain((L, D_MODEL)),
        "w_ffn1_in": nrm((L, D_MODEL, 2 * D_FF), D_MODEL ** -0.5),
        "w_ffn1_out": nrm((L, D_FF, D_MODEL), D_FF ** -0.5),
        "g_mix": gain((L, D_MODEL)),
        "w_in": nrm((L, D_MODEL, IN_COLS), D_MODEL ** -0.5),
        "ln_v_g": gain((L, A_WIDTH)),
        "ln_v_b": nrm((L, A_WIDTH), 0.02),
        "w_spatial": nrm((L, A_GROUPS, CHUNK, CHUNK), CHUNK ** -0.5),
        "b_spatial": 1.0 + nrm((L, A_GROUPS, CHUNK), 0.02),
        "w_gate_up": nrm((L, GATE_RANK, B_KW), GATE_RANK ** -0.5),
        "b_gate": nrm((L, B_KW), 0.02),
        "g_gla_out": gain((L, B_DV)),
        "w_proj_a": nrm((L, A_WIDTH, D_MODEL), A_WIDTH ** -0.5),
        "w_proj_b": nrm((L, B_VW, D_MODEL), B_VW ** -0.5),
        "w_out": nrm((L, D_MODEL, D_MODEL), D_MODEL ** -0.5),
        "g_ffn2": gain((L, D_MODEL)),
        "w_ffn2_in": nrm((L, D_MODEL, 2 * D_FF), D_MODEL ** -0.5),
        "w_ffn2_out": nrm((L, D_FF, D_MODEL), D_FF ** -0.5),
        "g_ple": gain((L, D_MODEL)),
        "w_ple_gate": nrm((L, D_MODEL, D_MODEL), D_MODEL ** -0.5),
        "w_ple": nrm((L, PLE_DIM, D_MODEL), PLE_DIM ** -0.5),
        "g_final": gain((D_MODEL,)),
    }


def reference(x_prompt, x_sample, p_prompt, p_sample, state_gla,
              g_ffn1, w_ffn1_in, w_ffn1_out, g_mix, w_in, ln_v_g, ln_v_b, w_spatial, b_spatial,
              w_gate_up, b_gate, g_gla_out, w_proj_a, w_proj_b, w_out,
              g_ffn2, w_ffn2_in, w_ffn2_out, g_ple, w_ple_gate, w_ple, g_final):
    pos_prompt = np.arange(CHUNK)
    pos_sample = PAST_LEN % CHUNK + np.arange(DEC_SEQ)
    s0_prompt = jnp.zeros((BATCH, B_HEADS, B_DK, B_DV), x_prompt.dtype)
    xp, xs = x_prompt, x_sample
    sp_list, ss_list, vs_list = [], [], []
    for i in range(DEPTH):
        w = (g_ffn1[i], w_ffn1_in[i], w_ffn1_out[i], g_mix[i], w_in[i], ln_v_g[i], ln_v_b[i],
             w_spatial[i], b_spatial[i], w_gate_up[i], b_gate[i], g_gla_out[i], w_proj_a[i],
             w_proj_b[i], w_out[i], g_ffn2[i], w_ffn2_in[i], w_ffn2_out[i], g_ple[i],
             w_ple_gate[i], w_ple[i])
        xp, sp, _ = layer(xp, p_prompt[i], s0_prompt, pos_prompt, CHUNK, GLA_BLOCK, *w)
        xs, ss, vs = layer(xs, p_sample[i], state_gla[i], pos_sample, DEC_SEQ, DEC_SEQ, *w)
        sp_list.append(sp)
        ss_list.append(ss)
        vs_list.append(vs)
    y_prompt = rmsnorm(xp, g_final)
    y_sample = rmsnorm(xs, g_final)
    state_gla_prompt = jnp.stack(sp_list)
    state_gla_sample = jnp.stack(ss_list)
    chunk_v_sample = jnp.stack(vs_list)
    return (y_prompt, y_sample, state_gla_prompt, state_gla_sample, chunk_v_sample)
```

```python
import functools

import jax
import jax.numpy as jnp
from jax import lax
from jax.experimental import pallas as pl
from jax.experimental.pallas import tpu as pltpu

D_MODEL = 1024
D_FF = 2816
PLE_DIM = 256
CHUNK = 128
A_GROUPS = 4
A_GROUP_DIM = 128
A_WIDTH = A_GROUPS * A_GROUP_DIM
B_HEADS = 4
B_DK = 128
B_DV = 256
B_KW = B_HEADS * B_DK
B_VW = B_HEADS * B_DV
GATE_RANK = 16
GATE_NORM = 16.0
EPS = 1e-6
IN_SIZES = (A_WIDTH, A_WIDTH, B_KW, B_KW, B_VW, B_VW, GATE_RANK, D_MODEL, D_MODEL)

LANES = 128
P_COLS = 6 * D_MODEL
FF_CHUNK = 256
HALF = CHUNK // 2
VMEM_LIMIT = 56 * 1024 * 1024

F32 = jnp.float32
BF16 = jnp.bfloat16


def _dot(a, b):
    return jnp.dot(a, b, preferred_element_type=F32)


def _dot_nt(a, b):
    return lax.dot_general(a, b, (((1,), (1,)), ((), ())), preferred_element_type=F32)


def _rms(x, g):
    return x * lax.rsqrt(jnp.mean(x * x, axis=-1, keepdims=True) + EPS) * g


def _log_sigmoid(x):
    return jnp.minimum(x, 0.0) - jnp.log1p(jnp.exp(-jnp.abs(x)))


def _layernorm(x, g, b):
    mu = jnp.mean(x, axis=-1, keepdims=True)
    xc = x - mu
    var = jnp.mean(xc * xc, axis=-1, keepdims=True)
    return xc * lax.rsqrt(var + EPS) * g + b


def _swiglu_residual(x, g_ref, w_in_ref, w_out_ref):
    h = _rms(x, g_ref[...]).astype(BF16)
    acc = None
    for lo in range(0, D_FF, FF_CHUNK):
        gate = _dot(h, w_in_ref[:, lo:lo + FF_CHUNK])
        up = _dot(h, w_in_ref[:, D_FF + lo:D_FF + lo + FF_CHUNK])
        act = (jax.nn.silu(gate) * up).astype(BF16)
        part = _dot(act, w_out_ref[lo:lo + FF_CHUNK, :])
        acc = part if acc is None else acc + part
    return x + 0.5 * acc


def _ffn1_kernel(x_ref, g1_ref, w1i_ref, w1o_ref, gmix_ref, x1_ref, h_ref):
    x1 = _swiglu_residual(x_ref[...], g1_ref, w1i_ref, w1o_ref)
    x1_ref[...] = x1
    h_ref[...] = _rms(x1, gmix_ref[...]).astype(BF16)


def _inproj_kernel(h_ref, w_ref, wlr_ref, p_ref, lr_ref):
    h = h_ref[...]
    for lo in range(0, P_COLS, D_MODEL):
        p_ref[:, lo:lo + D_MODEL] = _dot(h, w_ref[:, lo:lo + D_MODEL]).astype(p_ref.dtype)
    lr_ref[...] = _dot(h, wlr_ref[...]).astype(lr_ref.dtype)


def _mixer_kernel(uv_ref, qk_ref, v_ref, gb_ref, lr_ref, lng_ref, lnb_ref, ws_ref, bsp_ref,
                  wgu_ref, bg_ref, ggla_ref, ya_ref, yb_ref, sout_ref, s_scr, *, n_chunks):
    t = pl.program_id(1)

    @pl.when(t == 0)
    def _():
        s_scr[...] = jnp.zeros_like(s_scr)

    row = lax.broadcasted_iota(jnp.int32, (CHUNK, CHUNK), 0)
    col = lax.broadcasted_iota(jnp.int32, (CHUNK, CHUNK), 1)
    causal = row >= col
    ones_tril = causal.astype(BF16)
    ws_tril = [jnp.where(causal, ws_ref[g], 0.0).astype(BF16) for g in range(A_GROUPS)]
    zero_blk = jnp.zeros((HALF, B_DK), BF16)

    for c in range(n_chunks):
        rows = slice(c * CHUNK, (c + 1) * CHUNK)

        uv = uv_ref[rows, :].astype(F32)
        u = jax.nn.gelu(uv[:, :A_WIDTH])
        vn = _layernorm(jax.nn.gelu(uv[:, A_WIDTH:]), lng_ref[...], lnb_ref[...]).astype(BF16)
        for g in range(A_GROUPS):
            cols = slice(g * A_GROUP_DIM, (g + 1) * A_GROUP_DIM)
            mixed = _dot(ws_tril[g], vn[:, cols]) + bsp_ref[:, g:g + 1]
            ya_ref[rows, cols] = (u[:, cols] * mixed).astype(ya_ref.dtype)

        logit = _dot(lr_ref[rows, :], wgu_ref[...]) + bg_ref[...]
        la = _log_sigmoid(logit) * (1.0 / GATE_NORM)
        la_hi = la.astype(BF16)
        la_lo = (la - la_hi.astype(F32)).astype(BF16)
        b = _dot(ones_tril, la_hi) + _dot(ones_tril, la_lo)

        r0 = b[HALF // 2 - 1:HALF // 2, :]
        r1 = b[HALF + HALF // 2 - 1:HALF + HALF // 2, :]
        b_mid = b[HALF - 1:HALF, :]
        b_end = b[CHUNK - 1:CHUNK, :]
        bq0 = b[:HALF] - r0
        bq1 = b[HALF:] - r1
        qk = qk_ref[rows, :].astype(F32)
        qs = qk[:, :B_KW] * (B_DK ** -0.5)
        k = qk[:, B_KW:]
        a0 = qs[:HALF] * jnp.exp(bq0)
        a1 = qs[HALF:] * jnp.exp(bq1)
        k0 = k[:HALF] * jnp.exp(-bq0)
        k1 = k[HALF:] * jnp.exp(-bq1)
        q_off1 = a1 * jnp.exp(r1 - b_mid)
        k_off0 = k0 * jnp.exp(b_mid - r0)
        q_int = jnp.concatenate([a0 * jnp.exp(r0), q_off1 * jnp.exp(b_mid)], axis=0).astype(BF16)
        k_st = jnp.concatenate([k_off0 * jnp.exp(b_end - b_mid), k1 * jnp.exp(b_end - r1)], axis=0)
        decay = jnp.exp(b_end)
        a0, a1, k0, k1 = (z.astype(BF16) for z in (a0, a1, k0, k1))
        q_off1 = q_off1.astype(BF16)
        k_off0 = k_off0.astype(BF16)

        v = v_ref[rows, :]
        gb = gb_ref[rows, :].astype(F32)
        for h in range(B_HEADS):
            kc = slice(h * B_DK, (h + 1) * B_DK)
            vc = slice(h * B_DV, (h + 1) * B_DV)
            q3 = jnp.concatenate([
                jnp.concatenate([a0[:, kc], zero_blk, zero_blk], axis=1),
                jnp.concatenate([zero_blk, q_off1[:, kc], a1[:, kc]], axis=1)], axis=0)
            k3 = jnp.concatenate([
                jnp.concatenate([k0[:, kc], k_off0[:, kc], zero_blk], axis=1),
                jnp.concatenate([zero_blk, zero_blk, k1[:, kc]], axis=1)], axis=0)
            scores = jnp.where(causal, _dot_nt(q3, k3), 0.0).astype(BF16)
            s_prev = s_scr[h]
            v_h = v[:, vc]
            o = _dot(jnp.concatenate([scores, q_int[:, kc]], axis=1),
                     jnp.concatenate([v_h, s_prev.astype(BF16)], axis=0))
            upd = _dot(k_st[:, kc].T.astype(BF16), v_h)
            decay_col = jnp.broadcast_to(decay[:, kc], (B_DK, B_DK)).T
            s_scr[h] = s_prev * jnp.concatenate([decay_col, decay_col], axis=1) + upd
            o_n = _rms(o, ggla_ref[...])
            yb_ref[rows, vc] = (o_n * jax.nn.silu(gb[:, vc])).astype(yb_ref.dtype)

    @pl.when(t == pl.num_programs(1) - 1)
    def _():
        sout_ref[0] = s_scr[...]


def _sample_mixer_kernel(uv_ref, qk_ref, v_ref, gb_ref, lr_ref, s_ref, lng_ref, lnb_ref, ws0_ref,
                         bs0_ref, wgu_ref, bg_ref, ggla_ref, ya_ref, yb_ref, sout_ref, cv_ref,
                         o_scr, *, nb):
    uv = uv_ref[...]
    u = jax.nn.gelu(uv[:, :A_WIDTH])
    vn = _layernorm(jax.nn.gelu(uv[:, A_WIDTH:]), lng_ref[...], lnb_ref[...])
    cv_ref[...] = vn
    ya_ref[...] = u * (vn * ws0_ref[...] + bs0_ref[...])

    logit = _dot(lr_ref[...].astype(BF16), wgu_ref[...]) + bg_ref[...]
    a = jnp.exp(_log_sigmoid(logit) * (1.0 / GATE_NORM))
    qk = qk_ref[...]
    qs = qk[:, :B_KW] * (B_DK ** -0.5)
    k = qk[:, B_KW:]
    v = v_ref[...]
    pad = jnp.zeros((LANES - 3 * nb, B_DK), F32)
    for h in range(B_HEADS):
        kc = slice(h * B_DK, (h + 1) * B_DK)
        vc = slice(h * B_DV, (h + 1) * B_DV)
        xt = jnp.concatenate([a[:, kc], k[:, kc], qs[:, kc], pad], axis=0).T
        for n in range(nb):
            s_new = (s_ref[n, h] * xt[:, n:n + 1]
                     + xt[:, nb + n:nb + n + 1] * v[n:n + 1, vc])
            sout_ref[n, h] = s_new
            o_scr[n:n + 1, vc] = jnp.sum(xt[:, 2 * nb + n:2 * nb + n + 1] * s_new,
                                         axis=0, keepdims=True)
    gb = gb_ref[...]
    for h in range(B_HEADS):
        vc = slice(h * B_DV, (h + 1) * B_DV)
        yb_ref[:, vc] = _rms(o_scr[:, vc], ggla_ref[...]) * jax.nn.silu(gb[:, vc])


def _merge_kernel(ya_ref, yb_ref, ga_ref, gb_ref, x1_ref, wpa_ref, wpb_ref, wo_ref, x2_ref):
    mix = (jax.nn.sigmoid(ga_ref[...].astype(F32)) * _dot(ya_ref[...], wpa_ref[...])
           + jax.nn.sigmoid(gb_ref[...].astype(F32)) * _dot(yb_ref[...], wpb_ref[...]))
    x2_ref[...] = x1_ref[...] + _dot(mix.astype(BF16), wo_ref[...])


def _ffn2_kernel(x_ref, p_ref, g2_ref, w2i_ref, w2o_ref, gple_ref, wpg_ref, wple_ref, gfin_ref,
                 y_ref, *, final_norm):
    x3 = _swiglu_residual(x_ref[...], g2_ref, w2i_ref, w2o_ref)
    gate = jax.nn.sigmoid(_dot(_rms(x3, gple_ref[...]).astype(BF16), wpg_ref[...]))
    x4 = x3 + _dot(p_ref[...].astype(BF16), wple_ref[...]) * gate
    y_ref[...] = _rms(x4, gfin_ref[...]) if final_norm else x4


def _resident(shape):
    zeros = (0,) * len(shape)
    return pl.BlockSpec(shape, lambda *_: zeros, pipeline_mode=pl.Buffered(1))


def _rows(tm, width, colblk=0):
    return pl.BlockSpec((tm, width), lambda i: (i, colblk))


def _params(n_axes):
    return pltpu.CompilerParams(dimension_semantics=("arbitrary",) * n_axes,
                                vmem_limit_bytes=VMEM_LIMIT)


def _token_tile(m):
    return 512 if m % 512 == 0 else m


def _ffn1(x, g1, w1i, w1o, gmix):
    m = x.shape[0]
    tm = _token_tile(m)
    return pl.pallas_call(
        _ffn1_kernel,
        grid=(m // tm,),
        in_specs=[_rows(tm, D_MODEL), _resident(g1.shape), _resident(w1i.shape),
                  _resident(w1o.shape), _resident(gmix.shape)],
        out_specs=[_rows(tm, D_MODEL), _rows(tm, D_MODEL)],
        out_shape=[jax.ShapeDtypeStruct((m, D_MODEL), F32),
                   jax.ShapeDtypeStruct((m, D_MODEL), BF16)],
        compiler_params=_params(1),
        name="ffn1",
    )(x, g1, w1i, w1o, gmix)


def _inproj(h, w_main, w_lr, out_dtype):
    m = h.shape[0]
    tm = _token_tile(m)
    return pl.pallas_call(
        _inproj_kernel,
        grid=(m // tm,),
        in_specs=[_rows(tm, D_MODEL), _resident(w_main.shape), _resident(w_lr.shape)],
        out_specs=[_rows(tm, P_COLS), _rows(tm, LANES)],
        out_shape=[jax.ShapeDtypeStruct((m, P_COLS), out_dtype),
                   jax.ShapeDtypeStruct((m, LANES), out_dtype)],
        compiler_params=_params(1),
        name="inproj",
    )(h, w_main, w_lr)


def _mixer(p, lr, n_seq, seq, lng, lnb, ws, bsp_t, wgu, bg, ggla):
    tt = 2 * CHUNK
    nt = seq // tt
    m = n_seq * seq

    def seg(colblk):
        return pl.BlockSpec((tt, D_MODEL), lambda b, t: (b * nt + t, colblk))

    small = [lng, lnb, ws, bsp_t, wgu, bg, ggla]
    return pl.pallas_call(
        functools.partial(_mixer_kernel, n_chunks=tt // CHUNK),
        grid=(n_seq, nt),
        in_specs=[seg(0), seg(1), seg(2), seg(3),
                  pl.BlockSpec((tt, LANES), lambda b, t: (b * nt + t, 0))]
                 + [_resident(w.shape) for w in small],
        out_specs=[pl.BlockSpec((tt, A_WIDTH), lambda b, t: (b * nt + t, 0)),
                   pl.BlockSpec((tt, B_VW), lambda b, t: (b * nt + t, 0)),
                   pl.BlockSpec((1, B_HEADS, B_DK, B_DV), lambda b, t: (b, 0, 0, 0))],
        out_shape=[jax.ShapeDtypeStruct((m, A_WIDTH), BF16),
                   jax.ShapeDtypeStruct((m, B_VW), BF16),
                   jax.ShapeDtypeStruct((n_seq, B_HEADS, B_DK, B_DV), F32)],
        scratch_shapes=[pltpu.VMEM((B_HEADS, B_DK, B_DV), F32)],
        compiler_params=_params(2),
        name="mixer",
    )(p, p, p, p, lr, *small)


def _sample_mixer(p, lr, state, lng, lnb, ws0, bs0, wgu, bg, ggla):
    m = p.shape[0]
    nb = 8

    def seg(colblk):
        return pl.BlockSpec((nb, D_MODEL), lambda i: (i, colblk))

    state_spec = pl.BlockSpec((nb, B_HEADS, B_DK, B_DV), lambda i: (i, 0, 0, 0))
    small = [lng, lnb, ws0, bs0, wgu, bg, ggla]
    return pl.pallas_call(
        functools.partial(_sample_mixer_kernel, nb=nb),
        grid=(m // nb,),
        in_specs=[seg(0), seg(1), seg(2), seg(3), _rows(nb, LANES), state_spec]
                 + [_resident(w.shape) for w in small],
        out_specs=[_rows(nb, A_WIDTH), _rows(nb, B_VW), state_spec, _rows(nb, A_WIDTH)],
        out_shape=[jax.ShapeDtypeStruct((m, A_WIDTH), F32),
                   jax.ShapeDtypeStruct((m, B_VW), F32),
                   jax.ShapeDtypeStruct(state.shape, F32),
                   jax.ShapeDtypeStruct((m, A_WIDTH), F32)],
        scratch_shapes=[pltpu.VMEM((nb, B_VW), F32)],
        compiler_params=_params(1),
        name="sample_mixer",
    )(p, p, p, p, lr, state, *small)


def _merge(ya, yb, p, x1, wpa, wpb, wo):
    m = x1.shape[0]
    tm = _token_tile(m)
    return pl.pallas_call(
        _merge_kernel,
        grid=(m // tm,),
        in_specs=[_rows(tm, A_WIDTH), _rows(tm, B_VW), _rows(tm, D_MODEL, 4),
                  _rows(tm, D_MODEL, 5), _rows(tm, D_MODEL), _resident(wpa.shape),
                  _resident(wpb.shape), _resident(wo.shape)],
        out_specs=_rows(tm, D_MODEL),
        out_shape=jax.ShapeDtypeStruct((m, D_MODEL), F32),
        compiler_params=_params(1),
        name="merge",
    )(ya, yb, p, p, x1, wpa, wpb, wo)


def _ffn2(x, p, g2, w2i, w2o, gple, wpg, wple, gfin, final_norm):
    m = x.shape[0]
    tm = _token_tile(m)
    weights = [g2, w2i, w2o, gple, wpg, wple, gfin]
    return pl.pallas_call(
        functools.partial(_ffn2_kernel, final_norm=final_norm),
        grid=(m // tm,),
        in_specs=[_rows(tm, D_MODEL), _rows(tm, PLE_DIM)] + [_resident(w.shape) for w in weights],
        out_specs=_rows(tm, D_MODEL),
        out_shape=jax.ShapeDtypeStruct((m, D_MODEL), F32),
        compiler_params=_params(1),
        name="ffn2",
    )(x, p, *weights)


def kernel(x_prompt, x_sample, p_prompt, p_sample, state_gla, g_ffn1, w_ffn1_in, w_ffn1_out, g_mix,
           w_in, ln_v_g, ln_v_b, w_spatial, b_spatial, w_gate_up, b_gate, g_gla_out, w_proj_a,
           w_proj_b, w_out, g_ffn2, w_ffn2_in, w_ffn2_out, g_ple, w_ple_gate, w_ple, g_final):
    depth = w_in.shape[0]
    n_seq, seq, _ = x_prompt.shape
    n_dec, dec_seq, _ = x_sample.shape
    assert dec_seq == 1 and seq % (2 * CHUNK) == 0

    def row(vec):
        return vec.reshape(1, -1).astype(F32)

    lr_lo = sum(IN_SIZES[:6])
    lr_hi = lr_lo + GATE_RANK
    xp = x_prompt.reshape(n_seq * seq, D_MODEL)
    xs = x_sample.reshape(n_dec, D_MODEL)
    gfin = row(g_final)
    sp_list, ss_list, vs_list = [], [], []
    for i in range(depth):
        w1i, w1o = w_ffn1_in[i].astype(BF16), w_ffn1_out[i].astype(BF16)
        w2i, w2o = w_ffn2_in[i].astype(BF16), w_ffn2_out[i].astype(BF16)
        w_main = jnp.concatenate([w_in[i][:, :lr_lo], w_in[i][:, lr_hi:]], axis=1).astype(BF16)
        w_lr = jnp.pad(w_in[i][:, lr_lo:lr_hi], ((0, 0), (0, LANES - GATE_RANK))).astype(BF16)
        wgu = jnp.pad(w_gate_up[i], ((0, LANES - GATE_RANK), (0, 0))).astype(BF16)
        wpa, wpb, wo = (w.astype(BF16) for w in (w_proj_a[i], w_proj_b[i], w_out[i]))
        wpg, wple = w_ple_gate[i].astype(BF16), w_ple[i].astype(BF16)
        g1, gmix, g2, gple = row(g_ffn1[i]), row(g_mix[i]), row(g_ffn2[i]), row(g_ple[i])
        lng, lnb, bg, ggla = row(ln_v_g[i]), row(ln_v_b[i]), row(b_gate[i]), row(g_gla_out[i])
        ws0 = row(jnp.repeat(w_spatial[i][:, 0, 0], A_GROUP_DIM))
        bs0 = row(jnp.repeat(b_spatial[i][:, 0], A_GROUP_DIM))

        x1, h = _ffn1(xp, g1, w1i, w1o, gmix)
        p, lr = _inproj(h, w_main, w_lr, BF16)
        ya, yb, sp = _mixer(p, lr, n_seq, seq, lng, lnb, w_spatial[i], b_spatial[i].T, wgu, bg,
                            ggla)
        x2 = _merge(ya, yb, p, x1, wpa, wpb, wo)
        pp = p_prompt[i].reshape(n_seq * seq, PLE_DIM)
        last = i == depth - 1
        xp = _ffn2(x2, pp, g2, w2i, w2o, gple, wpg, wple, gfin, last)

        x1, h = _ffn1(xs, g1, w1i, w1o, gmix)
        p, lr = _inproj(h, w_main, w_lr, F32)
        ya, yb, ss, cv = _sample_mixer(p, lr, state_gla[i], lng, lnb, ws0, bs0, wgu, bg, ggla)
        x2 = _merge(ya.astype(BF16), yb.astype(BF16), p, x1, wpa, wpb, wo)
        ps = p_sample[i].reshape(n_dec, PLE_DIM)
        xs = _ffn2(x2, ps, g2, w2i, w2o, gple, wpg, wple, gfin, last)

        sp_list.append(sp)
        ss_list.append(ss)
        vs_list.append(cv.reshape(n_dec, dec_seq, A_WIDTH))
    return (xp.reshape(n_seq, seq, D_MODEL), xs.reshape(n_dec, dec_seq, D_MODEL),
            jnp.stack(sp_list), jnp.stack(ss_list), jnp.stack(vs_list))
```

```python
import functools

import jax
import jax.numpy as jnp
from jax import lax
from jax.experimental import pallas as pl
from jax.experimental.pallas import tpu as pltpu

D_MODEL = 1024
D_FF = 2816
PLE_DIM = 256
CHUNK = 128
A_GROUPS = 4
A_GROUP_DIM = 128
A_WIDTH = A_GROUPS * A_GROUP_DIM
B_HEADS = 4
B_DK = 128
B_DV = 256
B_KW = B_HEADS * B_DK
B_VW = B_HEADS * B_DV
GATE_RANK = 16
GATE_NORM = 16.0
EPS = 1e-6
IN_SIZES = (A_WIDTH, A_WIDTH, B_KW, B_KW, B_VW, B_VW, GATE_RANK, D_MODEL, D_MODEL)

LANES = 128
P_COLS = 6 * D_MODEL
QK_LO = 2 * A_WIDTH
VB_LO = QK_LO + 2 * B_KW
GB_LO = VB_LO + B_VW
LR_LO = GB_LO + B_VW
GATE_LO = LR_LO + GATE_RANK
FF_CHUNK = 256
HALF = CHUNK // 2
VMEM_LIMIT = 56 * 1024 * 1024

F32 = jnp.float32
BF16 = jnp.bfloat16


def _dot(a, b):
    return jnp.dot(a, b, preferred_element_type=F32)


def _dot_nt(a, b):
    return lax.dot_general(a, b, (((1,), (1,)), ((), ())), preferred_element_type=F32)


def _rms(x, g):
    return x * lax.rsqrt(jnp.mean(x * x, axis=-1, keepdims=True) + EPS) * g


def _log_sigmoid(x):
    return jnp.minimum(x, 0.0) - jnp.log1p(jnp.exp(-jnp.abs(x)))


def _layernorm(x, g, b):
    mu = jnp.mean(x, axis=-1, keepdims=True)
    xc = x - mu
    var = jnp.mean(xc * xc, axis=-1, keepdims=True)
    return xc * lax.rsqrt(var + EPS) * g + b


def _swiglu_residual(x, g_ref, w_in_ref, w_out_ref):
    h = _rms(x, g_ref[...]).astype(BF16)
    acc = None
    for lo in range(0, D_FF, FF_CHUNK):
        gate = _dot(h, w_in_ref[:, lo:lo + FF_CHUNK])
        up = _dot(h, w_in_ref[:, D_FF + lo:D_FF + lo + FF_CHUNK])
        act = (jax.nn.silu(gate) * up).astype(BF16)
        part = _dot(act, w_out_ref[lo:lo + FF_CHUNK, :])
        acc = part if acc is None else acc + part
    return x + 0.5 * acc


def _ffn1_kernel(x_ref, g1_ref, w1i_ref, w1o_ref, gmix_ref, x1_ref, h_ref):
    x1 = _swiglu_residual(x_ref[...], g1_ref, w1i_ref, w1o_ref)
    x1_ref[...] = x1
    h_ref[...] = _rms(x1, gmix_ref[...]).astype(BF16)


def _inproj_kernel(h_ref, w_ref, wgu_ref, bg_ref, lng_ref, lnb_ref, p_ref, la_ref, wgate_scr):
    @pl.when(pl.program_id(0) == 0)
    def _():
        wgate_scr[...] = w_ref[:, GATE_LO:GATE_LO + 2 * D_MODEL]

    def put(lo, val):
        p_ref[:, lo:lo + val.shape[1]] = val.astype(p_ref.dtype)

    h = h_ref[...]
    uv = _dot(h, w_ref[:, 0:2 * A_WIDTH])
    put(0, jax.nn.gelu(uv[:, :A_WIDTH]))
    put(A_WIDTH, _layernorm(jax.nn.gelu(uv[:, A_WIDTH:]), lng_ref[...], lnb_ref[...]))
    qk = _dot(h, w_ref[:, QK_LO:QK_LO + 2 * B_KW])
    put(QK_LO, qk[:, :B_KW] * (B_DK ** -0.5))
    put(QK_LO + B_KW, qk[:, B_KW:])
    put(VB_LO, _dot(h, w_ref[:, VB_LO:VB_LO + B_VW]))
    put(GB_LO, jax.nn.silu(_dot(h, w_ref[:, GB_LO:GB_LO + B_VW])))
    lr = _dot(h, w_ref[:, LR_LO:LR_LO + LANES]).astype(BF16)
    logit = _dot(lr, wgu_ref[...]) + bg_ref[...]
    la_ref[...] = _log_sigmoid(logit) * (1.0 / GATE_NORM)
    for j in range(2):
        gate = _dot(h, wgate_scr[:, j * D_MODEL:(j + 1) * D_MODEL])
        put(LR_LO + j * D_MODEL, jax.nn.sigmoid(gate))


def _mixer_kernel(uv_ref, qk_ref, v_ref, gb_ref, la_ref, ws_ref, bsp_ref, ggla_ref,
                  ya_ref, yb_ref, sout_ref, s_scr, *, n_chunks):
    t = pl.program_id(1)

    @pl.when(t == 0)
    def _():
        s_scr[...] = jnp.zeros_like(s_scr)

    row = lax.broadcasted_iota(jnp.int32, (CHUNK, CHUNK), 0)
    col = lax.broadcasted_iota(jnp.int32, (CHUNK, CHUNK), 1)
    causal = row >= col
    ones_tril = causal.astype(BF16)
    ws_tril = [jnp.where(causal, ws_ref[g], 0.0).astype(BF16) for g in range(A_GROUPS)]
    zero_blk = jnp.zeros((HALF, B_DK), BF16)

    for c in range(n_chunks):
        rows = slice(c * CHUNK, (c + 1) * CHUNK)

        u = uv_ref[rows, :A_WIDTH].astype(F32)
        vn = uv_ref[rows, A_WIDTH:]
        for g in range(A_GROUPS):
            cols = slice(g * A_GROUP_DIM, (g + 1) * A_GROUP_DIM)
            mixed = _dot(ws_tril[g], vn[:, cols]) + bsp_ref[:, g:g + 1]
            ya_ref[rows, cols] = (u[:, cols] * mixed).astype(ya_ref.dtype)

        la = la_ref[rows, :]
        la_hi = la.astype(BF16)
        la_lo = (la - la_hi.astype(F32)).astype(BF16)
        b = _dot(ones_tril, la_hi) + _dot(ones_tril, la_lo)

        r0 = b[HALF // 2 - 1:HALF // 2, :]
        r1 = b[HALF + HALF // 2 - 1:HALF + HALF // 2, :]
        b_mid = b[HALF - 1:HALF, :]
        b_end = b[CHUNK - 1:CHUNK, :]
        bq0 = b[:HALF] - r0
        bq1 = b[HALF:] - r1
        qk = qk_ref[rows, :].astype(F32)
        qs = qk[:, :B_KW]
        k = qk[:, B_KW:]
        a0 = qs[:HALF] * jnp.exp(bq0)
        a1 = qs[HALF:] * jnp.exp(bq1)
        k0 = k[:HALF] * jnp.exp(-bq0)
        k1 = k[HALF:] * jnp.exp(-bq1)
        q_off1 = a1 * jnp.exp(r1 - b_mid)
        k_off0 = k0 * jnp.exp(b_mid - r0)
        q_int = jnp.concatenate([a0 * jnp.exp(r0), q_off1 * jnp.exp(b_mid)], axis=0).astype(BF16)
        k_st = jnp.concatenate([k_off0 * jnp.exp(b_end - b_mid), k1 * jnp.exp(b_end - r1)], axis=0)
        decay = jnp.exp(b_end)
        a0, a1, k0, k1 = (z.astype(BF16) for z in (a0, a1, k0, k1))
        q_off1 = q_off1.astype(BF16)
        k_off0 = k_off0.astype(BF16)

        v = v_ref[rows, :]
        gb = gb_ref[rows, :].astype(F32)
        for h in range(B_HEADS):
            kc = slice(h * B_DK, (h + 1) * B_DK)
            vc = slice(h * B_DV, (h + 1) * B_DV)
            q3 = jnp.concatenate([
                jnp.concatenate([a0[:, kc], zero_blk, zero_blk], axis=1),
                jnp.concatenate([zero_blk, q_off1[:, kc], a1[:, kc]], axis=1)], axis=0)
            k3 = jnp.concatenate([
                jnp.concatenate([k0[:, kc], k_off0[:, kc], zero_blk], axis=1),
                jnp.concatenate([zero_blk, zero_blk, k1[:, kc]], axis=1)], axis=0)
            scores = jnp.where(causal, _dot_nt(q3, k3), 0.0).astype(BF16)
            s_prev = s_scr[h]
            v_h = v[:, vc]
            o = _dot(jnp.concatenate([scores, q_int[:, kc]], axis=1),
                     jnp.concatenate([v_h, s_prev.astype(BF16)], axis=0))
            upd = _dot(k_st[:, kc].T.astype(BF16), v_h)
            decay_col = jnp.broadcast_to(decay[:, kc], (B_DK, B_DK)).T
            s_scr[h] = s_prev * jnp.concatenate([decay_col, decay_col], axis=1) + upd
            o_n = _rms(o, ggla_ref[...])
            yb_ref[rows, vc] = (o_n * gb[:, vc]).astype(yb_ref.dtype)

    @pl.when(t == pl.num_programs(1) - 1)
    def _():
        sout_ref[0] = s_scr[...]


def _sample_mixer_kernel(uv_ref, qk_ref, v_ref, gb_ref, la_ref, s_ref, ws0_ref, bs0_ref, ggla_ref,
                         ya_ref, yb_ref, sout_ref, cv_ref, o_scr, *, nb):
    vn = uv_ref[:, A_WIDTH:]
    cv_ref[...] = vn
    ya_ref[...] = uv_ref[:, :A_WIDTH] * (vn * ws0_ref[...] + bs0_ref[...])

    a = jnp.exp(la_ref[...])
    qk = qk_ref[...]
    qs = qk[:, :B_KW]
    k = qk[:, B_KW:]
    v = v_ref[...]
    pad = jnp.zeros((LANES - 3 * nb, B_DK), F32)
    for h in range(B_HEADS):
        kc = slice(h * B_DK, (h + 1) * B_DK)
        vc = slice(h * B_DV, (h + 1) * B_DV)
        xt = jnp.concatenate([a[:, kc], k[:, kc], qs[:, kc], pad], axis=0).T
        for n in range(nb):
            s_new = (s_ref[n, h] * xt[:, n:n + 1]
                     + xt[:, nb + n:nb + n + 1] * v[n:n + 1, vc])
            sout_ref[n, h] = s_new
            o_scr[n:n + 1, vc] = jnp.sum(xt[:, 2 * nb + n:2 * nb + n + 1] * s_new,
                                         axis=0, keepdims=True)
    gb = gb_ref[...]
    for h in range(B_HEADS):
        vc = slice(h * B_DV, (h + 1) * B_DV)
        yb_ref[:, vc] = _rms(o_scr[:, vc], ggla_ref[...]) * gb[:, vc]


def _merge_kernel(ya_ref, yb_ref, ga_ref, gb_ref, x1_ref, wpa_ref, wpb_ref, wo_ref, x2_ref):
    mix = (ga_ref[...].astype(F32) * _dot(ya_ref[...], wpa_ref[...])
           + gb_ref[...].astype(F32) * _dot(yb_ref[...], wpb_ref[...]))
    x2_ref[...] = x1_ref[...] + _dot(mix.astype(BF16), wo_ref[...])


def _ffn2_kernel(x_ref, p_ref, g2_ref, w2i_ref, w2o_ref, gple_ref, wpg_ref, wple_ref, gfin_ref,
                 y_ref, *, final_norm):
    x3 = _swiglu_residual(x_ref[...], g2_ref, w2i_ref, w2o_ref)
    gate = jax.nn.sigmoid(_dot(_rms(x3, gple_ref[...]).astype(BF16), wpg_ref[...]))
    x4 = x3 + _dot(p_ref[...].astype(BF16), wple_ref[...]) * gate
    y_ref[...] = _rms(x4, gfin_ref[...]) if final_norm else x4


def _resident(shape):
    zeros = (0,) * len(shape)
    return pl.BlockSpec(shape, lambda *_: zeros, pipeline_mode=pl.Buffered(1))


def _rows(tm, width, colblk=0):
    return pl.BlockSpec((tm, width), lambda i: (i, colblk))


def _params(n_axes):
    return pltpu.CompilerParams(dimension_semantics=("arbitrary",) * n_axes,
                                vmem_limit_bytes=VMEM_LIMIT)


def _token_tile(m):
    return 512 if m % 512 == 0 else m


def _ffn1(x, g1, w1i, w1o, gmix):
    m = x.shape[0]
    tm = _token_tile(m)
    return pl.pallas_call(
        _ffn1_kernel,
        grid=(m // tm,),
        in_specs=[_rows(tm, D_MODEL), _resident(g1.shape), _resident(w1i.shape),
                  _resident(w1o.shape), _resident(gmix.shape)],
        out_specs=[_rows(tm, D_MODEL), _rows(tm, D_MODEL)],
        out_shape=[jax.ShapeDtypeStruct((m, D_MODEL), F32),
                   jax.ShapeDtypeStruct((m, D_MODEL), BF16)],
        compiler_params=_params(1),
        name="ffn1",
    )(x, g1, w1i, w1o, gmix)


def _inproj(h, w_in, wgu, bg, lng, lnb, out_dtype):
    m = h.shape[0]
    tm = _token_tile(m)
    small = [wgu, bg, lng, lnb]
    return pl.pallas_call(
        _inproj_kernel,
        grid=(m // tm,),
        in_specs=[_rows(tm, D_MODEL), _resident(w_in.shape)] + [_resident(w.shape) for w in small],
        out_specs=[_rows(tm, P_COLS), _rows(tm, B_KW)],
        out_shape=[jax.ShapeDtypeStruct((m, P_COLS), out_dtype),
                   jax.ShapeDtypeStruct((m, B_KW), F32)],
        scratch_shapes=[pltpu.VMEM((D_MODEL, 2 * D_MODEL), BF16)],
        compiler_params=_params(1),
        name="inproj",
    )(h, w_in, *small)


def _mixer(p, la, n_seq, seq, ws, bsp_t, ggla):
    tt = 2 * CHUNK
    nt = seq // tt
    m = n_seq * seq

    def seg(colblk):
        return pl.BlockSpec((tt, D_MODEL), lambda b, t: (b * nt + t, colblk))

    small = [ws, bsp_t, ggla]
    return pl.pallas_call(
        functools.partial(_mixer_kernel, n_chunks=tt // CHUNK),
        grid=(n_seq, nt),
        in_specs=[seg(0), seg(1), seg(2), seg(3),
                  pl.BlockSpec((tt, B_KW), lambda b, t: (b * nt + t, 0))]
                 + [_resident(w.shape) for w in small],
        out_specs=[pl.BlockSpec((tt, A_WIDTH), lambda b, t: (b * nt + t, 0)),
                   pl.BlockSpec((tt, B_VW), lambda b, t: (b * nt + t, 0)),
                   pl.BlockSpec((1, B_HEADS, B_DK, B_DV), lambda b, t: (b, 0, 0, 0))],
        out_shape=[jax.ShapeDtypeStruct((m, A_WIDTH), BF16),
                   jax.ShapeDtypeStruct((m, B_VW), BF16),
                   jax.ShapeDtypeStruct((n_seq, B_HEADS, B_DK, B_DV), F32)],
        scratch_shapes=[pltpu.VMEM((B_HEADS, B_DK, B_DV), F32)],
        compiler_params=_params(2),
        name="mixer",
    )(p, p, p, p, la, *small)


def _sample_mixer(p, la, state, ws0, bs0, ggla):
    m = p.shape[0]
    nb = 8

    def seg(colblk):
        return pl.BlockSpec((nb, D_MODEL), lambda i: (i, colblk))

    state_spec = pl.BlockSpec((nb, B_HEADS, B_DK, B_DV), lambda i: (i, 0, 0, 0))
    small = [ws0, bs0, ggla]
    return pl.pallas_call(
        functools.partial(_sample_mixer_kernel, nb=nb),
        grid=(m // nb,),
        in_specs=[seg(0), seg(1), seg(2), seg(3), _rows(nb, B_KW), state_spec]
                 + [_resident(w.shape) for w in small],
        out_specs=[_rows(nb, A_WIDTH), _rows(nb, B_VW), state_spec, _rows(nb, A_WIDTH)],
        out_shape=[jax.ShapeDtypeStruct((m, A_WIDTH), F32),
                   jax.ShapeDtypeStruct((m, B_VW), F32),
                   jax.ShapeDtypeStruct(state.shape, F32),
                   jax.ShapeDtypeStruct((m, A_WIDTH), F32)],
        scratch_shapes=[pltpu.VMEM((nb, B_VW), F32)],
        compiler_params=_params(1),
        name="sample_mixer",
    )(p, p, p, p, la, state, *small)


def _merge(ya, yb, p, x1, wpa, wpb, wo):
    m = x1.shape[0]
    tm = _token_tile(m)
    return pl.pallas_call(
        _merge_kernel,
        grid=(m // tm,),
        in_specs=[_rows(tm, A_WIDTH), _rows(tm, B_VW), _rows(tm, D_MODEL, 4),
                  _rows(tm, D_MODEL, 5), _rows(tm, D_MODEL), _resident(wpa.shape),
                  _resident(wpb.shape), _resident(wo.shape)],
        out_specs=_rows(tm, D_MODEL),
        out_shape=jax.ShapeDtypeStruct((m, D_MODEL), F32),
        compiler_params=_params(1),
        name="merge",
    )(ya, yb, p, p, x1, wpa, wpb, wo)


def _ffn2(x, p, g2, w2i, w2o, gple, wpg, wple, gfin, final_norm):
    m = x.shape[0]
    tm = _token_tile(m)
    weights = [g2, w2i, w2o, gple, wpg, wple, gfin]
    return pl.pallas_call(
        functools.partial(_ffn2_kernel, final_norm=final_norm),
        grid=(m // tm,),
        in_specs=[_rows(tm, D_MODEL), _rows(tm, PLE_DIM)] + [_resident(w.shape) for w in weights],
        out_specs=_rows(tm, D_MODEL),
        out_shape=jax.ShapeDtypeStruct((m, D_MODEL), F32),
        compiler_params=_params(1),
        name="ffn2",
    )(x, p, *weights)


def kernel(x_prompt, x_sample, p_prompt, p_sample, state_gla, g_ffn1, w_ffn1_in, w_ffn1_out, g_mix,
           w_in, ln_v_g, ln_v_b, w_spatial, b_spatial, w_gate_up, b_gate, g_gla_out, w_proj_a,
           w_proj_b, w_out, g_ffn2, w_ffn2_in, w_ffn2_out, g_ple, w_ple_gate, w_ple, g_final):
    depth = w_in.shape[0]
    n_seq, seq, _ = x_prompt.shape
    n_dec, dec_seq, _ = x_sample.shape
    assert dec_seq == 1 and seq % (2 * CHUNK) == 0

    def row(vec):
        return vec.reshape(1, -1).astype(F32)

    assert sum(IN_SIZES[:6]) == LR_LO and sum(IN_SIZES) == GATE_LO + 2 * D_MODEL
    xp = x_prompt.reshape(n_seq * seq, D_MODEL)
    xs = x_sample.reshape(n_dec, D_MODEL)
    gfin = row(g_final)
    sp_list, ss_list, vs_list = [], [], []
    for i in range(depth):
        w1i, w1o = w_ffn1_in[i].astype(BF16), w_ffn1_out[i].astype(BF16)
        w2i, w2o = w_ffn2_in[i].astype(BF16), w_ffn2_out[i].astype(BF16)
        w_inb = w_in[i].astype(BF16)
        wgu = jnp.pad(w_gate_up[i], ((0, LANES - GATE_RANK), (0, 0))).astype(BF16)
        wpa, wpb, wo = (w.astype(BF16) for w in (w_proj_a[i], w_proj_b[i], w_out[i]))
        wpg, wple = w_ple_gate[i].astype(BF16), w_ple[i].astype(BF16)
        g1, gmix, g2, gple = row(g_ffn1[i]), row(g_mix[i]), row(g_ffn2[i]), row(g_ple[i])
        lng, lnb, bg, ggla = row(ln_v_g[i]), row(ln_v_b[i]), row(b_gate[i]), row(g_gla_out[i])
        ws0 = row(jnp.repeat(w_spatial[i][:, 0, 0], A_GROUP_DIM))
        bs0 = row(jnp.repeat(b_spatial[i][:, 0], A_GROUP_DIM))

        x1, h = _ffn1(xp, g1, w1i, w1o, gmix)
        p, la = _inproj(h, w_inb, wgu, bg, lng, lnb, BF16)
        ya, yb, sp = _mixer(p, la, n_seq, seq, w_spatial[i], b_spatial[i].T, ggla)
        x2 = _merge(ya, yb, p, x1, wpa, wpb, wo)
        pp = p_prompt[i].reshape(n_seq * seq, PLE_DIM)
        last = i == depth - 1
        xp = _ffn2(x2, pp, g2, w2i, w2o, gple, wpg, wple, gfin, last)

        x1, h = _ffn1(xs, g1, w1i, w1o, gmix)
        p, la = _inproj(h, w_inb, wgu, bg, lng, lnb, F32)
        ya, yb, ss, cv = _sample_mixer(p, la, state_gla[i], ws0, bs0, ggla)
        x2 = _merge(ya.astype(BF16), yb.astype(BF16), p, x1, wpa, wpb, wo)
        ps = p_sample[i].reshape(n_dec, PLE_DIM)
        xs = _ffn2(x2, ps, g2, w2i, w2o, gple, wpg, wple, gfin, last)

        sp_list.append(sp)
        ss_list.append(ss)
        vs_list.append(cv.reshape(n_dec, dec_seq, A_WIDTH))
    return (xp.reshape(n_seq, seq, D_MODEL), xs.reshape(n_dec, dec_seq, D_MODEL),
            jnp.stack(sp_list), jnp.stack(ss_list), jnp.stack(vs_list))
```

```python
import functools

import jax
import jax.numpy as jnp
from jax import lax
from jax.experimental import pallas as pl
from jax.experimental.pallas import tpu as pltpu

D_MODEL = 1024
D_FF = 2816
PLE_DIM = 256
CHUNK = 128
A_GROUPS = 4
A_GROUP_DIM = 128
A_WIDTH = A_GROUPS * A_GROUP_DIM
B_HEADS = 4
B_DK = 128
B_DV = 256
B_KW = B_HEADS * B_DK
B_VW = B_HEADS * B_DV
GATE_RANK = 16
GATE_NORM = 16.0
EPS = 1e-6
IN_SIZES = (A_WIDTH, A_WIDTH, B_KW, B_KW, B_VW, B_VW, GATE_RANK, D_MODEL, D_MODEL)

LANES = 128
P_COLS = 6 * D_MODEL
QK_LO = 2 * A_WIDTH
VB_LO = QK_LO + 2 * B_KW
GB_LO = VB_LO + B_VW
LR_LO = GB_LO + B_VW
GATE_LO = LR_LO + GATE_RANK
FF_CHUNK = 256
HALF = CHUNK // 2
MIXER_TILE = 4 * CHUNK
VMEM_LIMIT = 56 * 1024 * 1024

F32 = jnp.float32
BF16 = jnp.bfloat16


def _dot(a, b):
    return jnp.dot(a, b, preferred_element_type=F32)


def _dot_nt(a, b):
    return lax.dot_general(a, b, (((1,), (1,)), ((), ())), preferred_element_type=F32)


def _rms(x, g):
    return x * lax.rsqrt(jnp.mean(x * x, axis=-1, keepdims=True) + EPS) * g


def _log_sigmoid(x):
    return jnp.minimum(x, 0.0) - jnp.log1p(jnp.exp(-jnp.abs(x)))


def _layernorm(x, g, b):
    mu = jnp.mean(x, axis=-1, keepdims=True)
    xc = x - mu
    var = jnp.mean(xc * xc, axis=-1, keepdims=True)
    return xc * lax.rsqrt(var + EPS) * g + b


def _swiglu_residual(x, g_ref, w_in_ref, w_out_ref):
    h = _rms(x, g_ref[...]).astype(BF16)
    acc = None
    for lo in range(0, D_FF, FF_CHUNK):
        gate = _dot(h, w_in_ref[:, lo:lo + FF_CHUNK])
        up = _dot(h, w_in_ref[:, D_FF + lo:D_FF + lo + FF_CHUNK])
        act = (jax.nn.silu(gate) * up).astype(BF16)
        part = _dot(act, w_out_ref[lo:lo + FF_CHUNK, :])
        acc = part if acc is None else acc + part
    return x + 0.5 * acc


def _ffn1_kernel(x_ref, g1_ref, w1i_ref, w1o_ref, gmix_ref, x1_ref, h_ref):
    x1 = _swiglu_residual(x_ref[...], g1_ref, w1i_ref, w1o_ref)
    x1_ref[...] = x1
    h_ref[...] = _rms(x1, gmix_ref[...]).astype(BF16)


def _inproj_kernel(h_ref, w_ref, wgu_ref, bg_ref, lng_ref, lnb_ref, p_ref, la_ref, wgate_scr):
    @pl.when(pl.program_id(0) == 0)
    def _():
        wgate_scr[...] = w_ref[:, GATE_LO:GATE_LO + 2 * D_MODEL]

    def put(lo, val):
        p_ref[:, lo:lo + val.shape[1]] = val.astype(p_ref.dtype)

    h = h_ref[...]
    uv = _dot(h, w_ref[:, 0:2 * A_WIDTH])
    put(0, jax.nn.gelu(uv[:, :A_WIDTH]))
    put(A_WIDTH, _layernorm(jax.nn.gelu(uv[:, A_WIDTH:]), lng_ref[...], lnb_ref[...]))
    qk = _dot(h, w_ref[:, QK_LO:QK_LO + 2 * B_KW])
    put(QK_LO, qk[:, :B_KW] * (B_DK ** -0.5))
    put(QK_LO + B_KW, qk[:, B_KW:])
    put(VB_LO, _dot(h, w_ref[:, VB_LO:VB_LO + B_VW]))
    put(GB_LO, jax.nn.silu(_dot(h, w_ref[:, GB_LO:GB_LO + B_VW])))
    lr = _dot(h, w_ref[:, LR_LO:LR_LO + LANES]).astype(BF16)
    logit = _dot(lr, wgu_ref[...]) + bg_ref[...]
    la_ref[...] = _log_sigmoid(logit) * (1.0 / GATE_NORM)
    for j in range(2):
        gate = _dot(h, wgate_scr[:, j * D_MODEL:(j + 1) * D_MODEL])
        put(LR_LO + j * D_MODEL, jax.nn.sigmoid(gate))


def _mixer_kernel(uv_ref, qk_ref, v_ref, gb_ref, la_ref, sga_ref, sgb_ref, x1_ref, ws_ref, bsp_ref,
                  ggla_ref, wpa_ref, wpb_ref, wo_ref, x2_ref, sout_ref, s_scr, *, n_chunks):
    t = pl.program_id(1)

    @pl.when(t == 0)
    def _():
        s_scr[...] = jnp.zeros_like(s_scr)

    row = lax.broadcasted_iota(jnp.int32, (CHUNK, CHUNK), 0)
    col = lax.broadcasted_iota(jnp.int32, (CHUNK, CHUNK), 1)
    causal = row >= col
    ones_tril = causal.astype(BF16)
    ws_tril = [jnp.where(causal, ws_ref[g], 0.0).astype(BF16) for g in range(A_GROUPS)]
    zero_blk = jnp.zeros((HALF, B_DK), BF16)

    def chunk_mixers(rows):
        u = uv_ref[rows, :A_WIDTH].astype(F32)
        vn = uv_ref[rows, A_WIDTH:]
        ya = []
        for g in range(A_GROUPS):
            cols = slice(g * A_GROUP_DIM, (g + 1) * A_GROUP_DIM)
            mixed = _dot(ws_tril[g], vn[:, cols]) + bsp_ref[:, g:g + 1]
            ya.append((u[:, cols] * mixed).astype(BF16))
        yield

        la = la_ref[rows, :]
        la_hi = la.astype(BF16)
        la_lo = (la - la_hi.astype(F32)).astype(BF16)
        b = _dot(ones_tril, la_hi) + _dot(ones_tril, la_lo)

        r0 = b[HALF // 2 - 1:HALF // 2, :]
        r1 = b[HALF + HALF // 2 - 1:HALF + HALF // 2, :]
        b_mid = b[HALF - 1:HALF, :]
        b_end = b[CHUNK - 1:CHUNK, :]
        bq0 = b[:HALF] - r0
        bq1 = b[HALF:] - r1
        qk = qk_ref[rows, :].astype(F32)
        qs = qk[:, :B_KW]
        k = qk[:, B_KW:]
        a0 = qs[:HALF] * jnp.exp(bq0)
        a1 = qs[HALF:] * jnp.exp(bq1)
        k0 = k[:HALF] * jnp.exp(-bq0)
        k1 = k[HALF:] * jnp.exp(-bq1)
        q_off1 = a1 * jnp.exp(r1 - b_mid)
        k_off0 = k0 * jnp.exp(b_mid - r0)
        q_int = jnp.concatenate([a0 * jnp.exp(r0), q_off1 * jnp.exp(b_mid)], axis=0).astype(BF16)
        k_st = jnp.concatenate([k_off0 * jnp.exp(b_end - b_mid), k1 * jnp.exp(b_end - r1)], axis=0)
        decay = jnp.exp(b_end)
        a0, a1, k0, k1 = (z.astype(BF16) for z in (a0, a1, k0, k1))
        q_off1 = q_off1.astype(BF16)
        k_off0 = k_off0.astype(BF16)
        yield

        v = v_ref[rows, :]
        gb = gb_ref[rows, :].astype(F32)
        yb = []
        for h in range(B_HEADS):
            kc = slice(h * B_DK, (h + 1) * B_DK)
            vc = slice(h * B_DV, (h + 1) * B_DV)
            q3 = jnp.concatenate([
                jnp.concatenate([a0[:, kc], zero_blk, zero_blk], axis=1),
                jnp.concatenate([zero_blk, q_off1[:, kc], a1[:, kc]], axis=1)], axis=0)
            k3 = jnp.concatenate([
                jnp.concatenate([k0[:, kc], k_off0[:, kc], zero_blk], axis=1),
                jnp.concatenate([zero_blk, zero_blk, k1[:, kc]], axis=1)], axis=0)
            scores = jnp.where(causal, _dot_nt(q3, k3), 0.0).astype(BF16)
            s_prev = s_scr[h]
            v_h = v[:, vc]
            o = _dot(jnp.concatenate([scores, q_int[:, kc]], axis=1),
                     jnp.concatenate([v_h, s_prev.astype(BF16)], axis=0))
            upd = _dot(k_st[:, kc].T.astype(BF16), v_h)
            decay_col = jnp.broadcast_to(decay[:, kc], (B_DK, B_DK)).T
            s_scr[h] = s_prev * jnp.concatenate([decay_col, decay_col], axis=1) + upd
            o_n = _rms(o, ggla_ref[...])
            yb.append((o_n * gb[:, vc]).astype(BF16))
            yield
        return jnp.concatenate(ya, axis=1), jnp.concatenate(yb, axis=1)

    def merge_tasks(rows, ya, yb):
        half = D_MODEL // 2
        vals = {}

        def proj(name, lhs, w_ref, cols):
            def run():
                vals[name] = _dot(lhs, w_ref[:, cols])
            return run

        def mix(j):
            cols = slice(j * half, (j + 1) * half)

            def run():
                vals["mix", j] = (sga_ref[rows, cols].astype(F32) * vals["a", j]
                                  + sgb_ref[rows, cols].astype(F32) * vals["b", j]).astype(BF16)
            return run

        def out(j):
            cols = slice(j * half, (j + 1) * half)

            def run():
                m = jnp.concatenate([vals["mix", 0], vals["mix", 1]], axis=1)
                x2_ref[rows, cols] = x1_ref[rows, cols] + _dot(m, wo_ref[:, cols])
            return run

        tasks = []
        for j in range(2):
            cols = slice(j * half, (j + 1) * half)
            tasks += [proj(("a", j), ya, wpa_ref, cols), proj(("b", j), yb, wpb_ref, cols), mix(j)]
        return tasks + [out(0), out(1)]

    pending = []
    for c in range(n_chunks):
        rows = slice(c * CHUNK, (c + 1) * CHUNK)
        stages = chunk_mixers(rows)
        while True:
            try:
                next(stages)
            except StopIteration as done:
                ya, yb = done.value
                break
            for task in pending[:2]:
                task()
            pending = pending[2:]
        for task in pending:
            task()
        pending = merge_tasks(rows, ya, yb)
    for task in pending:
        task()

    @pl.when(t == pl.num_programs(1) - 1)
    def _():
        sout_ref[0] = s_scr[...]


def _sample_mixer_kernel(uv_ref, qk_ref, v_ref, gb_ref, la_ref, s_ref, ws0_ref, bs0_ref, ggla_ref,
                         ya_ref, yb_ref, sout_ref, cv_ref, o_scr, *, nb):
    vn = uv_ref[:, A_WIDTH:]
    cv_ref[...] = vn
    ya_ref[...] = uv_ref[:, :A_WIDTH] * (vn * ws0_ref[...] + bs0_ref[...])

    a = jnp.exp(la_ref[...])
    qk = qk_ref[...]
    qs = qk[:, :B_KW]
    k = qk[:, B_KW:]
    v = v_ref[...]
    pad = jnp.zeros((LANES - 3 * nb, B_DK), F32)
    for h in range(B_HEADS):
        kc = slice(h * B_DK, (h + 1) * B_DK)
        vc = slice(h * B_DV, (h + 1) * B_DV)
        xt = jnp.concatenate([a[:, kc], k[:, kc], qs[:, kc], pad], axis=0).T
        for n in range(nb):
            s_new = (s_ref[n, h] * xt[:, n:n + 1]
                     + xt[:, nb + n:nb + n + 1] * v[n:n + 1, vc])
            sout_ref[n, h] = s_new
            o_scr[n:n + 1, vc] = jnp.sum(xt[:, 2 * nb + n:2 * nb + n + 1] * s_new,
                                         axis=0, keepdims=True)
    gb = gb_ref[...]
    for h in range(B_HEADS):
        vc = slice(h * B_DV, (h + 1) * B_DV)
        yb_ref[:, vc] = _rms(o_scr[:, vc], ggla_ref[...]) * gb[:, vc]


def _merge_kernel(ya_ref, yb_ref, ga_ref, gb_ref, x1_ref, wpa_ref, wpb_ref, wo_ref, x2_ref):
    mix = (ga_ref[...].astype(F32) * _dot(ya_ref[...], wpa_ref[...])
           + gb_ref[...].astype(F32) * _dot(yb_ref[...], wpb_ref[...]))
    x2_ref[...] = x1_ref[...] + _dot(mix.astype(BF16), wo_ref[...])


def _ffn2_kernel(x_ref, p_ref, g2_ref, w2i_ref, w2o_ref, gple_ref, wpg_ref, wple_ref, gfin_ref,
                 y_ref, *, final_norm):
    x3 = _swiglu_residual(x_ref[...], g2_ref, w2i_ref, w2o_ref)
    gate = jax.nn.sigmoid(_dot(_rms(x3, gple_ref[...]).astype(BF16), wpg_ref[...]))
    x4 = x3 + _dot(p_ref[...].astype(BF16), wple_ref[...]) * gate
    y_ref[...] = _rms(x4, gfin_ref[...]) if final_norm else x4


def _resident(shape):
    zeros = (0,) * len(shape)
    return pl.BlockSpec(shape, lambda *_: zeros, pipeline_mode=pl.Buffered(1))


def _rows(tm, width, colblk=0):
    return pl.BlockSpec((tm, width), lambda i: (i, colblk))


def _params(n_axes):
    return pltpu.CompilerParams(dimension_semantics=("arbitrary",) * n_axes,
                                vmem_limit_bytes=VMEM_LIMIT)


def _token_tile(m):
    return 512 if m % 512 == 0 else m


def _ffn1(x, g1, w1i, w1o, gmix):
    m = x.shape[0]
    tm = _token_tile(m)
    return pl.pallas_call(
        _ffn1_kernel,
        grid=(m // tm,),
        in_specs=[_rows(tm, D_MODEL), _resident(g1.shape), _resident(w1i.shape),
                  _resident(w1o.shape), _resident(gmix.shape)],
        out_specs=[_rows(tm, D_MODEL), _rows(tm, D_MODEL)],
        out_shape=[jax.ShapeDtypeStruct((m, D_MODEL), F32),
                   jax.ShapeDtypeStruct((m, D_MODEL), BF16)],
        compiler_params=_params(1),
        name="ffn1",
    )(x, g1, w1i, w1o, gmix)


def _inproj(h, w_in, wgu, bg, lng, lnb, out_dtype):
    m = h.shape[0]
    tm = _token_tile(m)
    small = [wgu, bg, lng, lnb]
    return pl.pallas_call(
        _inproj_kernel,
        grid=(m // tm,),
        in_specs=[_rows(tm, D_MODEL), _resident(w_in.shape)] + [_resident(w.shape) for w in small],
        out_specs=[_rows(tm, P_COLS), _rows(tm, B_KW)],
        out_shape=[jax.ShapeDtypeStruct((m, P_COLS), out_dtype),
                   jax.ShapeDtypeStruct((m, B_KW), F32)],
        scratch_shapes=[pltpu.VMEM((D_MODEL, 2 * D_MODEL), BF16)],
        compiler_params=_params(1),
        name="inproj",
    )(h, w_in, *small)


def _mixer(p, la, x1, n_seq, seq, ws, bsp_t, ggla, wpa, wpb, wo):
    tt = MIXER_TILE
    nt = seq // tt
    m = n_seq * seq

    def seg(colblk, width=D_MODEL):
        return pl.BlockSpec((tt, width), lambda b, t: (b * nt + t, colblk))

    small = [ws, bsp_t, ggla, wpa, wpb, wo]
    return pl.pallas_call(
        functools.partial(_mixer_kernel, n_chunks=tt // CHUNK),
        grid=(n_seq, nt),
        in_specs=[seg(0), seg(1), seg(2), seg(3), seg(0, B_KW), seg(4), seg(5), seg(0)]
                 + [_resident(w.shape) for w in small],
        out_specs=[seg(0),
                   pl.BlockSpec((1, B_HEADS, B_DK, B_DV), lambda b, t: (b, 0, 0, 0))],
        out_shape=[jax.ShapeDtypeStruct((m, D_MODEL), F32),
                   jax.ShapeDtypeStruct((n_seq, B_HEADS, B_DK, B_DV), F32)],
        scratch_shapes=[pltpu.VMEM((B_HEADS, B_DK, B_DV), F32)],
        compiler_params=_params(2),
        name="mixer",
    )(p, p, p, p, la, p, p, x1, *small)


def _sample_mixer(p, la, state, ws0, bs0, ggla):
    m = p.shape[0]
    nb = 8

    def seg(colblk):
        return pl.BlockSpec((nb, D_MODEL), lambda i: (i, colblk))

    state_spec = pl.BlockSpec((nb, B_HEADS, B_DK, B_DV), lambda i: (i, 0, 0, 0))
    small = [ws0, bs0, ggla]
    return pl.pallas_call(
        functools.partial(_sample_mixer_kernel, nb=nb),
        grid=(m // nb,),
        in_specs=[seg(0), seg(1), seg(2), seg(3), _rows(nb, B_KW), state_spec]
                 + [_resident(w.shape) for w in small],
        out_specs=[_rows(nb, A_WIDTH), _rows(nb, B_VW), state_spec, _rows(nb, A_WIDTH)],
        out_shape=[jax.ShapeDtypeStruct((m, A_WIDTH), F32),
                   jax.ShapeDtypeStruct((m, B_VW), F32),
                   jax.ShapeDtypeStruct(state.shape, F32),
                   jax.ShapeDtypeStruct((m, A_WIDTH), F32)],
        scratch_shapes=[pltpu.VMEM((nb, B_VW), F32)],
        compiler_params=_params(1),
        name="sample_mixer",
    )(p, p, p, p, la, state, *small)


def _merge(ya, yb, p, x1, wpa, wpb, wo):
    m = x1.shape[0]
    tm = _token_tile(m)
    return pl.pallas_call(
        _merge_kernel,
        grid=(m // tm,),
        in_specs=[_rows(tm, A_WIDTH), _rows(tm, B_VW), _rows(tm, D_MODEL, 4),
                  _rows(tm, D_MODEL, 5), _rows(tm, D_MODEL), _resident(wpa.shape),
                  _resident(wpb.shape), _resident(wo.shape)],
        out_specs=_rows(tm, D_MODEL),
        out_shape=jax.ShapeDtypeStruct((m, D_MODEL), F32),
        compiler_params=_params(1),
        name="merge",
    )(ya, yb, p, p, x1, wpa, wpb, wo)


def _ffn2(x, p, g2, w2i, w2o, gple, wpg, wple, gfin, final_norm):
    m = x.shape[0]
    tm = _token_tile(m)
    weights = [g2, w2i, w2o, gple, wpg, wple, gfin]
    return pl.pallas_call(
        functools.partial(_ffn2_kernel, final_norm=final_norm),
        grid=(m // tm,),
        in_specs=[_rows(tm, D_MODEL), _rows(tm, PLE_DIM)] + [_resident(w.shape) for w in weights],
        out_specs=_rows(tm, D_MODEL),
        out_shape=jax.ShapeDtypeStruct((m, D_MODEL), F32),
        compiler_params=_params(1),
        name="ffn2",
    )(x, p, *weights)


def kernel(x_prompt, x_sample, p_prompt, p_sample, state_gla, g_ffn1, w_ffn1_in, w_ffn1_out, g_mix,
           w_in, ln_v_g, ln_v_b, w_spatial, b_spatial, w_gate_up, b_gate, g_gla_out, w_proj_a,
           w_proj_b, w_out, g_ffn2, w_ffn2_in, w_ffn2_out, g_ple, w_ple_gate, w_ple, g_final):
    depth = w_in.shape[0]
    n_seq, seq, _ = x_prompt.shape
    n_dec, dec_seq, _ = x_sample.shape
    assert dec_seq == 1 and seq % MIXER_TILE == 0

    def row(vec):
        return vec.reshape(1, -1).astype(F32)

    assert sum(IN_SIZES[:6]) == LR_LO and sum(IN_SIZES) == GATE_LO + 2 * D_MODEL
    xp = x_prompt.reshape(n_seq * seq, D_MODEL)
    xs = x_sample.reshape(n_dec, D_MODEL)
    gfin = row(g_final)
    sp_list, ss_list, vs_list = [], [], []
    for i in range(depth):
        w1i, w1o = w_ffn1_in[i].astype(BF16), w_ffn1_out[i].astype(BF16)
        w2i, w2o = w_ffn2_in[i].astype(BF16), w_ffn2_out[i].astype(BF16)
        w_inb = w_in[i].astype(BF16)
        wgu = jnp.pad(w_gate_up[i], ((0, LANES - GATE_RANK), (0, 0))).astype(BF16)
        wpa, wpb, wo = (w.astype(BF16) for w in (w_proj_a[i], w_proj_b[i], w_out[i]))
        wpg, wple = w_ple_gate[i].astype(BF16), w_ple[i].astype(BF16)
        g1, gmix, g2, gple = row(g_ffn1[i]), row(g_mix[i]), row(g_ffn2[i]), row(g_ple[i])
        lng, lnb, bg, ggla = row(ln_v_g[i]), row(ln_v_b[i]), row(b_gate[i]), row(g_gla_out[i])
        ws0 = row(jnp.repeat(w_spatial[i][:, 0, 0], A_GROUP_DIM))
        bs0 = row(jnp.repeat(b_spatial[i][:, 0], A_GROUP_DIM))

        x1, h = _ffn1(xp, g1, w1i, w1o, gmix)
        p, la = _inproj(h, w_inb, wgu, bg, lng, lnb, BF16)
        x2, sp = _mixer(p, la, x1, n_seq, seq, w_spatial[i], b_spatial[i].T, ggla, wpa, wpb, wo)
        pp = p_prompt[i].reshape(n_seq * seq, PLE_DIM)
        last = i == depth - 1
        xp = _ffn2(x2, pp, g2, w2i, w2o, gple, wpg, wple, gfin, last)

        x1, h = _ffn1(xs, g1, w1i, w1o, gmix)
        p, la = _inproj(h, w_inb, wgu, bg, lng, lnb, F32)
        ya, yb, ss, cv = _sample_mixer(p, la, state_gla[i], ws0, bs0, ggla)
        x2 = _merge(ya.astype(BF16), yb.astype(BF16), p, x1, wpa, wpb, wo)
        ps = p_sample[i].reshape(n_dec, PLE_DIM)
        xs = _ffn2(x2, ps, g2, w2i, w2o, gple, wpg, wple, gfin, last)

        sp_list.append(sp)
        ss_list.append(ss)
        vs_list.append(cv.reshape(n_dec, dec_seq, A_WIDTH))
    return (xp.reshape(n_seq, seq, D_MODEL), xs.reshape(n_dec, dec_seq, D_MODEL),
            jnp.stack(sp_list), jnp.stack(ss_list), jnp.stack(vs_list))
```

```python
import functools

import jax
import jax.numpy as jnp
from jax import lax
from jax.experimental import pallas as pl
from jax.experimental.pallas import tpu as pltpu

D_MODEL = 1024
D_FF = 2816
PLE_DIM = 256
CHUNK = 128
A_GROUPS = 4
A_GROUP_DIM = 128
A_WIDTH = A_GROUPS * A_GROUP_DIM
B_HEADS = 4
B_DK = 128
B_DV = 256
B_KW = B_HEADS * B_DK
B_VW = B_HEADS * B_DV
GATE_RANK = 16
GATE_NORM = 16.0
EPS = 1e-6
IN_SIZES = (A_WIDTH, A_WIDTH, B_KW, B_KW, B_VW, B_VW, GATE_RANK, D_MODEL, D_MODEL)

LANES = 128
P_COLS = 6 * D_MODEL
QK_LO = 2 * A_WIDTH
VB_LO = QK_LO + 2 * B_KW
GB_LO = VB_LO + B_VW
LR_LO = GB_LO + B_VW
GATE_LO = LR_LO + GATE_RANK
FF_CHUNK = 256
HALF = CHUNK // 2
MIXER_TILE = 4 * CHUNK
VMEM_LIMIT = 56 * 1024 * 1024

F32 = jnp.float32
BF16 = jnp.bfloat16


def _dot(a, b):
    return jnp.dot(a, b, preferred_element_type=F32)


def _dot_nt(a, b):
    return lax.dot_general(a, b, (((1,), (1,)), ((), ())), preferred_element_type=F32)


def _rms(x, g):
    return x * lax.rsqrt(jnp.mean(x * x, axis=-1, keepdims=True) + EPS) * g


def _log_sigmoid(x):
    return jnp.minimum(x, 0.0) - jnp.log1p(jnp.exp(-jnp.abs(x)))


def _layernorm(x, g, b):
    mu = jnp.mean(x, axis=-1, keepdims=True)
    xc = x - mu
    var = jnp.mean(xc * xc, axis=-1, keepdims=True)
    return xc * lax.rsqrt(var + EPS) * g + b


def _swiglu_residual(x, g_ref, w_in_ref, w_out_ref):
    h = _rms(x, g_ref[...]).astype(BF16)
    acc = None
    for lo in range(0, D_FF, FF_CHUNK):
        gate = _dot(h, w_in_ref[:, lo:lo + FF_CHUNK])
        up = _dot(h, w_in_ref[:, D_FF + lo:D_FF + lo + FF_CHUNK])
        act = (jax.nn.silu(gate) * up).astype(BF16)
        part = _dot(act, w_out_ref[lo:lo + FF_CHUNK, :])
        acc = part if acc is None else acc + part
    return x + 0.5 * acc


def _ffn1_kernel(x_ref, g1_ref, w1i_ref, w1o_ref, gmix_ref, x1_ref, h_ref):
    x1 = _swiglu_residual(x_ref[...], g1_ref, w1i_ref, w1o_ref)
    x1_ref[...] = x1
    h_ref[...] = _rms(x1, gmix_ref[...]).astype(BF16)


def _inproj_kernel(h_ref, w_ref, wgu_ref, bg_ref, lng_ref, lnb_ref, p_ref, la_ref, wgate_scr):
    @pl.when(pl.program_id(0) == 0)
    def _():
        wgate_scr[...] = w_ref[:, GATE_LO:GATE_LO + 2 * D_MODEL]

    def put(lo, val):
        p_ref[:, lo:lo + val.shape[1]] = val.astype(p_ref.dtype)

    h = h_ref[...]
    uv = _dot(h, w_ref[:, 0:2 * A_WIDTH])
    put(0, jax.nn.gelu(uv[:, :A_WIDTH]))
    put(A_WIDTH, _layernorm(jax.nn.gelu(uv[:, A_WIDTH:]), lng_ref[...], lnb_ref[...]))
    qk = _dot(h, w_ref[:, QK_LO:QK_LO + 2 * B_KW])
    put(QK_LO, qk[:, :B_KW] * (B_DK ** -0.5))
    put(QK_LO + B_KW, qk[:, B_KW:])
    put(VB_LO, _dot(h, w_ref[:, VB_LO:VB_LO + B_VW]))
    put(GB_LO, jax.nn.silu(_dot(h, w_ref[:, GB_LO:GB_LO + B_VW])))
    lr = _dot(h, w_ref[:, LR_LO:LR_LO + LANES]).astype(BF16)
    logit = _dot(lr, wgu_ref[...]) + bg_ref[...]
    la_ref[...] = _log_sigmoid(logit) * (1.0 / GATE_NORM)
    for j in range(2):
        gate = _dot(h, wgate_scr[:, j * D_MODEL:(j + 1) * D_MODEL])
        put(LR_LO + j * D_MODEL, jax.nn.sigmoid(gate))


def _gla_prep(la, qk, ones_tril):
    la_hi = la.astype(BF16)
    la_lo = (la - la_hi.astype(F32)).astype(BF16)
    b = _dot(ones_tril, la_hi) + _dot(ones_tril, la_lo)
    r0 = b[HALF // 2 - 1:HALF // 2, :]
    r1 = b[HALF + HALF // 2 - 1:HALF + HALF // 2, :]
    b_mid = b[HALF - 1:HALF, :]
    b_end = b[CHUNK - 1:CHUNK, :]
    bq0 = b[:HALF] - r0
    bq1 = b[HALF:] - r1
    qs = qk[:, :B_KW]
    k = qk[:, B_KW:]
    a0 = qs[:HALF] * jnp.exp(bq0)
    a1 = qs[HALF:] * jnp.exp(bq1)
    k0 = k[:HALF] * jnp.exp(-bq0)
    k1 = k[HALF:] * jnp.exp(-bq1)
    q_off1 = a1 * jnp.exp(r1 - b_mid)
    k_off0 = k0 * jnp.exp(b_mid - r0)
    q_int = jnp.concatenate([a0 * jnp.exp(r0), q_off1 * jnp.exp(b_mid)], axis=0)
    k_st = jnp.concatenate([k_off0 * jnp.exp(b_end - b_mid), k1 * jnp.exp(b_end - r1)], axis=0)
    return dict(a0=a0.astype(BF16), a1=a1.astype(BF16), k0=k0.astype(BF16), k1=k1.astype(BF16),
                q_off1=q_off1.astype(BF16), k_off0=k_off0.astype(BF16), q_int=q_int.astype(BF16),
                k_st=k_st, decay=jnp.exp(b_end))


def _mixer_kernel(uv_ref, qk_ref, v_ref, gb_ref, la_ref, sga_ref, sgb_ref, x1_ref, ws_ref, bsp_ref,
                  ggla_ref, wpa_ref, wpb_ref, wo_ref, x2_ref, sout_ref, s_scr, *, n_chunks):
    t = pl.program_id(1)

    @pl.when(t == 0)
    def _():
        s_scr[...] = jnp.zeros_like(s_scr)

    row = lax.broadcasted_iota(jnp.int32, (CHUNK, CHUNK), 0)
    col = lax.broadcasted_iota(jnp.int32, (CHUNK, CHUNK), 1)
    causal = row >= col
    ones_tril = causal.astype(BF16)
    ws_tril = [jnp.where(causal, ws_ref[g], 0.0).astype(BF16) for g in range(A_GROUPS)]
    zero_blk = jnp.zeros((HALF, B_DK), BF16)
    chunks = [slice(c * CHUNK, (c + 1) * CHUNK) for c in range(n_chunks)]
    heads = [(slice(h * B_DK, (h + 1) * B_DK), slice(h * B_DV, (h + 1) * B_DV))
             for h in range(B_HEADS)]

    ya = []
    for rows in chunks:
        u = uv_ref[rows, :A_WIDTH].astype(F32)
        vn = uv_ref[rows, A_WIDTH:]
        parts = []
        for g in range(A_GROUPS):
            cols = slice(g * A_GROUP_DIM, (g + 1) * A_GROUP_DIM)
            mixed = _dot(ws_tril[g], vn[:, cols]) + bsp_ref[:, g:g + 1]
            parts.append((u[:, cols] * mixed).astype(BF16))
        ya.append(jnp.concatenate(parts, axis=1))

    prep = [_gla_prep(la_ref[rows, :], qk_ref[rows, :].astype(F32), ones_tril) for rows in chunks]
    scores = []
    for p in prep:
        per_head = []
        for kc, _ in heads:
            q3 = jnp.concatenate([
                jnp.concatenate([p["a0"][:, kc], zero_blk, zero_blk], axis=1),
                jnp.concatenate([zero_blk, p["q_off1"][:, kc], p["a1"][:, kc]], axis=1)], axis=0)
            k3 = jnp.concatenate([
                jnp.concatenate([p["k0"][:, kc], p["k_off0"][:, kc], zero_blk], axis=1),
                jnp.concatenate([zero_blk, zero_blk, p["k1"][:, kc]], axis=1)], axis=0)
            per_head.append(jnp.where(causal, _dot_nt(q3, k3), 0.0).astype(BF16))
        scores.append(per_head)

    state = [s_scr[h] for h in range(B_HEADS)]
    state_before = []
    for rows, p in zip(chunks, prep):
        state_before.append([s.astype(BF16) for s in state])
        for h, (kc, vc) in enumerate(heads):
            upd = _dot(p["k_st"][:, kc].T.astype(BF16), v_ref[rows, vc])
            decay_col = jnp.broadcast_to(p["decay"][:, kc], (B_DK, B_DK)).T
            state[h] = state[h] * jnp.concatenate([decay_col, decay_col], axis=1) + upd
    for h in range(B_HEADS):
        s_scr[h] = state[h]

    yb = []
    for c, (rows, p) in enumerate(zip(chunks, prep)):
        gb = gb_ref[rows, :].astype(F32)
        parts = []
        for h, (kc, vc) in enumerate(heads):
            o = _dot(jnp.concatenate([scores[c][h], p["q_int"][:, kc]], axis=1),
                     jnp.concatenate([v_ref[rows, vc], state_before[c][h]], axis=0))
            parts.append((_rms(o, ggla_ref[...]) * gb[:, vc]).astype(BF16))
        yb.append(jnp.concatenate(parts, axis=1))

    mix = (sga_ref[...].astype(F32) * _dot(jnp.concatenate(ya, axis=0), wpa_ref[...])
           + sgb_ref[...].astype(F32) * _dot(jnp.concatenate(yb, axis=0), wpb_ref[...]))
    x2_ref[...] = x1_ref[...] + _dot(mix.astype(BF16), wo_ref[...])

    @pl.when(t == pl.num_programs(1) - 1)
    def _():
        sout_ref[0] = s_scr[...]


def _sample_mixer_kernel(uv_ref, qk_ref, v_ref, gb_ref, la_ref, s_ref, ws0_ref, bs0_ref, ggla_ref,
                         ya_ref, yb_ref, sout_ref, cv_ref, o_scr, *, nb):
    vn = uv_ref[:, A_WIDTH:]
    cv_ref[...] = vn
    ya_ref[...] = uv_ref[:, :A_WIDTH] * (vn * ws0_ref[...] + bs0_ref[...])

    a = jnp.exp(la_ref[...])
    qk = qk_ref[...]
    qs = qk[:, :B_KW]
    k = qk[:, B_KW:]
    v = v_ref[...]
    pad = jnp.zeros((LANES - 3 * nb, B_DK), F32)
    for h in range(B_HEADS):
        kc = slice(h * B_DK, (h + 1) * B_DK)
        vc = slice(h * B_DV, (h + 1) * B_DV)
        xt = jnp.concatenate([a[:, kc], k[:, kc], qs[:, kc], pad], axis=0).T
        for n in range(nb):
            s_new = (s_ref[n, h] * xt[:, n:n + 1]
                     + xt[:, nb + n:nb + n + 1] * v[n:n + 1, vc])
            sout_ref[n, h] = s_new
            o_scr[n:n + 1, vc] = jnp.sum(xt[:, 2 * nb + n:2 * nb + n + 1] * s_new,
                                         axis=0, keepdims=True)
    gb = gb_ref[...]
    for h in range(B_HEADS):
        vc = slice(h * B_DV, (h + 1) * B_DV)
        yb_ref[:, vc] = _rms(o_scr[:, vc], ggla_ref[...]) * gb[:, vc]


def _merge_kernel(ya_ref, yb_ref, ga_ref, gb_ref, x1_ref, wpa_ref, wpb_ref, wo_ref, x2_ref):
    mix = (ga_ref[...].astype(F32) * _dot(ya_ref[...], wpa_ref[...])
           + gb_ref[...].astype(F32) * _dot(yb_ref[...], wpb_ref[...]))
    x2_ref[...] = x1_ref[...] + _dot(mix.astype(BF16), wo_ref[...])


def _ffn2_kernel(x_ref, p_ref, g2_ref, w2i_ref, w2o_ref, gple_ref, wpg_ref, wple_ref, gfin_ref,
                 y_ref, *, final_norm):
    x3 = _swiglu_residual(x_ref[...], g2_ref, w2i_ref, w2o_ref)
    gate = jax.nn.sigmoid(_dot(_rms(x3, gple_ref[...]).astype(BF16), wpg_ref[...]))
    x4 = x3 + _dot(p_ref[...].astype(BF16), wple_ref[...]) * gate
    y_ref[...] = _rms(x4, gfin_ref[...]) if final_norm else x4


def _resident(shape):
    zeros = (0,) * len(shape)
    return pl.BlockSpec(shape, lambda *_: zeros, pipeline_mode=pl.Buffered(1))


def _rows(tm, width, colblk=0):
    return pl.BlockSpec((tm, width), lambda i: (i, colblk))


def _params(n_axes):
    return pltpu.CompilerParams(dimension_semantics=("arbitrary",) * n_axes,
                                vmem_limit_bytes=VMEM_LIMIT)


def _token_tile(m):
    return 512 if m % 512 == 0 else m


def _ffn1(x, g1, w1i, w1o, gmix):
    m = x.shape[0]
    tm = _token_tile(m)
    return pl.pallas_call(
        _ffn1_kernel,
        grid=(m // tm,),
        in_specs=[_rows(tm, D_MODEL), _resident(g1.shape), _resident(w1i.shape),
                  _resident(w1o.shape), _resident(gmix.shape)],
        out_specs=[_rows(tm, D_MODEL), _rows(tm, D_MODEL)],
        out_shape=[jax.ShapeDtypeStruct((m, D_MODEL), F32),
                   jax.ShapeDtypeStruct((m, D_MODEL), BF16)],
        compiler_params=_params(1),
        name="ffn1",
    )(x, g1, w1i, w1o, gmix)


def _inproj(h, w_in, wgu, bg, lng, lnb, out_dtype):
    m = h.shape[0]
    tm = _token_tile(m)
    small = [wgu, bg, lng, lnb]
    return pl.pallas_call(
        _inproj_kernel,
        grid=(m // tm,),
        in_specs=[_rows(tm, D_MODEL), _resident(w_in.shape)] + [_resident(w.shape) for w in small],
        out_specs=[_rows(tm, P_COLS), _rows(tm, B_KW)],
        out_shape=[jax.ShapeDtypeStruct((m, P_COLS), out_dtype),
                   jax.ShapeDtypeStruct((m, B_KW), F32)],
        scratch_shapes=[pltpu.VMEM((D_MODEL, 2 * D_MODEL), BF16)],
        compiler_params=_params(1),
        name="inproj",
    )(h, w_in, *small)


def _mixer(p, la, x1, n_seq, seq, ws, bsp_t, ggla, wpa, wpb, wo):
    tt = MIXER_TILE
    nt = seq // tt
    m = n_seq * seq

    def seg(colblk, width=D_MODEL):
        return pl.BlockSpec((tt, width), lambda b, t: (b * nt + t, colblk))

    small = [ws, bsp_t, ggla, wpa, wpb, wo]
    return pl.pallas_call(
        functools.partial(_mixer_kernel, n_chunks=tt // CHUNK),
        grid=(n_seq, nt),
        in_specs=[seg(0), seg(1), seg(2), seg(3), seg(0, B_KW), seg(4), seg(5), seg(0)]
                 + [_resident(w.shape) for w in small],
        out_specs=[seg(0),
                   pl.BlockSpec((1, B_HEADS, B_DK, B_DV), lambda b, t: (b, 0, 0, 0))],
        out_shape=[jax.ShapeDtypeStruct((m, D_MODEL), F32),
                   jax.ShapeDtypeStruct((n_seq, B_HEADS, B_DK, B_DV), F32)],
        scratch_shapes=[pltpu.VMEM((B_HEADS, B_DK, B_DV), F32)],
        compiler_params=_params(2),
        name="mixer",
    )(p, p, p, p, la, p, p, x1, *small)


def _sample_mixer(p, la, state, ws0, bs0, ggla):
    m = p.shape[0]
    nb = 8

    def seg(colblk):
        return pl.BlockSpec((nb, D_MODEL), lambda i: (i, colblk))

    state_spec = pl.BlockSpec((nb, B_HEADS, B_DK, B_DV), lambda i: (i, 0, 0, 0))
    small = [ws0, bs0, ggla]
    return pl.pallas_call(
        functools.partial(_sample_mixer_kernel, nb=nb),
        grid=(m // nb,),
        in_specs=[seg(0), seg(1), seg(2), seg(3), _rows(nb, B_KW), state_spec]
                 + [_resident(w.shape) for w in small],
        out_specs=[_rows(nb, A_WIDTH), _rows(nb, B_VW), state_spec, _rows(nb, A_WIDTH)],
        out_shape=[jax.ShapeDtypeStruct((m, A_WIDTH), F32),
                   jax.ShapeDtypeStruct((m, B_VW), F32),
                   jax.ShapeDtypeStruct(state.shape, F32),
                   jax.ShapeDtypeStruct((m, A_WIDTH), F32)],
        scratch_shapes=[pltpu.VMEM((nb, B_VW), F32)],
        compiler_params=_params(1),
        name="sample_mixer",
    )(p, p, p, p, la, state, *small)


def _merge(ya, yb, p, x1, wpa, wpb, wo):
    m = x1.shape[0]
    tm = _token_tile(m)
    return pl.pallas_call(
        _merge_kernel,
        grid=(m // tm,),
        in_specs=[_rows(tm, A_WIDTH), _rows(tm, B_VW), _rows(tm, D_MODEL, 4),
                  _rows(tm, D_MODEL, 5), _rows(tm, D_MODEL), _resident(wpa.shape),
                  _resident(wpb.shape), _resident(wo.shape)],
        out_specs=_rows(tm, D_MODEL),
        out_shape=jax.ShapeDtypeStruct((m, D_MODEL), F32),
        compiler_params=_params(1),
        name="merge",
    )(ya, yb, p, p, x1, wpa, wpb, wo)


def _ffn2(x, p, g2, w2i, w2o, gple, wpg, wple, gfin, final_norm):
    m = x.shape[0]
    tm = _token_tile(m)
    weights = [g2, w2i, w2o, gple, wpg, wple, gfin]
    return pl.pallas_call(
        functools.partial(_ffn2_kernel, final_norm=final_norm),
        grid=(m // tm,),
        in_specs=[_rows(tm, D_MODEL), _rows(tm, PLE_DIM)] + [_resident(w.shape) for w in weights],
        out_specs=_rows(tm, D_MODEL),
        out_shape=jax.ShapeDtypeStruct((m, D_MODEL), F32),
        compiler_params=_params(1),
        name="ffn2",
    )(x, p, *weights)


def kernel(x_prompt, x_sample, p_prompt, p_sample, state_gla, g_ffn1, w_ffn1_in, w_ffn1_out, g_mix,
           w_in, ln_v_g, ln_v_b, w_spatial, b_spatial, w_gate_up, b_gate, g_gla_out, w_proj_a,
           w_proj_b, w_out, g_ffn2, w_ffn2_in, w_ffn2_out, g_ple, w_ple_gate, w_ple, g_final):
    depth = w_in.shape[0]
    n_seq, seq, _ = x_prompt.shape
    n_dec, dec_seq, _ = x_sample.shape
    assert dec_seq == 1 and seq % MIXER_TILE == 0

    def row(vec):
        return vec.reshape(1, -1).astype(F32)

    assert sum(IN_SIZES[:6]) == LR_LO and sum(IN_SIZES) == GATE_LO + 2 * D_MODEL
    xp = x_prompt.reshape(n_seq * seq, D_MODEL)
    xs = x_sample.reshape(n_dec, D_MODEL)
    gfin = row(g_final)
    sp_list, ss_list, vs_list = [], [], []
    for i in range(depth):
        w1i, w1o = w_ffn1_in[i].astype(BF16), w_ffn1_out[i].astype(BF16)
        w2i, w2o = w_ffn2_in[i].astype(BF16), w_ffn2_out[i].astype(BF16)
        w_inb = w_in[i].astype(BF16)
        wgu = jnp.pad(w_gate_up[i], ((0, LANES - GATE_RANK), (0, 0))).astype(BF16)
        wpa, wpb, wo = (w.astype(BF16) for w in (w_proj_a[i], w_proj_b[i], w_out[i]))
        wpg, wple = w_ple_gate[i].astype(BF16), w_ple[i].astype(BF16)
        g1, gmix, g2, gple = row(g_ffn1[i]), row(g_mix[i]), row(g_ffn2[i]), row(g_ple[i])
        lng, lnb, bg, ggla = row(ln_v_g[i]), row(ln_v_b[i]), row(b_gate[i]), row(g_gla_out[i])
        ws0 = row(jnp.repeat(w_spatial[i][:, 0, 0], A_GROUP_DIM))
        bs0 = row(jnp.repeat(b_spatial[i][:, 0], A_GROUP_DIM))

        x1, h = _ffn1(xp, g1, w1i, w1o, gmix)
        p, la = _inproj(h, w_inb, wgu, bg, lng, lnb, BF16)
        x2, sp = _mixer(p, la, x1, n_seq, seq, w_spatial[i], b_spatial[i].T, ggla, wpa, wpb, wo)
        pp = p_prompt[i].reshape(n_seq * seq, PLE_DIM)
        last = i == depth - 1
        xp = _ffn2(x2, pp, g2, w2i, w2o, gple, wpg, wple, gfin, last)

        x1, h = _ffn1(xs, g1, w1i, w1o, gmix)
        p, la = _inproj(h, w_inb, wgu, bg, lng, lnb, F32)
        ya, yb, ss, cv = _sample_mixer(p, la, state_gla[i], ws0, bs0, ggla)
        x2 = _merge(ya.astype(BF16), yb.astype(BF16), p, x1, wpa, wpb, wo)
        ps = p_sample[i].reshape(n_dec, PLE_DIM)
        xs = _ffn2(x2, ps, g2, w2i, w2o, gple, wpg, wple, gfin, last)

        sp_list.append(sp)
        ss_list.append(ss)
        vs_list.append(cv.reshape(n_dec, dec_seq, A_WIDTH))
    return (xp.reshape(n_seq, seq, D_MODEL), xs.reshape(n_dec, dec_seq, D_MODEL),
            jnp.stack(sp_list), jnp.stack(ss_list), jnp.stack(vs_list))
```

```python
import functools

import jax
import jax.numpy as jnp
from jax import lax
from jax.experimental import pallas as pl
from jax.experimental.pallas import tpu as pltpu

D_MODEL = 1024
D_FF = 2816
PLE_DIM = 256
CHUNK = 128
A_GROUPS = 4
A_GROUP_DIM = 128
A_WIDTH = A_GROUPS * A_GROUP_DIM
B_HEADS = 4
B_DK = 128
B_DV = 256
B_KW = B_HEADS * B_DK
B_VW = B_HEADS * B_DV
GATE_RANK = 16
GATE_NORM = 16.0
EPS = 1e-6
IN_SIZES = (A_WIDTH, A_WIDTH, B_KW, B_KW, B_VW, B_VW, GATE_RANK, D_MODEL, D_MODEL)

LANES = 128
P_COLS = 6 * D_MODEL
QK_LO = 2 * A_WIDTH
VB_LO = QK_LO + 2 * B_KW
GB_LO = VB_LO + B_VW
LR_LO = GB_LO + B_VW
GATE_LO = LR_LO + GATE_RANK
FF_CHUNK = 256
HALF = CHUNK // 2
MIXER_TILE = 4 * CHUNK
FFN_TILE = 1024
PROJ_TILE = 512
BF16_SUBLANES = 16
VMEM_LIMIT = 56 * 1024 * 1024

F32 = jnp.float32
BF16 = jnp.bfloat16


def _dot(a, b):
    return jnp.dot(a, b, preferred_element_type=F32)


def _dot_nt(a, b):
    return lax.dot_general(a, b, (((1,), (1,)), ((), ())), preferred_element_type=F32)


def _rms(x, g):
    return x * lax.rsqrt(jnp.mean(x * x, axis=-1, keepdims=True) + EPS) * g


def _log_sigmoid(x):
    return jnp.minimum(x, 0.0) - jnp.log1p(jnp.exp(-jnp.abs(x)))


def _layernorm(x, g, b):
    mu = jnp.mean(x, axis=-1, keepdims=True)
    xc = x - mu
    var = jnp.mean(xc * xc, axis=-1, keepdims=True)
    return xc * lax.rsqrt(var + EPS) * g + b


def _swiglu_residual(x, g_ref, w_in_ref, w_out_ref):
    h = _rms(x, g_ref[...]).astype(BF16)
    acc = None
    for lo in range(0, D_FF, FF_CHUNK):
        gate = _dot(h, w_in_ref[:, lo:lo + FF_CHUNK])
        up = _dot(h, w_in_ref[:, D_FF + lo:D_FF + lo + FF_CHUNK])
        act = (jax.nn.silu(gate) * up).astype(BF16)
        part = _dot(act, w_out_ref[lo:lo + FF_CHUNK, :])
        acc = part if acc is None else acc + part
    return x + 0.5 * acc


def _cast_blocks(src_refs, dst_refs):
    for src, dst in zip(src_refs, dst_refs):
        dst[...] = src[...].astype(dst.dtype)


def _ffn1_kernel(x_ref, g1_ref, w1i_ref, w1o_ref, gmix_ref, *refs, n_cast):
    cast_src, (x1_ref, h_ref), cast_dst = refs[:n_cast], refs[n_cast:n_cast + 2], refs[n_cast + 2:]
    _cast_blocks(cast_src, cast_dst)
    x1 = _swiglu_residual(x_ref[...], g1_ref, w1i_ref, w1o_ref)
    x1_ref[...] = x1
    h_ref[...] = _rms(x1, gmix_ref[...]).astype(BF16)


def _inproj_kernel(h_ref, wt_ref, wgu_ref, bg_ref, lng_ref, lnb_ref, *refs, n_cast):
    cast_src, (p_ref, la_ref), cast_dst = refs[:n_cast], refs[n_cast:n_cast + 2], refs[n_cast + 2:]
    _cast_blocks(cast_src, cast_dst)

    def put(lo, val):
        p_ref[:, lo:lo + val.shape[1]] = val.astype(p_ref.dtype)

    def proj(lo, width):
        return _dot_nt(h, wt_ref[lo:lo + width, :])

    h = h_ref[...]
    uv = proj(0, 2 * A_WIDTH)
    put(0, jax.nn.gelu(uv[:, :A_WIDTH]))
    put(A_WIDTH, _layernorm(jax.nn.gelu(uv[:, A_WIDTH:]), lng_ref[...], lnb_ref[...]))
    qk = proj(QK_LO, 2 * B_KW)
    put(QK_LO, qk[:, :B_KW] * (B_DK ** -0.5))
    put(QK_LO + B_KW, qk[:, B_KW:])
    put(VB_LO, proj(VB_LO, B_VW))
    put(GB_LO, jax.nn.silu(proj(GB_LO, B_VW)))
    lr = proj(LR_LO, LANES).astype(BF16)
    logit = _dot(lr, wgu_ref[...]) + bg_ref[...]
    la_ref[...] = _log_sigmoid(logit) * (1.0 / GATE_NORM)
    for j in range(2):
        put(LR_LO + j * D_MODEL, jax.nn.sigmoid(proj(GATE_LO + j * D_MODEL, D_MODEL)))


def _gla_prep(la, qk, ones_tril):
    la_hi = la.astype(BF16)
    la_lo = (la - la_hi.astype(F32)).astype(BF16)
    b = _dot(ones_tril, la_hi) + _dot(ones_tril, la_lo)
    r0 = b[HALF // 2 - 1:HALF // 2, :]
    r1 = b[HALF + HALF // 2 - 1:HALF + HALF // 2, :]
    b_mid = b[HALF - 1:HALF, :]
    b_end = b[CHUNK - 1:CHUNK, :]
    bq0 = b[:HALF] - r0
    bq1 = b[HALF:] - r1
    qs = qk[:, :B_KW]
    k = qk[:, B_KW:]
    a0 = qs[:HALF] * jnp.exp(bq0)
    a1 = qs[HALF:] * jnp.exp(bq1)
    k0 = k[:HALF] * jnp.exp(-bq0)
    k1 = k[HALF:] * jnp.exp(-bq1)
    q_off1 = a1 * jnp.exp(r1 - b_mid)
    k_off0 = k0 * jnp.exp(b_mid - r0)
    q_int = jnp.concatenate([a0 * jnp.exp(r0), q_off1 * jnp.exp(b_mid)], axis=0)
    k_st = jnp.concatenate([k_off0 * jnp.exp(b_end - b_mid), k1 * jnp.exp(b_end - r1)], axis=0)
    return dict(a0=a0.astype(BF16), a1=a1.astype(BF16), k0=k0.astype(BF16), k1=k1.astype(BF16),
                q_off1=q_off1.astype(BF16), k_off0=k_off0.astype(BF16), q_int=q_int.astype(BF16),
                k_st=k_st, decay=jnp.exp(b_end))


def _mixer_kernel(uv_ref, qk_ref, v_ref, gb_ref, la_ref, sga_ref, sgb_ref, x1_ref, ws_ref, bsp_ref,
                  ggla_ref, wpa_ref, wpb_ref, wo_ref, x2_ref, sout_ref, s_scr, *, n_chunks):
    t = pl.program_id(1)

    @pl.when(t == 0)
    def _():
        s_scr[...] = jnp.zeros_like(s_scr)

    row = lax.broadcasted_iota(jnp.int32, (CHUNK, CHUNK), 0)
    col = lax.broadcasted_iota(jnp.int32, (CHUNK, CHUNK), 1)
    causal = row >= col
    ones_tril = causal.astype(BF16)
    ws_tril = [jnp.where(causal, ws_ref[g], 0.0).astype(BF16) for g in range(A_GROUPS)]
    zero_blk = jnp.zeros((HALF, B_DK), BF16)
    chunks = [slice(c * CHUNK, (c + 1) * CHUNK) for c in range(n_chunks)]
    heads = [(slice(h * B_DK, (h + 1) * B_DK), slice(h * B_DV, (h + 1) * B_DV))
             for h in range(B_HEADS)]

    ya = []
    for rows in chunks:
        u = uv_ref[rows, :A_WIDTH].astype(F32)
        vn = uv_ref[rows, A_WIDTH:]
        parts = []
        for g in range(A_GROUPS):
            cols = slice(g * A_GROUP_DIM, (g + 1) * A_GROUP_DIM)
            mixed = _dot(ws_tril[g], vn[:, cols]) + bsp_ref[:, g:g + 1]
            parts.append((u[:, cols] * mixed).astype(BF16))
        ya.append(jnp.concatenate(parts, axis=1))

    prep = [_gla_prep(la_ref[rows, :], qk_ref[rows, :].astype(F32), ones_tril) for rows in chunks]
    scores = []
    for p in prep:
        per_head = []
        for kc, _ in heads:
            q3 = jnp.concatenate([
                jnp.concatenate([p["a0"][:, kc], zero_blk, zero_blk], axis=1),
                jnp.concatenate([zero_blk, p["q_off1"][:, kc], p["a1"][:, kc]], axis=1)], axis=0)
            k3 = jnp.concatenate([
                jnp.concatenate([p["k0"][:, kc], p["k_off0"][:, kc], zero_blk], axis=1),
                jnp.concatenate([zero_blk, zero_blk, p["k1"][:, kc]], axis=1)], axis=0)
            per_head.append(jnp.where(causal, _dot_nt(q3, k3), 0.0).astype(BF16))
        scores.append(per_head)

    state = [s_scr[h] for h in range(B_HEADS)]
    state_before = []
    for rows, p in zip(chunks, prep):
        state_before.append([s.astype(BF16) for s in state])
        for h, (kc, vc) in enumerate(heads):
            upd = _dot(p["k_st"][:, kc].T.astype(BF16), v_ref[rows, vc])
            decay_col = jnp.broadcast_to(p["decay"][:, kc], (B_DK, B_DK)).T
            state[h] = state[h] * jnp.concatenate([decay_col, decay_col], axis=1) + upd
    for h in range(B_HEADS):
        s_scr[h] = state[h]

    yb = []
    for c, (rows, p) in enumerate(zip(chunks, prep)):
        gb = gb_ref[rows, :].astype(F32)
        parts = []
        for h, (kc, vc) in enumerate(heads):
            o = _dot(jnp.concatenate([scores[c][h], p["q_int"][:, kc]], axis=1),
                     jnp.concatenate([v_ref[rows, vc], state_before[c][h]], axis=0))
            parts.append((_rms(o, ggla_ref[...]) * gb[:, vc]).astype(BF16))
        yb.append(jnp.concatenate(parts, axis=1))

    mix = (sga_ref[...].astype(F32) * _dot(jnp.concatenate(ya, axis=0), wpa_ref[...])
           + sgb_ref[...].astype(F32) * _dot(jnp.concatenate(yb, axis=0), wpb_ref[...]))
    x2_ref[...] = x1_ref[...] + _dot(mix.astype(BF16), wo_ref[...])

    @pl.when(t == pl.num_programs(1) - 1)
    def _():
        sout_ref[0] = s_scr[...]


def _sample_mixer_kernel(uv_ref, qk_ref, v_ref, gb_ref, la_ref, s_ref, ws0_ref, bs0_ref, ggla_ref,
                         ya_ref, yb_ref, sout_ref, cv_ref, o_scr, *, nb):
    vn = uv_ref[:, A_WIDTH:]
    cv_ref[...] = vn
    ya_ref[...] = uv_ref[:, :A_WIDTH] * (vn * ws0_ref[...] + bs0_ref[...])

    a = jnp.exp(la_ref[...])
    qk = qk_ref[...]
    qs = qk[:, :B_KW]
    k = qk[:, B_KW:]
    v = v_ref[...]
    pad = jnp.zeros((LANES - 3 * nb, B_DK), F32)
    for h in range(B_HEADS):
        kc = slice(h * B_DK, (h + 1) * B_DK)
        vc = slice(h * B_DV, (h + 1) * B_DV)
        xt = jnp.concatenate([a[:, kc], k[:, kc], qs[:, kc], pad], axis=0).T
        for n in range(nb):
            s_new = (s_ref[n, h] * xt[:, n:n + 1]
                     + xt[:, nb + n:nb + n + 1] * v[n:n + 1, vc])
            sout_ref[n, h] = s_new
            o_scr[n:n + 1, vc] = jnp.sum(xt[:, 2 * nb + n:2 * nb + n + 1] * s_new,
                                         axis=0, keepdims=True)
    gb = gb_ref[...]
    for h in range(B_HEADS):
        vc = slice(h * B_DV, (h + 1) * B_DV)
        yb_ref[:, vc] = _rms(o_scr[:, vc], ggla_ref[...]) * gb[:, vc]


def _merge_kernel(ya_ref, yb_ref, ga_ref, gb_ref, x1_ref, wpa_ref, wpb_ref, wo_ref, x2_ref):
    mix = (ga_ref[...].astype(F32) * _dot(ya_ref[...], wpa_ref[...])
           + gb_ref[...].astype(F32) * _dot(yb_ref[...], wpb_ref[...]))
    x2_ref[...] = x1_ref[...] + _dot(mix.astype(BF16), wo_ref[...])


def _ffn2_kernel(x_ref, p_ref, g2_ref, w2i_ref, w2o_ref, gple_ref, wpg_ref, wple_ref, gfin_ref,
                 y_ref, *, final_norm):
    x3 = _swiglu_residual(x_ref[...], g2_ref, w2i_ref, w2o_ref)
    gate = jax.nn.sigmoid(_dot(_rms(x3, gple_ref[...]).astype(BF16), wpg_ref[...]))
    x4 = x3 + _dot(p_ref[...].astype(BF16), wple_ref[...]) * gate
    y_ref[...] = _rms(x4, gfin_ref[...]) if final_norm else x4


def _resident(shape):
    zeros = (0,) * len(shape)
    return pl.BlockSpec(shape, lambda *_: zeros, pipeline_mode=pl.Buffered(1))


def _rows(tm, width, colblk=0):
    return pl.BlockSpec((tm, width), lambda i: (i, colblk))


def _params(n_axes):
    return pltpu.CompilerParams(dimension_semantics=("arbitrary",) * n_axes,
                                vmem_limit_bytes=VMEM_LIMIT)


def _token_tile(m, tile):
    return tile if m % tile == 0 else m


def _cast_specs(weights, n_steps):
    specs, shapes = [], []
    for w in weights:
        rows, cols = w.shape
        n_blocks = max(n for n in range(1, n_steps + 1) if rows % (n * BF16_SUBLANES) == 0)
        specs.append(pl.BlockSpec((rows // n_blocks, cols),
                                  lambda i, n=n_blocks: (jnp.minimum(i, n - 1), 0)))
        shapes.append(jax.ShapeDtypeStruct(w.shape, BF16))
    return specs, shapes


def _ffn1(x, g1, w1i, w1o, gmix, to_cast=()):
    m = x.shape[0]
    tm = _token_tile(m, FFN_TILE)
    cast_specs, cast_shapes = _cast_specs(to_cast, m // tm)
    x1, h, *cast = pl.pallas_call(
        functools.partial(_ffn1_kernel, n_cast=len(to_cast)),
        grid=(m // tm,),
        in_specs=[_rows(tm, D_MODEL), _resident(g1.shape), _resident(w1i.shape),
                  _resident(w1o.shape), _resident(gmix.shape)] + cast_specs,
        out_specs=[_rows(tm, D_MODEL), _rows(tm, D_MODEL)] + cast_specs,
        out_shape=[jax.ShapeDtypeStruct((m, D_MODEL), F32),
                   jax.ShapeDtypeStruct((m, D_MODEL), BF16)] + cast_shapes,
        compiler_params=_params(1),
        name="ffn1",
    )(x, g1, w1i, w1o, gmix, *to_cast)
    return x1, h, cast


def _inproj(h, w_in, wgu, bg, lng, lnb, out_dtype, to_cast=()):
    m = h.shape[0]
    tm = _token_tile(m, PROJ_TILE)
    small = [wgu, bg, lng, lnb]
    cast_specs, cast_shapes = _cast_specs(to_cast, m // tm)
    p, la, *cast = pl.pallas_call(
        functools.partial(_inproj_kernel, n_cast=len(to_cast)),
        grid=(m // tm,),
        in_specs=[_rows(tm, D_MODEL), _resident(w_in.shape)] + [_resident(w.shape) for w in small]
                 + cast_specs,
        out_specs=[_rows(tm, P_COLS), _rows(tm, B_KW)] + cast_specs,
        out_shape=[jax.ShapeDtypeStruct((m, P_COLS), out_dtype),
                   jax.ShapeDtypeStruct((m, B_KW), F32)] + cast_shapes,
        compiler_params=_params(1),
        name="inproj",
    )(h, w_in, *small, *to_cast)
    return p, la, cast


def _mixer(p, la, x1, n_seq, seq, ws, bsp_t, ggla, wpa, wpb, wo):
    tt = MIXER_TILE
    nt = seq // tt
    m = n_seq * seq

    def seg(colblk, width=D_MODEL):
        return pl.BlockSpec((tt, width), lambda b, t: (b * nt + t, colblk))

    small = [ws, bsp_t, ggla, wpa, wpb, wo]
    return pl.pallas_call(
        functools.partial(_mixer_kernel, n_chunks=tt // CHUNK),
        grid=(n_seq, nt),
        in_specs=[seg(0), seg(1), seg(2), seg(3), seg(0, B_KW), seg(4), seg(5), seg(0)]
                 + [_resident(w.shape) for w in small],
        out_specs=[seg(0),
                   pl.BlockSpec((1, B_HEADS, B_DK, B_DV), lambda b, t: (b, 0, 0, 0))],
        out_shape=[jax.ShapeDtypeStruct((m, D_MODEL), F32),
                   jax.ShapeDtypeStruct((n_seq, B_HEADS, B_DK, B_DV), F32)],
        scratch_shapes=[pltpu.VMEM((B_HEADS, B_DK, B_DV), F32)],
        compiler_params=_params(2),
        name="mixer",
    )(p, p, p, p, la, p, p, x1, *small)


def _sample_mixer(p, la, state, ws0, bs0, ggla):
    m = p.shape[0]
    nb = 8

    def seg(colblk):
        return pl.BlockSpec((nb, D_MODEL), lambda i: (i, colblk))

    state_spec = pl.BlockSpec((nb, B_HEADS, B_DK, B_DV), lambda i: (i, 0, 0, 0))
    small = [ws0, bs0, ggla]
    return pl.pallas_call(
        functools.partial(_sample_mixer_kernel, nb=nb),
        grid=(m // nb,),
        in_specs=[seg(0), seg(1), seg(2), seg(3), _rows(nb, B_KW), state_spec]
                 + [_resident(w.shape) for w in small],
        out_specs=[_rows(nb, A_WIDTH), _rows(nb, B_VW), state_spec, _rows(nb, A_WIDTH)],
        out_shape=[jax.ShapeDtypeStruct((m, A_WIDTH), F32),
                   jax.ShapeDtypeStruct((m, B_VW), F32),
                   jax.ShapeDtypeStruct(state.shape, F32),
                   jax.ShapeDtypeStruct((m, A_WIDTH), F32)],
        scratch_shapes=[pltpu.VMEM((nb, B_VW), F32)],
        compiler_params=_params(1),
        name="sample_mixer",
    )(p, p, p, p, la, state, *small)


def _merge(ya, yb, p, x1, wpa, wpb, wo):
    m = x1.shape[0]
    tm = _token_tile(m, PROJ_TILE)
    return pl.pallas_call(
        _merge_kernel,
        grid=(m // tm,),
        in_specs=[_rows(tm, A_WIDTH), _rows(tm, B_VW), _rows(tm, D_MODEL, 4),
                  _rows(tm, D_MODEL, 5), _rows(tm, D_MODEL), _resident(wpa.shape),
                  _resident(wpb.shape), _resident(wo.shape)],
        out_specs=_rows(tm, D_MODEL),
        out_shape=jax.ShapeDtypeStruct((m, D_MODEL), F32),
        compiler_params=_params(1),
        name="merge",
    )(ya, yb, p, p, x1, wpa, wpb, wo)


def _ffn2(x, p, g2, w2i, w2o, gple, wpg, wple, gfin, final_norm):
    m = x.shape[0]
    tm = _token_tile(m, FFN_TILE)
    weights = [g2, w2i, w2o, gple, wpg, wple, gfin]
    return pl.pallas_call(
        functools.partial(_ffn2_kernel, final_norm=final_norm),
        grid=(m // tm,),
        in_specs=[_rows(tm, D_MODEL), _rows(tm, PLE_DIM)] + [_resident(w.shape) for w in weights],
        out_specs=_rows(tm, D_MODEL),
        out_shape=jax.ShapeDtypeStruct((m, D_MODEL), F32),
        compiler_params=_params(1),
        name="ffn2",
    )(x, p, *weights)


def kernel(x_prompt, x_sample, p_prompt, p_sample, state_gla, g_ffn1, w_ffn1_in, w_ffn1_out, g_mix,
           w_in, ln_v_g, ln_v_b, w_spatial, b_spatial, w_gate_up, b_gate, g_gla_out, w_proj_a,
           w_proj_b, w_out, g_ffn2, w_ffn2_in, w_ffn2_out, g_ple, w_ple_gate, w_ple, g_final):
    depth = w_in.shape[0]
    n_seq, seq, _ = x_prompt.shape
    n_dec, dec_seq, _ = x_sample.shape
    assert dec_seq == 1 and seq % MIXER_TILE == 0

    def row(vec):
        return vec.reshape(1, -1).astype(F32)

    assert sum(IN_SIZES[:6]) == LR_LO and sum(IN_SIZES) == GATE_LO + 2 * D_MODEL
    xp = x_prompt.reshape(n_seq * seq, D_MODEL)
    xs = x_sample.reshape(n_dec, D_MODEL)
    gfin = row(g_final)
    sp_list, ss_list, vs_list = [], [], []
    for i in range(depth):
        w1i, w1o = w_ffn1_in[i].astype(BF16), w_ffn1_out[i].astype(BF16)
        wgu = jnp.pad(w_gate_up[i], ((0, LANES - GATE_RANK), (0, 0))).astype(BF16)
        g1, gmix, g2, gple = row(g_ffn1[i]), row(g_mix[i]), row(g_ffn2[i]), row(g_ple[i])
        lng, lnb, bg, ggla = row(ln_v_g[i]), row(ln_v_b[i]), row(b_gate[i]), row(g_gla_out[i])
        ws0 = row(jnp.repeat(w_spatial[i][:, 0, 0], A_GROUP_DIM))
        bs0 = row(jnp.repeat(b_spatial[i][:, 0], A_GROUP_DIM))

        x1, h, (w_inb,) = _ffn1(xp, g1, w1i, w1o, gmix, to_cast=[w_in[i].T])
        later = [w_proj_a[i], w_proj_b[i], w_out[i], w_ffn2_in[i], w_ffn2_out[i], w_ple_gate[i],
                 w_ple[i]]
        p, la, (wpa, wpb, wo, w2i, w2o, wpg, wple) = _inproj(h, w_inb, wgu, bg, lng, lnb, BF16,
                                                             to_cast=later)
        x2, sp = _mixer(p, la, x1, n_seq, seq, w_spatial[i], b_spatial[i].T, ggla, wpa, wpb, wo)
        pp = p_prompt[i].reshape(n_seq * seq, PLE_DIM)
        last = i == depth - 1
        xp = _ffn2(x2, pp, g2, w2i, w2o, gple, wpg, wple, gfin, last)

        x1, h, _ = _ffn1(xs, g1, w1i, w1o, gmix)
        p, la, _ = _inproj(h, w_inb, wgu, bg, lng, lnb, F32)
        ya, yb, ss, cv = _sample_mixer(p, la, state_gla[i], ws0, bs0, ggla)
        x2 = _merge(ya.astype(BF16), yb.astype(BF16), p, x1, wpa, wpb, wo)
        ps = p_sample[i].reshape(n_dec, PLE_DIM)
        xs = _ffn2(x2, ps, g2, w2i, w2o, gple, wpg, wple, gfin, last)

        sp_list.append(sp)
        ss_list.append(ss)
        vs_list.append(cv.reshape(n_dec, dec_seq, A_WIDTH))
    return (xp.reshape(n_seq, seq, D_MODEL), xs.reshape(n_dec, dec_seq, D_MODEL),
            jnp.stack(sp_list), jnp.stack(ss_list), jnp.stack(vs_list))
```

```python
import functools

import jax
import jax.numpy as jnp
from jax import lax
from jax.experimental import pallas as pl
from jax.experimental.pallas import tpu as pltpu

D_MODEL = 1024
D_FF = 2816
PLE_DIM = 256
CHUNK = 128
A_GROUPS = 4
A_GROUP_DIM = 128
A_WIDTH = A_GROUPS * A_GROUP_DIM
B_HEADS = 4
B_DK = 128
B_DV = 256
B_KW = B_HEADS * B_DK
B_VW = B_HEADS * B_DV
GATE_RANK = 16
GATE_NORM = 16.0
EPS = 1e-6
IN_SIZES = (A_WIDTH, A_WIDTH, B_KW, B_KW, B_VW, B_VW, GATE_RANK, D_MODEL, D_MODEL)

LANES = 128
P_COLS = 6 * D_MODEL
QK_LO = 2 * A_WIDTH
VB_LO = QK_LO + 2 * B_KW
GB_LO = VB_LO + B_VW
LR_LO = GB_LO + B_VW
GATE_LO = LR_LO + GATE_RANK
FF_CHUNK = 256
HALF = CHUNK // 2
MIXER_TILE = 4 * CHUNK
TOKEN_TILE = 512
BF16_SUBLANES = 16
VMEM_LIMIT = 56 * 1024 * 1024

F32 = jnp.float32
BF16 = jnp.bfloat16


def _dot(a, b):
    return jnp.dot(a, b, preferred_element_type=F32)


def _dot_nt(a, b):
    return lax.dot_general(a, b, (((1,), (1,)), ((), ())), preferred_element_type=F32)


def _rms(x, g):
    return x * lax.rsqrt(jnp.mean(x * x, axis=-1, keepdims=True) + EPS) * g


def _log_sigmoid(x):
    return jnp.minimum(x, 0.0) - jnp.log1p(jnp.exp(-jnp.abs(x)))


def _layernorm(x, g, b):
    mu = jnp.mean(x, axis=-1, keepdims=True)
    xc = x - mu
    var = jnp.mean(xc * xc, axis=-1, keepdims=True)
    return xc * lax.rsqrt(var + EPS) * g + b


def _swiglu_residual(x, g_ref, w_in_ref, w_out_ref):
    h = _rms(x, g_ref[...]).astype(BF16)
    acc = None
    for lo in range(0, D_FF, FF_CHUNK):
        gate = _dot(h, w_in_ref[:, lo:lo + FF_CHUNK])
        up = _dot(h, w_in_ref[:, D_FF + lo:D_FF + lo + FF_CHUNK])
        act = (jax.nn.silu(gate) * up).astype(BF16)
        part = _dot(act, w_out_ref[lo:lo + FF_CHUNK, :])
        acc = part if acc is None else acc + part
    return x + 0.5 * acc


def _cast_blocks(src_refs, dst_refs):
    for src, dst in zip(src_refs, dst_refs):
        dst[...] = src[...].astype(dst.dtype)


def _two_group_kernel(*refs, body, n_in, n_w, n_out, n_cast):
    it = iter(refs)

    def take(k):
        return [next(it) for _ in range(k)]

    p_in, s_in, weights, cast_src = take(n_in), take(n_in), take(n_w), take(n_cast)
    p_out, s_out, cast_dst = take(n_out), take(n_out), take(n_cast)
    _cast_blocks(cast_src, cast_dst)
    step = pl.program_id(0)
    sample_step = pl.num_programs(0) - 1

    @pl.when(step < sample_step)
    def _():
        body(p_in, weights, p_out)

    @pl.when(step == sample_step)
    def _():
        body(s_in, weights, s_out)


def _ffn1_body(ins, weights, outs):
    (x_ref,), (g1_ref, w1i_ref, w1o_ref, gmix_ref), (x1_ref, h_ref) = ins, weights, outs
    x1 = _swiglu_residual(x_ref[...], g1_ref, w1i_ref, w1o_ref)
    x1_ref[...] = x1
    h_ref[...] = _rms(x1, gmix_ref[...]).astype(BF16)


def _inproj_body(ins, weights, outs):
    (h_ref,), (wt_ref, wgu_ref, bg_ref, lng_ref, lnb_ref), (p_ref, la_ref) = ins, weights, outs

    def put(lo, val):
        p_ref[:, lo:lo + val.shape[1]] = val.astype(p_ref.dtype)

    def proj(lo, width):
        return _dot_nt(h, wt_ref[lo:lo + width, :])

    h = h_ref[...]
    uv = proj(0, 2 * A_WIDTH)
    put(0, jax.nn.gelu(uv[:, :A_WIDTH]))
    put(A_WIDTH, _layernorm(jax.nn.gelu(uv[:, A_WIDTH:]), lng_ref[...], lnb_ref[...]))
    qk = proj(QK_LO, 2 * B_KW)
    put(QK_LO, qk[:, :B_KW] * (B_DK ** -0.5))
    put(QK_LO + B_KW, qk[:, B_KW:])
    put(VB_LO, proj(VB_LO, B_VW))
    put(GB_LO, jax.nn.silu(proj(GB_LO, B_VW)))
    lr = proj(LR_LO, LANES).astype(BF16)
    logit = _dot(lr, wgu_ref[...]) + bg_ref[...]
    la_ref[...] = _log_sigmoid(logit) * (1.0 / GATE_NORM)
    for j in range(2):
        put(LR_LO + j * D_MODEL, jax.nn.sigmoid(proj(GATE_LO + j * D_MODEL, D_MODEL)))


def _gla_prep(la, qk, ones_tril):
    la_hi = la.astype(BF16)
    la_lo = (la - la_hi.astype(F32)).astype(BF16)
    b = _dot(ones_tril, la_hi) + _dot(ones_tril, la_lo)
    r0 = b[HALF // 2 - 1:HALF // 2, :]
    r1 = b[HALF + HALF // 2 - 1:HALF + HALF // 2, :]
    b_mid = b[HALF - 1:HALF, :]
    b_end = b[CHUNK - 1:CHUNK, :]
    bq0 = b[:HALF] - r0
    bq1 = b[HALF:] - r1
    qs = qk[:, :B_KW]
    k = qk[:, B_KW:]
    a0 = qs[:HALF] * jnp.exp(bq0)
    a1 = qs[HALF:] * jnp.exp(bq1)
    k0 = k[:HALF] * jnp.exp(-bq0)
    k1 = k[HALF:] * jnp.exp(-bq1)
    q_off1 = a1 * jnp.exp(r1 - b_mid)
    k_off0 = k0 * jnp.exp(b_mid - r0)
    q_int = jnp.concatenate([a0 * jnp.exp(r0), q_off1 * jnp.exp(b_mid)], axis=0)
    k_st = jnp.concatenate([k_off0 * jnp.exp(b_end - b_mid), k1 * jnp.exp(b_end - r1)], axis=0)
    return dict(a0=a0.astype(BF16), a1=a1.astype(BF16), k0=k0.astype(BF16), k1=k1.astype(BF16),
                q_off1=q_off1.astype(BF16), k_off0=k_off0.astype(BF16), q_int=q_int.astype(BF16),
                k_st=k_st, decay=jnp.exp(b_end))


def _mixer_kernel(uv_ref, qk_ref, v_ref, gb_ref, la_ref, sga_ref, sgb_ref, x1_ref, ws_ref, bsp_ref,
                  ggla_ref, wpa_ref, wpb_ref, wo_ref, x2_ref, sout_ref, s_scr, *, n_chunks):
    t = pl.program_id(1)

    @pl.when(t == 0)
    def _():
        s_scr[...] = jnp.zeros_like(s_scr)

    row = lax.broadcasted_iota(jnp.int32, (CHUNK, CHUNK), 0)
    col = lax.broadcasted_iota(jnp.int32, (CHUNK, CHUNK), 1)
    causal = row >= col
    ones_tril = causal.astype(BF16)
    ws_tril = [jnp.where(causal, ws_ref[g], 0.0).astype(BF16) for g in range(A_GROUPS)]
    zero_blk = jnp.zeros((HALF, B_DK), BF16)
    chunks = [slice(c * CHUNK, (c + 1) * CHUNK) for c in range(n_chunks)]
    heads = [(slice(h * B_DK, (h + 1) * B_DK), slice(h * B_DV, (h + 1) * B_DV))
             for h in range(B_HEADS)]

    ya = []
    for rows in chunks:
        u = uv_ref[rows, :A_WIDTH].astype(F32)
        vn = uv_ref[rows, A_WIDTH:]
        parts = []
        for g in range(A_GROUPS):
            cols = slice(g * A_GROUP_DIM, (g + 1) * A_GROUP_DIM)
            mixed = _dot(ws_tril[g], vn[:, cols]) + bsp_ref[:, g:g + 1]
            parts.append((u[:, cols] * mixed).astype(BF16))
        ya.append(jnp.concatenate(parts, axis=1))

    prep = [_gla_prep(la_ref[rows, :], qk_ref[rows, :].astype(F32), ones_tril) for rows in chunks]
    scores = []
    for p in prep:
        per_head = []
        for kc, _ in heads:
            q3 = jnp.concatenate([
                jnp.concatenate([p["a0"][:, kc], zero_blk, zero_blk], axis=1),
                jnp.concatenate([zero_blk, p["q_off1"][:, kc], p["a1"][:, kc]], axis=1)], axis=0)
            k3 = jnp.concatenate([
                jnp.concatenate([p["k0"][:, kc], p["k_off0"][:, kc], zero_blk], axis=1),
                jnp.concatenate([zero_blk, zero_blk, p["k1"][:, kc]], axis=1)], axis=0)
            per_head.append(jnp.where(causal, _dot_nt(q3, k3), 0.0).astype(BF16))
        scores.append(per_head)

    state = [s_scr[h] for h in range(B_HEADS)]
    state_before = []
    for rows, p in zip(chunks, prep):
        state_before.append([s.astype(BF16) for s in state])
        for h, (kc, vc) in enumerate(heads):
            upd = _dot(p["k_st"][:, kc].T.astype(BF16), v_ref[rows, vc])
            decay_col = jnp.broadcast_to(p["decay"][:, kc], (B_DK, B_DK)).T
            state[h] = state[h] * jnp.concatenate([decay_col, decay_col], axis=1) + upd
    for h in range(B_HEADS):
        s_scr[h] = state[h]

    yb = []
    for c, (rows, p) in enumerate(zip(chunks, prep)):
        gb = gb_ref[rows, :].astype(F32)
        parts = []
        for h, (kc, vc) in enumerate(heads):
            o = _dot(jnp.concatenate([scores[c][h], p["q_int"][:, kc]], axis=1),
                     jnp.concatenate([v_ref[rows, vc], state_before[c][h]], axis=0))
            parts.append((_rms(o, ggla_ref[...]) * gb[:, vc]).astype(BF16))
        yb.append(jnp.concatenate(parts, axis=1))

    mix = (sga_ref[...].astype(F32) * _dot(jnp.concatenate(ya, axis=0), wpa_ref[...])
           + sgb_ref[...].astype(F32) * _dot(jnp.concatenate(yb, axis=0), wpb_ref[...]))
    x2_ref[...] = x1_ref[...] + _dot(mix.astype(BF16), wo_ref[...])

    @pl.when(t == pl.num_programs(1) - 1)
    def _():
        sout_ref[0] = s_scr[...]


def _sample_mixer_kernel(uv_ref, qk_ref, v_ref, gb_ref, la_ref, s_ref, ws0_ref, bs0_ref, ggla_ref,
                         ya_ref, yb_ref, sout_ref, cv_ref, o_scr, *, nb):
    vn = uv_ref[:, A_WIDTH:]
    cv_ref[...] = vn
    ya_ref[...] = uv_ref[:, :A_WIDTH] * (vn * ws0_ref[...] + bs0_ref[...])

    a = jnp.exp(la_ref[...])
    qk = qk_ref[...]
    qs = qk[:, :B_KW]
    k = qk[:, B_KW:]
    v = v_ref[...]
    pad = jnp.zeros((LANES - 3 * nb, B_DK), F32)
    for h in range(B_HEADS):
        kc = slice(h * B_DK, (h + 1) * B_DK)
        vc = slice(h * B_DV, (h + 1) * B_DV)
        xt = jnp.concatenate([a[:, kc], k[:, kc], qs[:, kc], pad], axis=0).T
        for n in range(nb):
            s_new = (s_ref[n, h] * xt[:, n:n + 1]
                     + xt[:, nb + n:nb + n + 1] * v[n:n + 1, vc])
            sout_ref[n, h] = s_new
            o_scr[n:n + 1, vc] = jnp.sum(xt[:, 2 * nb + n:2 * nb + n + 1] * s_new,
                                         axis=0, keepdims=True)
    gb = gb_ref[...]
    for h in range(B_HEADS):
        vc = slice(h * B_DV, (h + 1) * B_DV)
        yb_ref[:, vc] = _rms(o_scr[:, vc], ggla_ref[...]) * gb[:, vc]


def _merge_kernel(ya_ref, yb_ref, ga_ref, gb_ref, x1_ref, wpa_ref, wpb_ref, wo_ref, x2_ref):
    mix = (ga_ref[...].astype(F32) * _dot(ya_ref[...], wpa_ref[...])
           + gb_ref[...].astype(F32) * _dot(yb_ref[...], wpb_ref[...]))
    x2_ref[...] = x1_ref[...] + _dot(mix.astype(BF16), wo_ref[...])


def _ffn2_body(ins, weights, outs, *, final_norm):
    (x_ref, p_ref), (y_ref,) = ins, outs
    g2_ref, w2i_ref, w2o_ref, gple_ref, wpg_ref, wple_ref, gfin_ref = weights
    x3 = _swiglu_residual(x_ref[...], g2_ref, w2i_ref, w2o_ref)
    gate = jax.nn.sigmoid(_dot(_rms(x3, gple_ref[...]).astype(BF16), wpg_ref[...]))
    x4 = x3 + _dot(p_ref[...].astype(BF16), wple_ref[...]) * gate
    y_ref[...] = _rms(x4, gfin_ref[...]) if final_norm else x4


def _resident(shape):
    zeros = (0,) * len(shape)
    return pl.BlockSpec(shape, lambda *_: zeros, pipeline_mode=pl.Buffered(1))


def _rows(tm, width, colblk=0):
    return pl.BlockSpec((tm, width), lambda i: (i, colblk))


def _params(n_axes):
    return pltpu.CompilerParams(dimension_semantics=("arbitrary",) * n_axes,
                                vmem_limit_bytes=VMEM_LIMIT)


def _cast_specs(weights, n_steps):
    specs, shapes = [], []
    for w in weights:
        rows, cols = w.shape
        n_blocks = max(n for n in range(1, n_steps + 1) if rows % (n * BF16_SUBLANES) == 0)
        specs.append(pl.BlockSpec((rows // n_blocks, cols),
                                  lambda i, n=n_blocks: (jnp.minimum(i, n - 1), 0)))
        shapes.append(jax.ShapeDtypeStruct(w.shape, BF16))
    return specs, shapes


def _two_group_call(name, body, prompt_in, sample_in, weights, prompt_out, sample_out, to_cast=()):
    m, ms = prompt_in[0].shape[0], sample_in[0].shape[0]
    assert m % TOKEN_TILE == 0
    n = m // TOKEN_TILE

    def p_spec(cols):
        return pl.BlockSpec((TOKEN_TILE, cols), lambda i: (jnp.minimum(i, n - 1), 0))

    def s_spec(cols):
        return pl.BlockSpec((ms, cols), lambda i: (0, 0))

    cast_specs, cast_shapes = _cast_specs(to_cast, n)
    outs = pl.pallas_call(
        functools.partial(_two_group_kernel, body=body, n_in=len(prompt_in), n_w=len(weights),
                          n_out=len(prompt_out), n_cast=len(to_cast)),
        grid=(n + 1,),
        in_specs=[p_spec(a.shape[1]) for a in prompt_in] + [s_spec(a.shape[1]) for a in sample_in]
                 + [_resident(w.shape) for w in weights] + cast_specs,
        out_specs=[p_spec(c) for c, _ in prompt_out] + [s_spec(c) for c, _ in sample_out]
                  + cast_specs,
        out_shape=[jax.ShapeDtypeStruct((m, c), dt) for c, dt in prompt_out]
                  + [jax.ShapeDtypeStruct((ms, c), dt) for c, dt in sample_out] + cast_shapes,
        compiler_params=_params(1),
        name=name,
    )(*prompt_in, *sample_in, *weights, *to_cast)
    k = len(prompt_out)
    return outs[:k], outs[k:2 * k], outs[2 * k:]


def _mixer(p, la, x1, n_seq, seq, ws, bsp_t, ggla, wpa, wpb, wo):
    tt = MIXER_TILE
    nt = seq // tt
    m = n_seq * seq

    def seg(colblk, width=D_MODEL):
        return pl.BlockSpec((tt, width), lambda b, t: (b * nt + t, colblk))

    small = [ws, bsp_t, ggla, wpa, wpb, wo]
    return pl.pallas_call(
        functools.partial(_mixer_kernel, n_chunks=tt // CHUNK),
        grid=(n_seq, nt),
        in_specs=[seg(0), seg(1), seg(2), seg(3), seg(0, B_KW), seg(4), seg(5), seg(0)]
                 + [_resident(w.shape) for w in small],
        out_specs=[seg(0),
                   pl.BlockSpec((1, B_HEADS, B_DK, B_DV), lambda b, t: (b, 0, 0, 0))],
        out_shape=[jax.ShapeDtypeStruct((m, D_MODEL), F32),
                   jax.ShapeDtypeStruct((n_seq, B_HEADS, B_DK, B_DV), F32)],
        scratch_shapes=[pltpu.VMEM((B_HEADS, B_DK, B_DV), F32)],
        compiler_params=_params(2),
        name="mixer",
    )(p, p, p, p, la, p, p, x1, *small)


def _sample_mixer(p, la, state, ws0, bs0, ggla):
    m = p.shape[0]
    nb = 8

    def seg(colblk):
        return pl.BlockSpec((nb, D_MODEL), lambda i: (i, colblk))

    state_spec = pl.BlockSpec((nb, B_HEADS, B_DK, B_DV), lambda i: (i, 0, 0, 0))
    small = [ws0, bs0, ggla]
    return pl.pallas_call(
        functools.partial(_sample_mixer_kernel, nb=nb),
        grid=(m // nb,),
        in_specs=[seg(0), seg(1), seg(2), seg(3), _rows(nb, B_KW), state_spec]
                 + [_resident(w.shape) for w in small],
        out_specs=[_rows(nb, A_WIDTH), _rows(nb, B_VW), state_spec, _rows(nb, A_WIDTH)],
        out_shape=[jax.ShapeDtypeStruct((m, A_WIDTH), F32),
                   jax.ShapeDtypeStruct((m, B_VW), F32),
                   jax.ShapeDtypeStruct(state.shape, F32),
                   jax.ShapeDtypeStruct((m, A_WIDTH), F32)],
        scratch_shapes=[pltpu.VMEM((nb, B_VW), F32)],
        compiler_params=_params(1),
        name="sample_mixer",
    )(p, p, p, p, la, state, *small)


def _merge(ya, yb, p, x1, wpa, wpb, wo):
    tm = m = x1.shape[0]
    return pl.pallas_call(
        _merge_kernel,
        grid=(m // tm,),
        in_specs=[_rows(tm, A_WIDTH), _rows(tm, B_VW), _rows(tm, D_MODEL, 4),
                  _rows(tm, D_MODEL, 5), _rows(tm, D_MODEL), _resident(wpa.shape),
                  _resident(wpb.shape), _resident(wo.shape)],
        out_specs=_rows(tm, D_MODEL),
        out_shape=jax.ShapeDtypeStruct((m, D_MODEL), F32),
        compiler_params=_params(1),
        name="merge",
    )(ya, yb, p, p, x1, wpa, wpb, wo)


def kernel(x_prompt, x_sample, p_prompt, p_sample, state_gla, g_ffn1, w_ffn1_in, w_ffn1_out, g_mix,
           w_in, ln_v_g, ln_v_b, w_spatial, b_spatial, w_gate_up, b_gate, g_gla_out, w_proj_a,
           w_proj_b, w_out, g_ffn2, w_ffn2_in, w_ffn2_out, g_ple, w_ple_gate, w_ple, g_final):
    depth = w_in.shape[0]
    n_seq, seq, _ = x_prompt.shape
    n_dec, dec_seq, _ = x_sample.shape
    assert dec_seq == 1 and seq % MIXER_TILE == 0

    def row(vec):
        return vec.reshape(1, -1).astype(F32)

    assert sum(IN_SIZES[:6]) == LR_LO and sum(IN_SIZES) == GATE_LO + 2 * D_MODEL
    xp = x_prompt.reshape(n_seq * seq, D_MODEL)
    xs = x_sample.reshape(n_dec, D_MODEL)
    gfin = row(g_final)
    sp_list, ss_list, vs_list = [], [], []
    for i in range(depth):
        w1i, w1o = w_ffn1_in[i].astype(BF16), w_ffn1_out[i].astype(BF16)
        wgu = jnp.pad(w_gate_up[i], ((0, LANES - GATE_RANK), (0, 0))).astype(BF16)
        g1, gmix, g2, gple = row(g_ffn1[i]), row(g_mix[i]), row(g_ffn2[i]), row(g_ple[i])
        lng, lnb, bg, ggla = row(ln_v_g[i]), row(ln_v_b[i]), row(b_gate[i]), row(g_gla_out[i])
        ws0 = row(jnp.repeat(w_spatial[i][:, 0, 0], A_GROUP_DIM))
        bs0 = row(jnp.repeat(b_spatial[i][:, 0], A_GROUP_DIM))

        wide, act = (D_MODEL, F32), (D_MODEL, BF16)
        (x1p, hp), (x1s, hs), (w_inb,) = _two_group_call(
            "ffn1", _ffn1_body, [xp], [xs], [g1, w1i, w1o, gmix], [wide, act], [wide, act],
            to_cast=[w_in[i].T])
        later = [w_proj_a[i], w_proj_b[i], w_out[i], w_ffn2_in[i], w_ffn2_out[i], w_ple_gate[i],
                 w_ple[i]]
        (pp, lap), (ps, las), (wpa, wpb, wo, w2i, w2o, wpg, wple) = _two_group_call(
            "inproj", _inproj_body, [hp], [hs], [w_inb, wgu, bg, lng, lnb],
            [(P_COLS, BF16), (B_KW, F32)], [(P_COLS, F32), (B_KW, F32)], to_cast=later)

        x2p, sp = _mixer(pp, lap, x1p, n_seq, seq, w_spatial[i], b_spatial[i].T, ggla, wpa, wpb,
                         wo)
        ya, yb, ss, cv = _sample_mixer(ps, las, state_gla[i], ws0, bs0, ggla)
        x2s = _merge(ya.astype(BF16), yb.astype(BF16), ps, x1s, wpa, wpb, wo)

        ple_p = p_prompt[i].reshape(n_seq * seq, PLE_DIM)
        ple_s = p_sample[i].reshape(n_dec, PLE_DIM)
        (xp,), (xs,), _ = _two_group_call(
            "ffn2", functools.partial(_ffn2_body, final_norm=i == depth - 1), [x2p, ple_p],
            [x2s, ple_s], [g2, w2i, w2o, gple, wpg, wple, gfin], [wide], [wide])

        sp_list.append(sp)
        ss_list.append(ss)
        vs_list.append(cv.reshape(n_dec, dec_seq, A_WIDTH))
    return (xp.reshape(n_seq, seq, D_MODEL), xs.reshape(n_dec, dec_seq, D_MODEL),
            jnp.stack(sp_list), jnp.stack(ss_list), jnp.stack(vs_list))
```

```python
import functools

import jax
import jax.numpy as jnp
from jax import lax
from jax.experimental import pallas as pl
from jax.experimental.pallas import tpu as pltpu

D_MODEL = 1024
D_FF = 2816
PLE_DIM = 256
CHUNK = 128
A_GROUPS = 4
A_GROUP_DIM = 128
A_WIDTH = A_GROUPS * A_GROUP_DIM
B_HEADS = 4
B_DK = 128
B_DV = 256
B_KW = B_HEADS * B_DK
B_VW = B_HEADS * B_DV
GATE_RANK = 16
GATE_NORM = 16.0
EPS = 1e-6
IN_SIZES = (A_WIDTH, A_WIDTH, B_KW, B_KW, B_VW, B_VW, GATE_RANK, D_MODEL, D_MODEL)

LANES = 128
P_COLS = 6 * D_MODEL
QK_LO = 2 * A_WIDTH
VB_LO = QK_LO + 2 * B_KW
GB_LO = VB_LO + B_VW
LR_LO = GB_LO + B_VW
GATE_LO = LR_LO + GATE_RANK
FF_CHUNK = 256
HALF = CHUNK // 2
MIXER_TILE = 4 * CHUNK
TOKEN_TILE = 512
BF16_SUBLANES = 16
VMEM_LIMIT = 56 * 1024 * 1024

F32 = jnp.float32
BF16 = jnp.bfloat16


def _dot(a, b):
    return jnp.dot(a, b, preferred_element_type=F32)


def _dot_nt(a, b):
    return lax.dot_general(a, b, (((1,), (1,)), ((), ())), preferred_element_type=F32)


def _rms(x, g):
    return x * lax.rsqrt(jnp.mean(x * x, axis=-1, keepdims=True) + EPS) * g


def _log_sigmoid(x):
    return jnp.minimum(x, 0.0) - jnp.log1p(jnp.exp(-jnp.abs(x)))


def _layernorm(x, g, b):
    mu = jnp.mean(x, axis=-1, keepdims=True)
    xc = x - mu
    var = jnp.mean(xc * xc, axis=-1, keepdims=True)
    return xc * lax.rsqrt(var + EPS) * g + b


def _swiglu_residual(x, g_ref, w_in_ref, w_out_ref):
    h = _rms(x, g_ref[...]).astype(BF16)
    acc = None
    for lo in range(0, D_FF, FF_CHUNK):
        gate = _dot(h, w_in_ref[:, lo:lo + FF_CHUNK])
        up = _dot(h, w_in_ref[:, D_FF + lo:D_FF + lo + FF_CHUNK])
        act = (jax.nn.silu(gate) * up).astype(BF16)
        part = _dot(act, w_out_ref[lo:lo + FF_CHUNK, :])
        acc = part if acc is None else acc + part
    return x + 0.5 * acc


def _cast_blocks(src_refs, dst_refs):
    for src, dst in zip(src_refs, dst_refs):
        dst[...] = src[...].astype(dst.dtype)


def _two_group_kernel(*refs, body, n_in, n_w, n_out, n_cast):
    it = iter(refs)

    def take(k):
        return [next(it) for _ in range(k)]

    p_in, s_in, weights, cast_src = take(n_in), take(n_in), take(n_w), take(n_cast)
    p_out, s_out, cast_dst = take(n_out), take(n_out), take(n_cast)
    _cast_blocks(cast_src, cast_dst)
    step = pl.program_id(0)
    sample_step = pl.num_programs(0) - 1

    @pl.when(step < sample_step)
    def _():
        body(p_in, weights, p_out)

    @pl.when(step == sample_step)
    def _():
        body(s_in, weights, s_out)


def _ffn1_body(ins, weights, outs):
    (x_ref,), (g1_ref, w1i_ref, w1o_ref, gmix_ref), (x1_ref, h_ref) = ins, weights, outs
    x1 = _swiglu_residual(x_ref[...], g1_ref, w1i_ref, w1o_ref)
    x1_ref[...] = x1
    h_ref[...] = _rms(x1, gmix_ref[...]).astype(BF16)


def _inproj_body(ins, weights, outs):
    (h_ref,), (wt_ref, wgu_ref, bg_ref, lng_ref, lnb_ref), (p_ref, la_ref) = ins, weights, outs

    def put(lo, val):
        p_ref[:, lo:lo + val.shape[1]] = val.astype(p_ref.dtype)

    def proj(lo, width):
        return _dot_nt(h, wt_ref[lo:lo + width, :])

    h = h_ref[...]
    uv = proj(0, 2 * A_WIDTH)
    put(0, jax.nn.gelu(uv[:, :A_WIDTH]))
    put(A_WIDTH, _layernorm(jax.nn.gelu(uv[:, A_WIDTH:]), lng_ref[...], lnb_ref[...]))
    qk = proj(QK_LO, 2 * B_KW)
    put(QK_LO, qk[:, :B_KW] * (B_DK ** -0.5))
    put(QK_LO + B_KW, qk[:, B_KW:])
    put(VB_LO, proj(VB_LO, B_VW))
    put(GB_LO, jax.nn.silu(proj(GB_LO, B_VW)))
    lr = proj(LR_LO, LANES).astype(BF16)
    logit = _dot(lr, wgu_ref[...]) + bg_ref[...]
    la_ref[...] = _log_sigmoid(logit) * (1.0 / GATE_NORM)
    for j in range(2):
        put(LR_LO + j * D_MODEL, jax.nn.sigmoid(proj(GATE_LO + j * D_MODEL, D_MODEL)))


def _gla_prep(la, qk, ones_tril):
    la_hi = la.astype(BF16)
    la_lo = (la - la_hi.astype(F32)).astype(BF16)
    b = _dot(ones_tril, la_hi) + _dot(ones_tril, la_lo)
    r0 = b[HALF // 2 - 1:HALF // 2, :]
    r1 = b[HALF + HALF // 2 - 1:HALF + HALF // 2, :]
    b_mid = b[HALF - 1:HALF, :]
    b_end = b[CHUNK - 1:CHUNK, :]
    bq0 = b[:HALF] - r0
    bq1 = b[HALF:] - r1
    qs = qk[:, :B_KW]
    k = qk[:, B_KW:]
    a0 = qs[:HALF] * jnp.exp(bq0)
    a1 = qs[HALF:] * jnp.exp(bq1)
    k0 = k[:HALF] * jnp.exp(-bq0)
    k1 = k[HALF:] * jnp.exp(-bq1)
    q_off1 = a1 * jnp.exp(r1 - b_mid)
    k_off0 = k0 * jnp.exp(b_mid - r0)
    q_int = jnp.concatenate([a0 * jnp.exp(r0), q_off1 * jnp.exp(b_mid)], axis=0)
    k_st = jnp.concatenate([k_off0 * jnp.exp(b_end - b_mid), k1 * jnp.exp(b_end - r1)], axis=0)
    return dict(a0=a0.astype(BF16), a1=a1.astype(BF16), k0=k0.astype(BF16), k1=k1.astype(BF16),
                q_off1=q_off1.astype(BF16), k_off0=k_off0.astype(BF16), q_int=q_int.astype(BF16),
                k_st=k_st, decay=jnp.exp(b_end))


def _mixer_kernel(uv_ref, qk_ref, v_ref, gb_ref, la_ref, sga_ref, sgb_ref, x1_ref,
                  d_uv_ref, d_qk_ref, d_v_ref, d_gb_ref, d_la_ref, d_state_ref,
                  ws_ref, bsp_ref, ggla_ref, wpa_ref, wpb_ref, wo_ref, ws0_ref, bs0_ref,
                  x2_ref, sout_ref, d_ya_ref, d_yb_ref, d_sout_ref, d_cv_ref,
                  s_scr, o_scr, *, n_chunks, n_dec):
    t = pl.program_id(1)

    @pl.when(t == 0)
    def _():
        s_scr[...] = jnp.zeros_like(s_scr)

    row = lax.broadcasted_iota(jnp.int32, (CHUNK, CHUNK), 0)
    col = lax.broadcasted_iota(jnp.int32, (CHUNK, CHUNK), 1)
    causal = row >= col
    ones_tril = causal.astype(BF16)
    ws_tril = [jnp.where(causal, ws_ref[g], 0.0).astype(BF16) for g in range(A_GROUPS)]
    zero_blk = jnp.zeros((HALF, B_DK), BF16)
    chunks = [slice(c * CHUNK, (c + 1) * CHUNK) for c in range(n_chunks)]
    heads = [(slice(h * B_DK, (h + 1) * B_DK), slice(h * B_DV, (h + 1) * B_DV))
             for h in range(B_HEADS)]

    ya = []
    for rows in chunks:
        u = uv_ref[rows, :A_WIDTH].astype(F32)
        vn = uv_ref[rows, A_WIDTH:]
        parts = []
        for g in range(A_GROUPS):
            cols = slice(g * A_GROUP_DIM, (g + 1) * A_GROUP_DIM)
            mixed = _dot(ws_tril[g], vn[:, cols]) + bsp_ref[:, g:g + 1]
            parts.append((u[:, cols] * mixed).astype(BF16))
        ya.append(jnp.concatenate(parts, axis=1))

    prep = [_gla_prep(la_ref[rows, :], qk_ref[rows, :].astype(F32), ones_tril) for rows in chunks]
    scores = []
    for p in prep:
        per_head = []
        for kc, _ in heads:
            q3 = jnp.concatenate([
                jnp.concatenate([p["a0"][:, kc], zero_blk, zero_blk], axis=1),
                jnp.concatenate([zero_blk, p["q_off1"][:, kc], p["a1"][:, kc]], axis=1)], axis=0)
            k3 = jnp.concatenate([
                jnp.concatenate([p["k0"][:, kc], p["k_off0"][:, kc], zero_blk], axis=1),
                jnp.concatenate([zero_blk, zero_blk, p["k1"][:, kc]], axis=1)], axis=0)
            per_head.append(jnp.where(causal, _dot_nt(q3, k3), 0.0).astype(BF16))
        scores.append(per_head)

    state = [s_scr[h] for h in range(B_HEADS)]
    state_before = []
    for rows, p in zip(chunks, prep):
        state_before.append([s.astype(BF16) for s in state])
        for h, (kc, vc) in enumerate(heads):
            upd = _dot(p["k_st"][:, kc].T.astype(BF16), v_ref[rows, vc])
            decay_col = jnp.broadcast_to(p["decay"][:, kc], (B_DK, B_DK)).T
            state[h] = state[h] * jnp.concatenate([decay_col, decay_col], axis=1) + upd
    for h in range(B_HEADS):
        s_scr[h] = state[h]

    yb = []
    for c, (rows, p) in enumerate(zip(chunks, prep)):
        gb = gb_ref[rows, :].astype(F32)
        parts = []
        for h, (kc, vc) in enumerate(heads):
            o = _dot(jnp.concatenate([scores[c][h], p["q_int"][:, kc]], axis=1),
                     jnp.concatenate([v_ref[rows, vc], state_before[c][h]], axis=0))
            parts.append((_rms(o, ggla_ref[...]) * gb[:, vc]).astype(BF16))
        yb.append(jnp.concatenate(parts, axis=1))

    pa = _dot(jnp.concatenate(ya, axis=0), wpa_ref[...])
    _sample_mixers(d_uv_ref.at[0], d_qk_ref.at[0], d_v_ref.at[0], d_gb_ref.at[0], d_la_ref.at[0],
                   d_state_ref, ws0_ref, bs0_ref, ggla_ref, d_ya_ref.at[0], d_yb_ref.at[0],
                   d_sout_ref, d_cv_ref.at[0], o_scr, n_dec)
    mix = (sga_ref[...].astype(F32) * pa
           + sgb_ref[...].astype(F32) * _dot(jnp.concatenate(yb, axis=0), wpb_ref[...]))
    x2_ref[...] = x1_ref[...] + _dot(mix.astype(BF16), wo_ref[...])

    @pl.when(t == pl.num_programs(1) - 1)
    def _():
        sout_ref[0] = s_scr[...]


def _sample_mixers(uv_ref, qk_ref, v_ref, gb_ref, la_ref, s_ref, ws0_ref, bs0_ref, ggla_ref,
                   ya_ref, yb_ref, sout_ref, cv_ref, o_scr, nb):
    vn = uv_ref[:, A_WIDTH:]
    cv_ref[...] = vn
    ya_ref[...] = uv_ref[:, :A_WIDTH] * (vn * ws0_ref[...] + bs0_ref[...])

    a = jnp.exp(la_ref[...])
    qk = qk_ref[...]
    qs = qk[:, :B_KW]
    k = qk[:, B_KW:]
    v = v_ref[...]
    pad = jnp.zeros((LANES - 3 * nb, B_DK), F32)
    for h in range(B_HEADS):
        kc = slice(h * B_DK, (h + 1) * B_DK)
        vc = slice(h * B_DV, (h + 1) * B_DV)
        xt = jnp.concatenate([a[:, kc], k[:, kc], qs[:, kc], pad], axis=0).T
        for n in range(nb):
            s_new = (s_ref[n, h] * xt[:, n:n + 1]
                     + xt[:, nb + n:nb + n + 1] * v[n:n + 1, vc])
            sout_ref[n, h] = s_new
            o_scr[n:n + 1, vc] = jnp.sum(xt[:, 2 * nb + n:2 * nb + n + 1] * s_new,
                                         axis=0, keepdims=True)
    gb = gb_ref[...]
    for h in range(B_HEADS):
        vc = slice(h * B_DV, (h + 1) * B_DV)
        yb_ref[:, vc] = _rms(o_scr[:, vc], ggla_ref[...]) * gb[:, vc]


def _merge_kernel(ya_ref, yb_ref, ga_ref, gb_ref, x1_ref, wpa_ref, wpb_ref, wo_ref, x2_ref):
    mix = (ga_ref[...].astype(F32) * _dot(ya_ref[...], wpa_ref[...])
           + gb_ref[...].astype(F32) * _dot(yb_ref[...], wpb_ref[...]))
    x2_ref[...] = x1_ref[...] + _dot(mix.astype(BF16), wo_ref[...])


def _ffn2_body(ins, weights, outs, *, final_norm):
    (x_ref, p_ref), (y_ref,) = ins, outs
    g2_ref, w2i_ref, w2o_ref, gple_ref, wpg_ref, wple_ref, gfin_ref = weights
    x3 = _swiglu_residual(x_ref[...], g2_ref, w2i_ref, w2o_ref)
    gate = jax.nn.sigmoid(_dot(_rms(x3, gple_ref[...]).astype(BF16), wpg_ref[...]))
    x4 = x3 + _dot(p_ref[...].astype(BF16), wple_ref[...]) * gate
    y_ref[...] = _rms(x4, gfin_ref[...]) if final_norm else x4


def _resident(shape):
    zeros = (0,) * len(shape)
    return pl.BlockSpec(shape, lambda *_: zeros, pipeline_mode=pl.Buffered(1))


def _rows(tm, width, colblk=0):
    return pl.BlockSpec((tm, width), lambda i: (i, colblk))


def _params(n_axes):
    return pltpu.CompilerParams(dimension_semantics=("arbitrary",) * n_axes,
                                vmem_limit_bytes=VMEM_LIMIT)


def _cast_specs(weights, n_steps):
    specs, shapes = [], []
    for w in weights:
        rows, cols = w.shape
        n_blocks = max(n for n in range(1, n_steps + 1) if rows % (n * BF16_SUBLANES) == 0)
        specs.append(pl.BlockSpec((rows // n_blocks, cols),
                                  lambda i, n=n_blocks: (jnp.minimum(i, n - 1), 0)))
        shapes.append(jax.ShapeDtypeStruct(w.shape, BF16))
    return specs, shapes


def _two_group_call(name, body, prompt_in, sample_in, weights, prompt_out, sample_out, to_cast=()):
    m, ms = prompt_in[0].shape[0], sample_in[0].shape[0]
    assert m % TOKEN_TILE == 0
    n = m // TOKEN_TILE

    def p_spec(cols):
        return pl.BlockSpec((TOKEN_TILE, cols), lambda i: (jnp.minimum(i, n - 1), 0))

    def s_spec(cols):
        return pl.BlockSpec((ms, cols), lambda i: (0, 0))

    cast_specs, cast_shapes = _cast_specs(to_cast, n)
    outs = pl.pallas_call(
        functools.partial(_two_group_kernel, body=body, n_in=len(prompt_in), n_w=len(weights),
                          n_out=len(prompt_out), n_cast=len(to_cast)),
        grid=(n + 1,),
        in_specs=[p_spec(a.shape[1]) for a in prompt_in] + [s_spec(a.shape[1]) for a in sample_in]
                 + [_resident(w.shape) for w in weights] + cast_specs,
        out_specs=[p_spec(c) for c, _ in prompt_out] + [s_spec(c) for c, _ in sample_out]
                  + cast_specs,
        out_shape=[jax.ShapeDtypeStruct((m, c), dt) for c, dt in prompt_out]
                  + [jax.ShapeDtypeStruct((ms, c), dt) for c, dt in sample_out] + cast_shapes,
        compiler_params=_params(1),
        name=name,
    )(*prompt_in, *sample_in, *weights, *to_cast)
    k = len(prompt_out)
    return outs[:k], outs[k:2 * k], outs[2 * k:]


def _mixer(p, la, x1, n_seq, seq, d_p, d_la, d_state, ws, bsp_t, ggla, wpa, wpb, wo, ws0, bs0):
    tt = MIXER_TILE
    nt = seq // tt
    m = n_seq * seq
    steps = n_seq * nt
    n_all = d_p.shape[0]
    assert n_all % steps == 0
    n_dec = n_all // steps

    def seg(colblk, width=D_MODEL):
        return pl.BlockSpec((tt, width), lambda b, t: (b * nt + t, colblk))

    def d_seg(colblk, width=D_MODEL):
        return pl.BlockSpec((1, n_dec, width), lambda b, t: (b * nt + t, 0, colblk))

    d_state_spec = pl.BlockSpec((n_dec, B_HEADS, B_DK, B_DV), lambda b, t: (b * nt + t, 0, 0, 0))
    d_p3 = d_p.reshape(steps, n_dec, P_COLS)
    small = [ws, bsp_t, ggla, wpa, wpb, wo, ws0, bs0]
    x2, sp, ya, yb, ss, cv = pl.pallas_call(
        functools.partial(_mixer_kernel, n_chunks=tt // CHUNK, n_dec=n_dec),
        grid=(n_seq, nt),
        in_specs=[seg(0), seg(1), seg(2), seg(3), seg(0, B_KW), seg(4), seg(5), seg(0),
                  d_seg(0), d_seg(1), d_seg(2), d_seg(3), d_seg(0, B_KW), d_state_spec]
                 + [_resident(w.shape) for w in small],
        out_specs=[seg(0),
                   pl.BlockSpec((1, B_HEADS, B_DK, B_DV), lambda b, t: (b, 0, 0, 0)),
                   d_seg(0, A_WIDTH), d_seg(0, B_VW), d_state_spec, d_seg(0, A_WIDTH)],
        out_shape=[jax.ShapeDtypeStruct((m, D_MODEL), F32),
                   jax.ShapeDtypeStruct((n_seq, B_HEADS, B_DK, B_DV), F32),
                   jax.ShapeDtypeStruct((steps, n_dec, A_WIDTH), F32),
                   jax.ShapeDtypeStruct((steps, n_dec, B_VW), F32),
                   jax.ShapeDtypeStruct(d_state.shape, F32),
                   jax.ShapeDtypeStruct((steps, n_dec, A_WIDTH), F32)],
        scratch_shapes=[pltpu.VMEM((B_HEADS, B_DK, B_DV), F32), pltpu.VMEM((n_dec, B_VW), F32)],
        compiler_params=_params(2),
        name="mixer",
    )(p, p, p, p, la, p, p, x1, d_p3, d_p3, d_p3, d_p3, d_la.reshape(steps, n_dec, B_KW), d_state,
      *small)
    return (x2, sp, ya.reshape(n_all, A_WIDTH), yb.reshape(n_all, B_VW), ss,
            cv.reshape(n_all, A_WIDTH))


def _merge(ya, yb, p, x1, wpa, wpb, wo):
    tm = m = x1.shape[0]
    return pl.pallas_call(
        _merge_kernel,
        grid=(m // tm,),
        in_specs=[_rows(tm, A_WIDTH), _rows(tm, B_VW), _rows(tm, D_MODEL, 4),
                  _rows(tm, D_MODEL, 5), _rows(tm, D_MODEL), _resident(wpa.shape),
                  _resident(wpb.shape), _resident(wo.shape)],
        out_specs=_rows(tm, D_MODEL),
        out_shape=jax.ShapeDtypeStruct((m, D_MODEL), F32),
        compiler_params=_params(1),
        name="merge",
    )(ya, yb, p, p, x1, wpa, wpb, wo)


def kernel(x_prompt, x_sample, p_prompt, p_sample, state_gla, g_ffn1, w_ffn1_in, w_ffn1_out, g_mix,
           w_in, ln_v_g, ln_v_b, w_spatial, b_spatial, w_gate_up, b_gate, g_gla_out, w_proj_a,
           w_proj_b, w_out, g_ffn2, w_ffn2_in, w_ffn2_out, g_ple, w_ple_gate, w_ple, g_final):
    depth = w_in.shape[0]
    n_seq, seq, _ = x_prompt.shape
    n_dec, dec_seq, _ = x_sample.shape
    assert dec_seq == 1 and seq % MIXER_TILE == 0

    def row(vec):
        return vec.reshape(1, -1).astype(F32)

    assert sum(IN_SIZES[:6]) == LR_LO and sum(IN_SIZES) == GATE_LO + 2 * D_MODEL
    xp = x_prompt.reshape(n_seq * seq, D_MODEL)
    xs = x_sample.reshape(n_dec, D_MODEL)
    gfin = row(g_final)
    sp_list, ss_list, vs_list = [], [], []
    for i in range(depth):
        w1i, w1o = w_ffn1_in[i].astype(BF16), w_ffn1_out[i].astype(BF16)
        wgu = jnp.pad(w_gate_up[i], ((0, LANES - GATE_RANK), (0, 0))).astype(BF16)
        g1, gmix, g2, gple = row(g_ffn1[i]), row(g_mix[i]), row(g_ffn2[i]), row(g_ple[i])
        lng, lnb, bg, ggla = row(ln_v_g[i]), row(ln_v_b[i]), row(b_gate[i]), row(g_gla_out[i])
        ws0 = row(jnp.repeat(w_spatial[i][:, 0, 0], A_GROUP_DIM))
        bs0 = row(jnp.repeat(b_spatial[i][:, 0], A_GROUP_DIM))

        wide, act = (D_MODEL, F32), (D_MODEL, BF16)
        (x1p, hp), (x1s, hs), (w_inb,) = _two_group_call(
            "ffn1", _ffn1_body, [xp], [xs], [g1, w1i, w1o, gmix], [wide, act], [wide, act],
            to_cast=[w_in[i].T])
        later = [w_proj_a[i], w_proj_b[i], w_out[i], w_ffn2_in[i], w_ffn2_out[i], w_ple_gate[i],
                 w_ple[i]]
        (pp, lap), (ps, las), (wpa, wpb, wo, w2i, w2o, wpg, wple) = _two_group_call(
            "inproj", _inproj_body, [hp], [hs], [w_inb, wgu, bg, lng, lnb],
            [(P_COLS, BF16), (B_KW, F32)], [(P_COLS, F32), (B_KW, F32)], to_cast=later)

        x2p, sp, ya, yb, ss, cv = _mixer(pp, lap, x1p, n_seq, seq, ps, las, state_gla[i],
                                         w_spatial[i], b_spatial[i].T, ggla, wpa, wpb, wo, ws0, bs0)
        x2s = _merge(ya.astype(BF16), yb.astype(BF16), ps, x1s, wpa, wpb, wo)

        ple_p = p_prompt[i].reshape(n_seq * seq, PLE_DIM)
        ple_s = p_sample[i].reshape(n_dec, PLE_DIM)
        (xp,), (xs,), _ = _two_group_call(
            "ffn2", functools.partial(_ffn2_body, final_norm=i == depth - 1), [x2p, ple_p],
            [x2s, ple_s], [g2, w2i, w2o, gple, wpg, wple, gfin], [wide], [wide])

        sp_list.append(sp)
        ss_list.append(ss)
        vs_list.append(cv.reshape(n_dec, dec_seq, A_WIDTH))
    return (xp.reshape(n_seq, seq, D_MODEL), xs.reshape(n_dec, dec_seq, D_MODEL),
            jnp.stack(sp_list), jnp.stack(ss_list), jnp.stack(vs_list))
```

```python
import functools

import jax
import jax.numpy as jnp
from jax import lax
from jax.experimental import pallas as pl
from jax.experimental.pallas import tpu as pltpu

D_MODEL = 1024
D_FF = 2816
PLE_DIM = 256
CHUNK = 128
A_GROUPS = 4
A_GROUP_DIM = 128
A_WIDTH = A_GROUPS * A_GROUP_DIM
B_HEADS = 4
B_DK = 128
B_DV = 256
B_KW = B_HEADS * B_DK
B_VW = B_HEADS * B_DV
GATE_RANK = 16
GATE_NORM = 16.0
EPS = 1e-6
IN_SIZES = (A_WIDTH, A_WIDTH, B_KW, B_KW, B_VW, B_VW, GATE_RANK, D_MODEL, D_MODEL)

LANES = 128
P_COLS = 6 * D_MODEL
QK_LO = 2 * A_WIDTH
VB_LO = QK_LO + 2 * B_KW
GB_LO = VB_LO + B_VW
LR_LO = GB_LO + B_VW
GATE_LO = LR_LO + GATE_RANK
FF_CHUNK = 256
HALF = CHUNK // 2
MIXER_TILE = 4 * CHUNK
TOKEN_TILE = 512
SUB_TILE = 512
BF16_SUBLANES = 16
VMEM_LIMIT = 56 * 1024 * 1024

F32 = jnp.float32
BF16 = jnp.bfloat16


def _dot(a, b):
    return jnp.dot(a, b, preferred_element_type=F32)


def _dot_nt(a, b):
    return lax.dot_general(a, b, (((1,), (1,)), ((), ())), preferred_element_type=F32)


def _rms(x, g):
    return x * lax.rsqrt(jnp.mean(x * x, axis=-1, keepdims=True) + EPS) * g


def _log_sigmoid(x):
    return jnp.minimum(x, 0.0) - jnp.log1p(jnp.exp(-jnp.abs(x)))


def _layernorm(x, g, b):
    mu = jnp.mean(x, axis=-1, keepdims=True)
    xc = x - mu
    var = jnp.mean(xc * xc, axis=-1, keepdims=True)
    return xc * lax.rsqrt(var + EPS) * g + b


def _swiglu_stage(x_ref, g_ref, w_in_ref, w_out_ref, finish):
    n_rows = x_ref.shape[0]
    pending = None
    for r0 in range(0, n_rows, SUB_TILE):
        rows = slice(r0, min(r0 + SUB_TILE, n_rows))
        x = x_ref[rows, :]
        h = _rms(x, g_ref[...]).astype(BF16)
        acc = None
        for ci, lo in enumerate(range(0, D_FF, FF_CHUNK)):
            hi = min(lo + FF_CHUNK, D_FF)
            gate = _dot(h, w_in_ref[:, lo:hi])
            up = _dot(h, w_in_ref[:, D_FF + lo:D_FF + hi])
            act = (jax.nn.silu(gate) * up).astype(BF16)
            part = _dot(act, w_out_ref[lo:hi, :])
            acc = part if acc is None else acc + part
            if ci == 1 and pending is not None:
                pending()
                pending = None
        pending = functools.partial(finish, rows, x + 0.5 * acc)
    pending()


def _cast_blocks(src_refs, dst_refs):
    for src, dst in zip(src_refs, dst_refs):
        dst[...] = src[...].astype(dst.dtype)


def _two_group_kernel(*refs, body, n_in, n_w, n_out, n_cast):
    it = iter(refs)

    def take(k):
        return [next(it) for _ in range(k)]

    p_in, s_in, weights, cast_src = take(n_in), take(n_in), take(n_w), take(n_cast)
    p_out, s_out, cast_dst = take(n_out), take(n_out), take(n_cast)
    _cast_blocks(cast_src, cast_dst)
    step = pl.program_id(0)
    sample_step = pl.num_programs(0) - 1

    @pl.when(step < sample_step)
    def _():
        body(p_in, weights, p_out)

    @pl.when(step == sample_step)
    def _():
        body(s_in, weights, s_out)


def _ffn1_body(ins, weights, outs):
    (x_ref,), (g1_ref, w1i_ref, w1o_ref, gmix_ref), (x1_ref, h_ref) = ins, weights, outs

    def finish(rows, x1):
        x1_ref[rows, :] = x1
        h_ref[rows, :] = _rms(x1, gmix_ref[...]).astype(BF16)

    _swiglu_stage(x_ref, g1_ref, w1i_ref, w1o_ref, finish)


def _inproj_body(ins, weights, outs):
    (h_ref,), (wt_ref, wgu_ref, bg_ref, lng_ref, lnb_ref), (p_ref, la_ref) = ins, weights, outs

    def put(lo, val):
        p_ref[:, lo:lo + val.shape[1]] = val.astype(p_ref.dtype)

    def proj(lo, width):
        return _dot_nt(h, wt_ref[lo:lo + width, :])

    h = h_ref[...]
    lr = proj(LR_LO, LANES).astype(BF16)
    logit = _dot(lr, wgu_ref[...]) + bg_ref[...]
    la_ref[...] = _log_sigmoid(logit) * (1.0 / GATE_NORM)
    half = D_MODEL // 2
    segments = [(0, 0, jax.nn.gelu),
                (A_WIDTH, A_WIDTH,
                 lambda z: _layernorm(jax.nn.gelu(z), lng_ref[...], lnb_ref[...])),
                (QK_LO, QK_LO, lambda z: z * (B_DK ** -0.5)),
                (QK_LO + B_KW, QK_LO + B_KW, lambda z: z)]
    segments += [(VB_LO + j * half, VB_LO + j * half, lambda z: z) for j in range(2)]
    segments += [(GATE_LO + j * half, LR_LO + j * half, jax.nn.sigmoid) for j in range(4)]
    segments += [(GB_LO + j * half, GB_LO + j * half, jax.nn.silu) for j in range(2)]
    for w_lo, p_lo, act in segments:
        put(p_lo, act(proj(w_lo, half)))


def _sample_inproj_kernel(h_ref, wt_ref, wgu_ref, bg_ref, lng_ref, lnb_ref, p_ref, la_ref):
    _inproj_body([h_ref], [wt_ref, wgu_ref, bg_ref, lng_ref, lnb_ref], [p_ref, la_ref])


def _gla_prep(la, qk, ones_tril):
    la_hi = la.astype(BF16)
    la_lo = (la - la_hi.astype(F32)).astype(BF16)
    b = _dot(ones_tril, la_hi) + _dot(ones_tril, la_lo)
    r0 = b[HALF // 2 - 1:HALF // 2, :]
    r1 = b[HALF + HALF // 2 - 1:HALF + HALF // 2, :]
    b_mid = b[HALF - 1:HALF, :]
    b_end = b[CHUNK - 1:CHUNK, :]
    bq0 = b[:HALF] - r0
    bq1 = b[HALF:] - r1
    qs = qk[:, :B_KW]
    k = qk[:, B_KW:]
    a0 = qs[:HALF] * jnp.exp(bq0)
    a1 = qs[HALF:] * jnp.exp(bq1)
    k0 = k[:HALF] * jnp.exp(-bq0)
    k1 = k[HALF:] * jnp.exp(-bq1)
    q_off1 = a1 * jnp.exp(r1 - b_mid)
    k_off0 = k0 * jnp.exp(b_mid - r0)
    q_int = jnp.concatenate([a0 * jnp.exp(r0), q_off1 * jnp.exp(b_mid)], axis=0)
    k_st = jnp.concatenate([k_off0 * jnp.exp(b_end - b_mid), k1 * jnp.exp(b_end - r1)], axis=0)
    return dict(a0=a0.astype(BF16), a1=a1.astype(BF16), k0=k0.astype(BF16), k1=k1.astype(BF16),
                q_off1=q_off1.astype(BF16), k_off0=k_off0.astype(BF16), q_int=q_int.astype(BF16),
                k_st=k_st, decay=jnp.exp(b_end))


def _mixer_kernel(*refs, n_chunks, n_dec, n_cast):
    it = iter(refs)

    def take(k):
        return [next(it) for _ in range(k)]

    h_ref, x1_ref, d_uv_ref, d_qk_ref, d_v_ref, d_gb_ref, d_la_ref, d_state_ref = take(8)
    wt_ref, wgu_ref, bg_ref, lng_ref, lnb_ref = take(5)
    ws_ref, bsp_ref, ggla_ref, wpa_ref, wpb_ref, wo_ref, ws0_ref, bs0_ref = take(8)
    cast_src = take(n_cast)
    x2_ref, sout_ref, d_ya_ref, d_yb_ref, d_sout_ref, d_cv_ref = take(6)
    cast_dst = take(n_cast)
    s_scr, o_scr, p_scr, la_ref = take(4)
    t = pl.program_id(1)

    @pl.when(t == 0)
    def _():
        s_scr[...] = jnp.zeros_like(s_scr)

    _inproj_body([h_ref], [wt_ref, wgu_ref, bg_ref, lng_ref, lnb_ref], [p_scr, la_ref])
    uv_ref, qk_ref, v_ref, gb_ref, sga_ref, sgb_ref = (
        p_scr.at[:, j * D_MODEL:(j + 1) * D_MODEL] for j in range(6))

    row = lax.broadcasted_iota(jnp.int32, (CHUNK, CHUNK), 0)
    col = lax.broadcasted_iota(jnp.int32, (CHUNK, CHUNK), 1)
    causal = row >= col
    ones_tril = causal.astype(BF16)
    ws_tril = [jnp.where(causal, ws_ref[g], 0.0).astype(BF16) for g in range(A_GROUPS)]
    zero_blk = jnp.zeros((HALF, B_DK), BF16)
    chunks = [slice(c * CHUNK, (c + 1) * CHUNK) for c in range(n_chunks)]
    heads = [(slice(h * B_DK, (h + 1) * B_DK), slice(h * B_DV, (h + 1) * B_DV))
             for h in range(B_HEADS)]

    ya = []
    for rows in chunks:
        u = uv_ref[rows, :A_WIDTH].astype(F32)
        vn = uv_ref[rows, A_WIDTH:]
        parts = []
        for g in range(A_GROUPS):
            cols = slice(g * A_GROUP_DIM, (g + 1) * A_GROUP_DIM)
            mixed = _dot(ws_tril[g], vn[:, cols]) + bsp_ref[:, g:g + 1]
            parts.append((u[:, cols] * mixed).astype(BF16))
        ya.append(jnp.concatenate(parts, axis=1))

    prep = [_gla_prep(la_ref[rows, :], qk_ref[rows, :].astype(F32), ones_tril) for rows in chunks]
    scores = []
    for p in prep:
        per_head = []
        for kc, _ in heads:
            q3 = jnp.concatenate([
                jnp.concatenate([p["a0"][:, kc], zero_blk, zero_blk], axis=1),
                jnp.concatenate([zero_blk, p["q_off1"][:, kc], p["a1"][:, kc]], axis=1)], axis=0)
            k3 = jnp.concatenate([
                jnp.concatenate([p["k0"][:, kc], p["k_off0"][:, kc], zero_blk], axis=1),
                jnp.concatenate([zero_blk, zero_blk, p["k1"][:, kc]], axis=1)], axis=0)
            per_head.append(jnp.where(causal, _dot_nt(q3, k3), 0.0).astype(BF16))
        scores.append(per_head)

    state = [s_scr[h] for h in range(B_HEADS)]
    state_before = []
    for rows, p in zip(chunks, prep):
        state_before.append([s.astype(BF16) for s in state])
        for h, (kc, vc) in enumerate(heads):
            upd = _dot(p["k_st"][:, kc].T.astype(BF16), v_ref[rows, vc])
            decay_col = jnp.broadcast_to(p["decay"][:, kc], (B_DK, B_DK)).T
            state[h] = state[h] * jnp.concatenate([decay_col, decay_col], axis=1) + upd
    for h in range(B_HEADS):
        s_scr[h] = state[h]

    yb = []
    for c, (rows, p) in enumerate(zip(chunks, prep)):
        gb = gb_ref[rows, :].astype(F32)
        parts = []
        for h, (kc, vc) in enumerate(heads):
            o = _dot(jnp.concatenate([scores[c][h], p["q_int"][:, kc]], axis=1),
                     jnp.concatenate([v_ref[rows, vc], state_before[c][h]], axis=0))
            parts.append((_rms(o, ggla_ref[...]) * gb[:, vc]).astype(BF16))
        yb.append(jnp.concatenate(parts, axis=1))

    pa = _dot(jnp.concatenate(ya, axis=0), wpa_ref[...])
    _cast_blocks(cast_src, cast_dst)
    _sample_mixers(d_uv_ref.at[0], d_qk_ref.at[0], d_v_ref.at[0], d_gb_ref.at[0], d_la_ref.at[0],
                   d_state_ref, ws0_ref, bs0_ref, ggla_ref, d_ya_ref.at[0], d_yb_ref.at[0],
                   d_sout_ref, d_cv_ref.at[0], o_scr, n_dec)
    mix = (sga_ref[...].astype(F32) * pa
           + sgb_ref[...].astype(F32) * _dot(jnp.concatenate(yb, axis=0), wpb_ref[...]))
    x2_ref[...] = x1_ref[...] + _dot(mix.astype(BF16), wo_ref[...])

    @pl.when(t == pl.num_programs(1) - 1)
    def _():
        sout_ref[0] = s_scr[...]


def _sample_mixers(uv_ref, qk_ref, v_ref, gb_ref, la_ref, s_ref, ws0_ref, bs0_ref, ggla_ref,
                   ya_ref, yb_ref, sout_ref, cv_ref, o_scr, nb):
    vn = uv_ref[:, A_WIDTH:]
    cv_ref[...] = vn
    ya_ref[...] = uv_ref[:, :A_WIDTH] * (vn * ws0_ref[...] + bs0_ref[...])

    a = jnp.exp(la_ref[...])
    qk = qk_ref[...]
    qs = qk[:, :B_KW]
    k = qk[:, B_KW:]
    v = v_ref[...]
    pad = jnp.zeros((LANES - 3 * nb, B_DK), F32)
    for h in range(B_HEADS):
        kc = slice(h * B_DK, (h + 1) * B_DK)
        vc = slice(h * B_DV, (h + 1) * B_DV)
        xt = jnp.concatenate([a[:, kc], k[:, kc], qs[:, kc], pad], axis=0).T
        for n in range(nb):
            s_new = (s_ref[n, h] * xt[:, n:n + 1]
                     + xt[:, nb + n:nb + n + 1] * v[n:n + 1, vc])
            sout_ref[n, h] = s_new
            o_scr[n:n + 1, vc] = jnp.sum(xt[:, 2 * nb + n:2 * nb + n + 1] * s_new,
                                         axis=0, keepdims=True)
    gb = gb_ref[...]
    for h in range(B_HEADS):
        vc = slice(h * B_DV, (h + 1) * B_DV)
        yb_ref[:, vc] = _rms(o_scr[:, vc], ggla_ref[...]) * gb[:, vc]


def _merge_kernel(ya_ref, yb_ref, ga_ref, gb_ref, x1_ref, wpa_ref, wpb_ref, wo_ref, x2_ref):
    mix = (ga_ref[...].astype(F32) * _dot(ya_ref[...], wpa_ref[...])
           + gb_ref[...].astype(F32) * _dot(yb_ref[...], wpb_ref[...]))
    x2_ref[...] = x1_ref[...] + _dot(mix.astype(BF16), wo_ref[...])


def _ffn2_body(ins, weights, outs, *, final_norm):
    (x_ref, p_ref), (y_ref,) = ins, outs
    g2_ref, w2i_ref, w2o_ref, gple_ref, wpg_ref, wple_ref, gfin_ref = weights

    def finish(rows, x3):
        gate = jax.nn.sigmoid(_dot(_rms(x3, gple_ref[...]).astype(BF16), wpg_ref[...]))
        x4 = x3 + _dot(p_ref[rows, :].astype(BF16), wple_ref[...]) * gate
        y_ref[rows, :] = _rms(x4, gfin_ref[...]) if final_norm else x4

    _swiglu_stage(x_ref, g2_ref, w2i_ref, w2o_ref, finish)


def _resident(shape):
    zeros = (0,) * len(shape)
    return pl.BlockSpec(shape, lambda *_: zeros, pipeline_mode=pl.Buffered(1))


def _rows(tm, width, colblk=0):
    return pl.BlockSpec((tm, width), lambda i: (i, colblk))


def _params(n_axes):
    return pltpu.CompilerParams(dimension_semantics=("arbitrary",) * n_axes,
                                vmem_limit_bytes=VMEM_LIMIT)


def _cast_specs(weights, n_steps, linear_step=lambda i: i):
    specs, shapes = [], []
    for w in weights:
        rows, cols = w.shape
        n_blocks = max(n for n in range(1, n_steps + 1) if rows % (n * BF16_SUBLANES) == 0)
        specs.append(pl.BlockSpec(
            (rows // n_blocks, cols),
            lambda *idx, n=n_blocks: (jnp.minimum(linear_step(*idx), n - 1), 0)))
        shapes.append(jax.ShapeDtypeStruct(w.shape, BF16))
    return specs, shapes


def _two_group_call(name, body, prompt_in, sample_in, weights, prompt_out, sample_out, to_cast=()):
    m, ms = prompt_in[0].shape[0], sample_in[0].shape[0]
    assert m % TOKEN_TILE == 0
    n = m // TOKEN_TILE

    def p_spec(cols):
        return pl.BlockSpec((TOKEN_TILE, cols), lambda i: (jnp.minimum(i, n - 1), 0))

    def s_spec(cols):
        return pl.BlockSpec((ms, cols), lambda i: (0, 0))

    cast_specs, cast_shapes = _cast_specs(to_cast, n)
    outs = pl.pallas_call(
        functools.partial(_two_group_kernel, body=body, n_in=len(prompt_in), n_w=len(weights),
                          n_out=len(prompt_out), n_cast=len(to_cast)),
        grid=(n + 1,),
        in_specs=[p_spec(a.shape[1]) for a in prompt_in] + [s_spec(a.shape[1]) for a in sample_in]
                 + [_resident(w.shape) for w in weights] + cast_specs,
        out_specs=[p_spec(c) for c, _ in prompt_out] + [s_spec(c) for c, _ in sample_out]
                  + cast_specs,
        out_shape=[jax.ShapeDtypeStruct((m, c), dt) for c, dt in prompt_out]
                  + [jax.ShapeDtypeStruct((ms, c), dt) for c, dt in sample_out] + cast_shapes,
        compiler_params=_params(1),
        name=name,
    )(*prompt_in, *sample_in, *weights, *to_cast)
    k = len(prompt_out)
    return outs[:k], outs[k:2 * k], outs[2 * k:]


def _mixer(h, x1, n_seq, seq, d_p, d_la, d_state, proj_w, ws, bsp_t, ggla, wpa, wpb, wo, ws0, bs0,
           to_cast=()):
    tt = MIXER_TILE
    nt = seq // tt
    m = n_seq * seq
    steps = n_seq * nt
    n_all = d_p.shape[0]
    assert n_all % steps == 0
    n_dec = n_all // steps

    def seg(colblk, width=D_MODEL):
        return pl.BlockSpec((tt, width), lambda b, t: (b * nt + t, colblk))

    def d_seg(colblk, width=D_MODEL):
        return pl.BlockSpec((1, n_dec, width), lambda b, t: (b * nt + t, 0, colblk))

    d_state_spec = pl.BlockSpec((n_dec, B_HEADS, B_DK, B_DV), lambda b, t: (b * nt + t, 0, 0, 0))
    d_p3 = d_p.reshape(steps, n_dec, P_COLS)
    small = list(proj_w) + [ws, bsp_t, ggla, wpa, wpb, wo, ws0, bs0]
    cast_specs, cast_shapes = _cast_specs(to_cast, steps, lambda b, t: b * nt + t)
    x2, sp, ya, yb, ss, cv, *cast = pl.pallas_call(
        functools.partial(_mixer_kernel, n_chunks=tt // CHUNK, n_dec=n_dec, n_cast=len(to_cast)),
        grid=(n_seq, nt),
        in_specs=[seg(0), seg(0),
                  d_seg(0), d_seg(1), d_seg(2), d_seg(3), d_seg(0, B_KW), d_state_spec]
                 + [_resident(w.shape) for w in small] + cast_specs,
        out_specs=[seg(0),
                   pl.BlockSpec((1, B_HEADS, B_DK, B_DV), lambda b, t: (b, 0, 0, 0)),
                   d_seg(0, A_WIDTH), d_seg(0, B_VW), d_state_spec, d_seg(0, A_WIDTH)]
                  + cast_specs,
        out_shape=[jax.ShapeDtypeStruct((m, D_MODEL), F32),
                   jax.ShapeDtypeStruct((n_seq, B_HEADS, B_DK, B_DV), F32),
                   jax.ShapeDtypeStruct((steps, n_dec, A_WIDTH), F32),
                   jax.ShapeDtypeStruct((steps, n_dec, B_VW), F32),
                   jax.ShapeDtypeStruct(d_state.shape, F32),
                   jax.ShapeDtypeStruct((steps, n_dec, A_WIDTH), F32)] + cast_shapes,
        scratch_shapes=[pltpu.VMEM((B_HEADS, B_DK, B_DV), F32), pltpu.VMEM((n_dec, B_VW), F32),
                        pltpu.VMEM((tt, P_COLS), BF16), pltpu.VMEM((tt, B_KW), F32)],
        compiler_params=_params(2),
        name="mixer",
    )(h, x1, d_p3, d_p3, d_p3, d_p3, d_la.reshape(steps, n_dec, B_KW), d_state,
      *small, *to_cast)
    return (x2, sp, ya.reshape(n_all, A_WIDTH), yb.reshape(n_all, B_VW), ss,
            cv.reshape(n_all, A_WIDTH), cast)


def _sample_inproj(h, weights):
    m = h.shape[0]
    return pl.pallas_call(
        _sample_inproj_kernel,
        grid=(1,),
        in_specs=[_rows(m, D_MODEL)] + [_resident(w.shape) for w in weights],
        out_specs=[_rows(m, P_COLS), _rows(m, B_KW)],
        out_shape=[jax.ShapeDtypeStruct((m, P_COLS), F32), jax.ShapeDtypeStruct((m, B_KW), F32)],
        compiler_params=_params(1),
        name="sample_inproj",
    )(h, *weights)


def _merge(ya, yb, p, x1, wpa, wpb, wo):
    tm = m = x1.shape[0]
    return pl.pallas_call(
        _merge_kernel,
        grid=(m // tm,),
        in_specs=[_rows(tm, A_WIDTH), _rows(tm, B_VW), _rows(tm, D_MODEL, 4),
                  _rows(tm, D_MODEL, 5), _rows(tm, D_MODEL), _resident(wpa.shape),
                  _resident(wpb.shape), _resident(wo.shape)],
        out_specs=_rows(tm, D_MODEL),
        out_shape=jax.ShapeDtypeStruct((m, D_MODEL), F32),
        compiler_params=_params(1),
        name="merge",
    )(ya, yb, p, p, x1, wpa, wpb, wo)


def kernel(x_prompt, x_sample, p_prompt, p_sample, state_gla, g_ffn1, w_ffn1_in, w_ffn1_out, g_mix,
           w_in, ln_v_g, ln_v_b, w_spatial, b_spatial, w_gate_up, b_gate, g_gla_out, w_proj_a,
           w_proj_b, w_out, g_ffn2, w_ffn2_in, w_ffn2_out, g_ple, w_ple_gate, w_ple, g_final):
    depth = w_in.shape[0]
    n_seq, seq, _ = x_prompt.shape
    n_dec, dec_seq, _ = x_sample.shape
    assert dec_seq == 1 and seq % MIXER_TILE == 0

    def row(vec):
        return vec.reshape(1, -1).astype(F32)

    assert sum(IN_SIZES[:6]) == LR_LO and sum(IN_SIZES) == GATE_LO + 2 * D_MODEL
    xp = x_prompt.reshape(n_seq * seq, D_MODEL)
    xs = x_sample.reshape(n_dec, D_MODEL)
    gfin = row(g_final)
    sp_list, ss_list, vs_list = [], [], []
    for i in range(depth):
        w1i, w1o = w_ffn1_in[i].astype(BF16), w_ffn1_out[i].astype(BF16)
        wgu = jnp.pad(w_gate_up[i], ((0, LANES - GATE_RANK), (0, 0))).astype(BF16)
        g1, gmix, g2, gple = row(g_ffn1[i]), row(g_mix[i]), row(g_ffn2[i]), row(g_ple[i])
        lng, lnb, bg, ggla = row(ln_v_g[i]), row(ln_v_b[i]), row(b_gate[i]), row(g_gla_out[i])
        ws0 = row(jnp.repeat(w_spatial[i][:, 0, 0], A_GROUP_DIM))
        bs0 = row(jnp.repeat(b_spatial[i][:, 0], A_GROUP_DIM))

        wide, act = (D_MODEL, F32), (D_MODEL, BF16)
        (x1p, hp), (x1s, hs), (w_inb, wpa, wpb, wo) = _two_group_call(
            "ffn1", _ffn1_body, [xp], [xs], [g1, w1i, w1o, gmix], [wide, act], [wide, act],
            to_cast=[w_in[i].T, w_proj_a[i], w_proj_b[i], w_out[i]])
        proj_w = [w_inb, wgu, bg, lng, lnb]
        ps, las = _sample_inproj(hs, proj_w)
        x2p, sp, ya, yb, ss, cv, (w2i, w2o, wpg, wple) = _mixer(
            hp, x1p, n_seq, seq, ps, las, state_gla[i], proj_w, w_spatial[i], b_spatial[i].T, ggla,
            wpa, wpb, wo, ws0, bs0,
            to_cast=[w_ffn2_in[i], w_ffn2_out[i], w_ple_gate[i], w_ple[i]])
        x2s = _merge(ya.astype(BF16), yb.astype(BF16), ps, x1s, wpa, wpb, wo)

        ple_p = p_prompt[i].reshape(n_seq * seq, PLE_DIM)
        ple_s = p_sample[i].reshape(n_dec, PLE_DIM)
        (xp,), (xs,), _ = _two_group_call(
            "ffn2", functools.partial(_ffn2_body, final_norm=i == depth - 1), [x2p, ple_p],
            [x2s, ple_s], [g2, w2i, w2o, gple, wpg, wple, gfin], [wide], [wide])

        sp_list.append(sp)
        ss_list.append(ss)
        vs_list.append(cv.reshape(n_dec, dec_seq, A_WIDTH))
    return (xp.reshape(n_seq, seq, D_MODEL), xs.reshape(n_dec, dec_seq, D_MODEL),
            jnp.stack(sp_list), jnp.stack(ss_list), jnp.stack(vs_list))
```

```python
import functools

import jax
import jax.numpy as jnp
from jax import lax
from jax.experimental import pallas as pl
from jax.experimental.pallas import tpu as pltpu

D_MODEL = 1024
D_FF = 2816
PLE_DIM = 256
CHUNK = 128
A_GROUPS = 4
A_GROUP_DIM = 128
A_WIDTH = A_GROUPS * A_GROUP_DIM
B_HEADS = 4
B_DK = 128
B_DV = 256
B_KW = B_HEADS * B_DK
B_VW = B_HEADS * B_DV
GATE_RANK = 16
GATE_NORM = 16.0
EPS = 1e-6
IN_SIZES = (A_WIDTH, A_WIDTH, B_KW, B_KW, B_VW, B_VW, GATE_RANK, D_MODEL, D_MODEL)

LANES = 128
P_COLS = 6 * D_MODEL
QK_LO = 2 * A_WIDTH
VB_LO = QK_LO + 2 * B_KW
GB_LO = VB_LO + B_VW
LR_LO = GB_LO + B_VW
GATE_LO = LR_LO + GATE_RANK
FF_CHUNK = 256
HALF = CHUNK // 2
MIXER_TILE = 4 * CHUNK
TOKEN_TILE = 512
SUB_TILE = 512
BF16_SUBLANES = 16
VMEM_LIMIT = 56 * 1024 * 1024

F32 = jnp.float32
BF16 = jnp.bfloat16


def _dot(a, b):
    return jnp.dot(a, b, preferred_element_type=F32)


def _dot_nt(a, b):
    return lax.dot_general(a, b, (((1,), (1,)), ((), ())), preferred_element_type=F32)


def _rms(x, g):
    return x * lax.rsqrt(jnp.mean(x * x, axis=-1, keepdims=True) + EPS) * g


def _log_sigmoid(x):
    return jnp.minimum(x, 0.0) - jnp.log1p(jnp.exp(-jnp.abs(x)))


def _layernorm(x, g, b):
    mu = jnp.mean(x, axis=-1, keepdims=True)
    xc = x - mu
    var = jnp.mean(xc * xc, axis=-1, keepdims=True)
    return xc * lax.rsqrt(var + EPS) * g + b


def _swiglu_stage(x_ref, g_ref, w_in_ref, w_out_ref, finish):
    n_rows = x_ref.shape[0]
    pending = None
    for r0 in range(0, n_rows, SUB_TILE):
        rows = slice(r0, min(r0 + SUB_TILE, n_rows))
        x = x_ref[rows, :]
        h = _rms(x, g_ref[...]).astype(BF16)
        acc = None
        for ci, lo in enumerate(range(0, D_FF, FF_CHUNK)):
            hi = min(lo + FF_CHUNK, D_FF)
            gate = _dot(h, w_in_ref[:, lo:hi])
            up = _dot(h, w_in_ref[:, D_FF + lo:D_FF + hi])
            act = (jax.nn.silu(gate) * up).astype(BF16)
            part = _dot(act, w_out_ref[lo:hi, :])
            acc = part if acc is None else acc + part
            if ci == 1 and pending is not None:
                pending()
                pending = None
        pending = functools.partial(finish, rows, x + 0.5 * acc)
    pending()


def _cast_blocks(src_refs, dst_refs):
    for src, dst in zip(src_refs, dst_refs):
        dst[...] = src[...].astype(dst.dtype)


def _two_group_kernel(*refs, body, n_in, n_w, n_out, n_cast):
    it = iter(refs)

    def take(k):
        return [next(it) for _ in range(k)]

    p_in, s_in, weights, cast_src = take(n_in), take(n_in), take(n_w), take(n_cast)
    p_out, s_out, cast_dst = take(n_out), take(n_out), take(n_cast)
    _cast_blocks(cast_src, cast_dst)
    step = pl.program_id(0)
    sample_step = pl.num_programs(0) - 1

    @pl.when(step < sample_step)
    def _():
        body(p_in, weights, p_out)

    @pl.when(step == sample_step)
    def _():
        body(s_in, weights, s_out)


def _ffn1_body(ins, weights, outs):
    (x_ref,), (g1_ref, w1i_ref, w1o_ref, gmix_ref), (x1_ref, h_ref) = ins, weights, outs

    def finish(rows, x1):
        x1_ref[rows, :] = x1
        h_ref[rows, :] = _rms(x1, gmix_ref[...]).astype(BF16)

    _swiglu_stage(x_ref, g1_ref, w1i_ref, w1o_ref, finish)


def _activated_projection(h, weights, put, put_la):
    wt_ref, wgu_ref, bg_ref, lng_ref, lnb_ref = weights

    def proj(lo, width):
        return _dot_nt(h, wt_ref[lo:lo + width, :])

    lr = proj(LR_LO, LANES).astype(BF16)
    logit = _dot(lr, wgu_ref[...]) + bg_ref[...]
    put_la(_log_sigmoid(logit) * (1.0 / GATE_NORM))
    half = D_MODEL // 2
    segments = [(0, 0, jax.nn.gelu),
                (A_WIDTH, A_WIDTH,
                 lambda z: _layernorm(jax.nn.gelu(z), lng_ref[...], lnb_ref[...])),
                (QK_LO, QK_LO, lambda z: z * (B_DK ** -0.5)),
                (QK_LO + B_KW, QK_LO + B_KW, lambda z: z)]
    segments += [(VB_LO + j * half, VB_LO + j * half, lambda z: z) for j in range(2)]
    segments += [(GATE_LO + j * half, LR_LO + j * half, jax.nn.sigmoid) for j in range(4)]
    segments += [(GB_LO + j * half, GB_LO + j * half, jax.nn.silu) for j in range(2)]
    for w_lo, p_lo, act in segments:
        put(p_lo, act(proj(w_lo, half)))


def _gla_prep(la, qk, ones_tril):
    la_hi = la.astype(BF16)
    la_lo = (la - la_hi.astype(F32)).astype(BF16)
    b = _dot(ones_tril, la_hi) + _dot(ones_tril, la_lo)
    r0 = b[HALF // 2 - 1:HALF // 2, :]
    r1 = b[HALF + HALF // 2 - 1:HALF + HALF // 2, :]
    b_mid = b[HALF - 1:HALF, :]
    b_end = b[CHUNK - 1:CHUNK, :]
    bq0 = b[:HALF] - r0
    bq1 = b[HALF:] - r1
    qs = qk[:, :B_KW]
    k = qk[:, B_KW:]
    a0 = qs[:HALF] * jnp.exp(bq0)
    a1 = qs[HALF:] * jnp.exp(bq1)
    k0 = k[:HALF] * jnp.exp(-bq0)
    k1 = k[HALF:] * jnp.exp(-bq1)
    q_off1 = a1 * jnp.exp(r1 - b_mid)
    k_off0 = k0 * jnp.exp(b_mid - r0)
    q_int = jnp.concatenate([a0 * jnp.exp(r0), q_off1 * jnp.exp(b_mid)], axis=0)
    k_st = jnp.concatenate([k_off0 * jnp.exp(b_end - b_mid), k1 * jnp.exp(b_end - r1)], axis=0)
    return dict(a0=a0.astype(BF16), a1=a1.astype(BF16), k0=k0.astype(BF16), k1=k1.astype(BF16),
                q_off1=q_off1.astype(BF16), k_off0=k_off0.astype(BF16), q_int=q_int.astype(BF16),
                k_st=k_st, decay=jnp.exp(b_end))


def _mixer_kernel(*refs, n_chunks, n_dec, n_cast):
    it = iter(refs)

    def take(k):
        return [next(it) for _ in range(k)]

    h_ref, x1_ref, d_h_ref, d_state_ref = take(4)
    proj_w = take(5)
    ws_ref, bsp_ref, ggla_ref, wpa_ref, wpb_ref, wo_ref, ws0_ref, bs0_ref = take(8)
    cast_src = take(n_cast)
    x2_ref, sout_ref, d_ya_ref, d_yb_ref, d_sout_ref, d_cv_ref, d_gate_ref = take(7)
    cast_dst = take(n_cast)
    s_scr, o_scr, p_scr, la_ref, d_p_scr, d_la_ref = take(6)
    t = pl.program_id(1)
    tt = h_ref.shape[0]

    @pl.when(t == 0)
    def _():
        s_scr[...] = jnp.zeros_like(s_scr)

    def put(lo, val):
        p_scr[:, lo:lo + val.shape[1]] = val[:tt].astype(BF16)
        d_p_scr[:, lo:lo + val.shape[1]] = val[tt:tt + n_dec]

    def put_la(val):
        la_ref[...] = val[:tt]
        d_la_ref[...] = val[tt:tt + n_dec]

    _activated_projection(jnp.concatenate([h_ref[...], d_h_ref[0]], axis=0), proj_w, put, put_la)
    uv_ref, qk_ref, v_ref, gb_ref, sga_ref, sgb_ref = (
        p_scr.at[:, j * D_MODEL:(j + 1) * D_MODEL] for j in range(6))
    d_uv_ref, d_qk_ref, d_v_ref, d_gb_ref = (
        d_p_scr.at[:, j * D_MODEL:(j + 1) * D_MODEL] for j in range(4))

    row = lax.broadcasted_iota(jnp.int32, (CHUNK, CHUNK), 0)
    col = lax.broadcasted_iota(jnp.int32, (CHUNK, CHUNK), 1)
    causal = row >= col
    ones_tril = causal.astype(BF16)
    ws_tril = [jnp.where(causal, ws_ref[g], 0.0).astype(BF16) for g in range(A_GROUPS)]
    zero_blk = jnp.zeros((HALF, B_DK), BF16)
    chunks = [slice(c * CHUNK, (c + 1) * CHUNK) for c in range(n_chunks)]
    heads = [(slice(h * B_DK, (h + 1) * B_DK), slice(h * B_DV, (h + 1) * B_DV))
             for h in range(B_HEADS)]

    ya = []
    for rows in chunks:
        u = uv_ref[rows, :A_WIDTH].astype(F32)
        vn = uv_ref[rows, A_WIDTH:]
        parts = []
        for g in range(A_GROUPS):
            cols = slice(g * A_GROUP_DIM, (g + 1) * A_GROUP_DIM)
            mixed = _dot(ws_tril[g], vn[:, cols]) + bsp_ref[:, g:g + 1]
            parts.append((u[:, cols] * mixed).astype(BF16))
        ya.append(jnp.concatenate(parts, axis=1))

    prep = [_gla_prep(la_ref[rows, :], qk_ref[rows, :].astype(F32), ones_tril) for rows in chunks]
    scores = []
    for p in prep:
        per_head = []
        for kc, _ in heads:
            q3 = jnp.concatenate([
                jnp.concatenate([p["a0"][:, kc], zero_blk, zero_blk], axis=1),
                jnp.concatenate([zero_blk, p["q_off1"][:, kc], p["a1"][:, kc]], axis=1)], axis=0)
            k3 = jnp.concatenate([
                jnp.concatenate([p["k0"][:, kc], p["k_off0"][:, kc], zero_blk], axis=1),
                jnp.concatenate([zero_blk, zero_blk, p["k1"][:, kc]], axis=1)], axis=0)
            per_head.append(jnp.where(causal, _dot_nt(q3, k3), 0.0).astype(BF16))
        scores.append(per_head)

    state = [s_scr[h] for h in range(B_HEADS)]
    state_before = []
    for rows, p in zip(chunks, prep):
        state_before.append([s.astype(BF16) for s in state])
        for h, (kc, vc) in enumerate(heads):
            upd = _dot(p["k_st"][:, kc].T.astype(BF16), v_ref[rows, vc])
            decay_col = jnp.broadcast_to(p["decay"][:, kc], (B_DK, B_DK)).T
            state[h] = state[h] * jnp.concatenate([decay_col, decay_col], axis=1) + upd
    for h in range(B_HEADS):
        s_scr[h] = state[h]

    yb = []
    for c, (rows, p) in enumerate(zip(chunks, prep)):
        gb = gb_ref[rows, :].astype(F32)
        parts = []
        for h, (kc, vc) in enumerate(heads):
            o = _dot(jnp.concatenate([scores[c][h], p["q_int"][:, kc]], axis=1),
                     jnp.concatenate([v_ref[rows, vc], state_before[c][h]], axis=0))
            parts.append((_rms(o, ggla_ref[...]) * gb[:, vc]).astype(BF16))
        yb.append(jnp.concatenate(parts, axis=1))

    pa = _dot(jnp.concatenate(ya, axis=0), wpa_ref[...])
    _cast_blocks(cast_src, cast_dst)
    _sample_mixers(d_uv_ref, d_qk_ref, d_v_ref, d_gb_ref, d_la_ref, d_state_ref, ws0_ref, bs0_ref,
                   ggla_ref, d_ya_ref.at[0], d_yb_ref.at[0], d_sout_ref, d_cv_ref.at[0], o_scr,
                   n_dec)
    d_gate_ref[0] = d_p_scr[:, LR_LO:]
    mix = (sga_ref[...].astype(F32) * pa
           + sgb_ref[...].astype(F32) * _dot(jnp.concatenate(yb, axis=0), wpb_ref[...]))
    x2_ref[...] = x1_ref[...] + _dot(mix.astype(BF16), wo_ref[...])

    @pl.when(t == pl.num_programs(1) - 1)
    def _():
        sout_ref[0] = s_scr[...]


def _sample_mixers(uv_ref, qk_ref, v_ref, gb_ref, la_ref, s_ref, ws0_ref, bs0_ref, ggla_ref,
                   ya_ref, yb_ref, sout_ref, cv_ref, o_scr, nb):
    vn = uv_ref[:, A_WIDTH:]
    cv_ref[...] = vn
    ya_ref[...] = uv_ref[:, :A_WIDTH] * (vn * ws0_ref[...] + bs0_ref[...])

    a = jnp.exp(la_ref[...])
    qk = qk_ref[...]
    qs = qk[:, :B_KW]
    k = qk[:, B_KW:]
    v = v_ref[...]
    pad = jnp.zeros((LANES - 3 * nb, B_DK), F32)
    for h in range(B_HEADS):
        kc = slice(h * B_DK, (h + 1) * B_DK)
        vc = slice(h * B_DV, (h + 1) * B_DV)
        xt = jnp.concatenate([a[:, kc], k[:, kc], qs[:, kc], pad], axis=0).T
        for n in range(nb):
            s_new = (s_ref[n, h] * xt[:, n:n + 1]
                     + xt[:, nb + n:nb + n + 1] * v[n:n + 1, vc])
            sout_ref[n, h] = s_new
            o_scr[n:n + 1, vc] = jnp.sum(xt[:, 2 * nb + n:2 * nb + n + 1] * s_new,
                                         axis=0, keepdims=True)
    gb = gb_ref[...]
    for h in range(B_HEADS):
        vc = slice(h * B_DV, (h + 1) * B_DV)
        yb_ref[:, vc] = _rms(o_scr[:, vc], ggla_ref[...]) * gb[:, vc]


def _merge_kernel(ya_ref, yb_ref, ga_ref, gb_ref, x1_ref, wpa_ref, wpb_ref, wo_ref, x2_ref):
    mix = (ga_ref[...].astype(F32) * _dot(ya_ref[...], wpa_ref[...])
           + gb_ref[...].astype(F32) * _dot(yb_ref[...], wpb_ref[...]))
    x2_ref[...] = x1_ref[...] + _dot(mix.astype(BF16), wo_ref[...])


def _ffn2_body(ins, weights, outs, *, final_norm):
    (x_ref, p_ref), (y_ref,) = ins, outs
    g2_ref, w2i_ref, w2o_ref, gple_ref, wpg_ref, wple_ref, gfin_ref = weights

    def finish(rows, x3):
        gate = jax.nn.sigmoid(_dot(_rms(x3, gple_ref[...]).astype(BF16), wpg_ref[...]))
        x4 = x3 + _dot(p_ref[rows, :].astype(BF16), wple_ref[...]) * gate
        y_ref[rows, :] = _rms(x4, gfin_ref[...]) if final_norm else x4

    _swiglu_stage(x_ref, g2_ref, w2i_ref, w2o_ref, finish)


def _resident(shape):
    zeros = (0,) * len(shape)
    return pl.BlockSpec(shape, lambda *_: zeros, pipeline_mode=pl.Buffered(1))


def _rows(tm, width, colblk=0):
    return pl.BlockSpec((tm, width), lambda i: (i, colblk))


def _params(n_axes):
    return pltpu.CompilerParams(dimension_semantics=("arbitrary",) * n_axes,
                                vmem_limit_bytes=VMEM_LIMIT)


def _cast_specs(weights, n_steps, linear_step=lambda i: i):
    specs, shapes = [], []
    for w in weights:
        rows, cols = w.shape
        n_blocks = max(n for n in range(1, n_steps + 1) if rows % (n * BF16_SUBLANES) == 0)
        specs.append(pl.BlockSpec(
            (rows // n_blocks, cols),
            lambda *idx, n=n_blocks: (jnp.minimum(linear_step(*idx), n - 1), 0)))
        shapes.append(jax.ShapeDtypeStruct(w.shape, BF16))
    return specs, shapes


def _two_group_call(name, body, prompt_in, sample_in, weights, prompt_out, sample_out, to_cast=()):
    m, ms = prompt_in[0].shape[0], sample_in[0].shape[0]
    assert m % TOKEN_TILE == 0
    n = m // TOKEN_TILE

    def p_spec(cols):
        return pl.BlockSpec((TOKEN_TILE, cols), lambda i: (jnp.minimum(i, n - 1), 0))

    def s_spec(cols):
        return pl.BlockSpec((ms, cols), lambda i: (0, 0))

    cast_specs, cast_shapes = _cast_specs(to_cast, n)
    outs = pl.pallas_call(
        functools.partial(_two_group_kernel, body=body, n_in=len(prompt_in), n_w=len(weights),
                          n_out=len(prompt_out), n_cast=len(to_cast)),
        grid=(n + 1,),
        in_specs=[p_spec(a.shape[1]) for a in prompt_in] + [s_spec(a.shape[1]) for a in sample_in]
                 + [_resident(w.shape) for w in weights] + cast_specs,
        out_specs=[p_spec(c) for c, _ in prompt_out] + [s_spec(c) for c, _ in sample_out]
                  + cast_specs,
        out_shape=[jax.ShapeDtypeStruct((m, c), dt) for c, dt in prompt_out]
                  + [jax.ShapeDtypeStruct((ms, c), dt) for c, dt in sample_out] + cast_shapes,
        compiler_params=_params(1),
        name=name,
    )(*prompt_in, *sample_in, *weights, *to_cast)
    k = len(prompt_out)
    return outs[:k], outs[k:2 * k], outs[2 * k:]


def _mixer(h, x1, n_seq, seq, d_h, d_state, proj_w, ws, bsp_t, ggla, wpa, wpb, wo, ws0, bs0,
           to_cast=()):
    tt = MIXER_TILE
    nt = seq // tt
    m = n_seq * seq
    steps = n_seq * nt
    n_all = d_h.shape[0]
    assert n_all % steps == 0
    n_dec = n_all // steps
    assert n_dec <= BF16_SUBLANES
    d_h3 = jnp.pad(d_h.reshape(steps, n_dec, D_MODEL), ((0, 0), (0, BF16_SUBLANES - n_dec), (0, 0)))

    def seg(colblk, width=D_MODEL):
        return pl.BlockSpec((tt, width), lambda b, t: (b * nt + t, colblk))

    def d_seg(colblk, width=D_MODEL):
        return pl.BlockSpec((1, n_dec, width), lambda b, t: (b * nt + t, 0, colblk))

    d_state_spec = pl.BlockSpec((n_dec, B_HEADS, B_DK, B_DV), lambda b, t: (b * nt + t, 0, 0, 0))
    d_h_spec = pl.BlockSpec((1, BF16_SUBLANES, D_MODEL), lambda b, t: (b * nt + t, 0, 0))
    small = list(proj_w) + [ws, bsp_t, ggla, wpa, wpb, wo, ws0, bs0]
    cast_specs, cast_shapes = _cast_specs(to_cast, steps, lambda b, t: b * nt + t)
    x2, sp, ya, yb, ss, cv, gates, *cast = pl.pallas_call(
        functools.partial(_mixer_kernel, n_chunks=tt // CHUNK, n_dec=n_dec, n_cast=len(to_cast)),
        grid=(n_seq, nt),
        in_specs=[seg(0), seg(0), d_h_spec, d_state_spec]
                 + [_resident(w.shape) for w in small] + cast_specs,
        out_specs=[seg(0),
                   pl.BlockSpec((1, B_HEADS, B_DK, B_DV), lambda b, t: (b, 0, 0, 0)),
                   d_seg(0, A_WIDTH), d_seg(0, B_VW), d_state_spec, d_seg(0, A_WIDTH),
                   d_seg(0, 2 * D_MODEL)] + cast_specs,
        out_shape=[jax.ShapeDtypeStruct((m, D_MODEL), F32),
                   jax.ShapeDtypeStruct((n_seq, B_HEADS, B_DK, B_DV), F32),
                   jax.ShapeDtypeStruct((steps, n_dec, A_WIDTH), F32),
                   jax.ShapeDtypeStruct((steps, n_dec, B_VW), F32),
                   jax.ShapeDtypeStruct(d_state.shape, F32),
                   jax.ShapeDtypeStruct((steps, n_dec, A_WIDTH), F32),
                   jax.ShapeDtypeStruct((steps, n_dec, 2 * D_MODEL), F32)] + cast_shapes,
        scratch_shapes=[pltpu.VMEM((B_HEADS, B_DK, B_DV), F32), pltpu.VMEM((n_dec, B_VW), F32),
                        pltpu.VMEM((tt, P_COLS), BF16), pltpu.VMEM((tt, B_KW), F32),
                        pltpu.VMEM((n_dec, P_COLS), F32), pltpu.VMEM((n_dec, B_KW), F32)],
        compiler_params=_params(2),
        name="mixer",
    )(h, x1, d_h3, d_state, *small, *to_cast)
    return (x2, sp, ya.reshape(n_all, A_WIDTH), yb.reshape(n_all, B_VW), ss,
            cv.reshape(n_all, A_WIDTH), gates.reshape(n_all, 2 * D_MODEL), cast)


def _merge(ya, yb, gates, x1, wpa, wpb, wo):
    tm = m = x1.shape[0]
    return pl.pallas_call(
        _merge_kernel,
        grid=(m // tm,),
        in_specs=[_rows(tm, A_WIDTH), _rows(tm, B_VW), _rows(tm, D_MODEL, 0),
                  _rows(tm, D_MODEL, 1), _rows(tm, D_MODEL), _resident(wpa.shape),
                  _resident(wpb.shape), _resident(wo.shape)],
        out_specs=_rows(tm, D_MODEL),
        out_shape=jax.ShapeDtypeStruct((m, D_MODEL), F32),
        compiler_params=_params(1),
        name="merge",
    )(ya, yb, gates, gates, x1, wpa, wpb, wo)


def kernel(x_prompt, x_sample, p_prompt, p_sample, state_gla, g_ffn1, w_ffn1_in, w_ffn1_out, g_mix,
           w_in, ln_v_g, ln_v_b, w_spatial, b_spatial, w_gate_up, b_gate, g_gla_out, w_proj_a,
           w_proj_b, w_out, g_ffn2, w_ffn2_in, w_ffn2_out, g_ple, w_ple_gate, w_ple, g_final):
    depth = w_in.shape[0]
    n_seq, seq, _ = x_prompt.shape
    n_dec, dec_seq, _ = x_sample.shape
    assert dec_seq == 1 and seq % MIXER_TILE == 0

    def row(vec):
        return vec.reshape(1, -1).astype(F32)

    assert sum(IN_SIZES[:6]) == LR_LO and sum(IN_SIZES) == GATE_LO + 2 * D_MODEL
    xp = x_prompt.reshape(n_seq * seq, D_MODEL)
    xs = x_sample.reshape(n_dec, D_MODEL)
    gfin = row(g_final)
    sp_list, ss_list, vs_list = [], [], []
    for i in range(depth):
        w1i, w1o = w_ffn1_in[i].astype(BF16), w_ffn1_out[i].astype(BF16)
        wgu = jnp.pad(w_gate_up[i], ((0, LANES - GATE_RANK), (0, 0))).astype(BF16)
        g1, gmix, g2, gple = row(g_ffn1[i]), row(g_mix[i]), row(g_ffn2[i]), row(g_ple[i])
        lng, lnb, bg, ggla = row(ln_v_g[i]), row(ln_v_b[i]), row(b_gate[i]), row(g_gla_out[i])
        ws0 = row(jnp.repeat(w_spatial[i][:, 0, 0], A_GROUP_DIM))
        bs0 = row(jnp.repeat(b_spatial[i][:, 0], A_GROUP_DIM))

        wide, act = (D_MODEL, F32), (D_MODEL, BF16)
        (x1p, hp), (x1s, hs), (w_inb, wpa, wpb, wo) = _two_group_call(
            "ffn1", _ffn1_body, [xp], [xs], [g1, w1i, w1o, gmix], [wide, act], [wide, act],
            to_cast=[w_in[i].T, w_proj_a[i], w_proj_b[i], w_out[i]])
        x2p, sp, ya, yb, ss, cv, gates, (w2i, w2o, wpg, wple) = _mixer(
            hp, x1p, n_seq, seq, hs, state_gla[i], [w_inb, wgu, bg, lng, lnb], w_spatial[i],
            b_spatial[i].T, ggla, wpa, wpb, wo, ws0, bs0,
            to_cast=[w_ffn2_in[i], w_ffn2_out[i], w_ple_gate[i], w_ple[i]])
        x2s = _merge(ya.astype(BF16), yb.astype(BF16), gates, x1s, wpa, wpb, wo)

        ple_p = p_prompt[i].reshape(n_seq * seq, PLE_DIM)
        ple_s = p_sample[i].reshape(n_dec, PLE_DIM)
        (xp,), (xs,), _ = _two_group_call(
            "ffn2", functools.partial(_ffn2_body, final_norm=i == depth - 1), [x2p, ple_p],
            [x2s, ple_s], [g2, w2i, w2o, gple, wpg, wple, gfin], [wide], [wide])

        sp_list.append(sp)
        ss_list.append(ss)
        vs_list.append(cv.reshape(n_dec, dec_seq, A_WIDTH))
    return (xp.reshape(n_seq, seq, D_MODEL), xs.reshape(n_dec, dec_seq, D_MODEL),
            jnp.stack(sp_list), jnp.stack(ss_list), jnp.stack(vs_list))
```

```python
import functools

import jax
import jax.numpy as jnp
from jax import lax
from jax.experimental import pallas as pl
from jax.experimental.pallas import tpu as pltpu

D_MODEL = 1024
D_FF = 2816
PLE_DIM = 256
CHUNK = 128
A_GROUPS = 4
A_GROUP_DIM = 128
A_WIDTH = A_GROUPS * A_GROUP_DIM
B_HEADS = 4
B_DK = 128
B_DV = 256
B_KW = B_HEADS * B_DK
B_VW = B_HEADS * B_DV
GATE_RANK = 16
GATE_NORM = 16.0
EPS = 1e-6
IN_SIZES = (A_WIDTH, A_WIDTH, B_KW, B_KW, B_VW, B_VW, GATE_RANK, D_MODEL, D_MODEL)

LANES = 128
P_COLS = 6 * D_MODEL
QK_LO = 2 * A_WIDTH
VB_LO = QK_LO + 2 * B_KW
GB_LO = VB_LO + B_VW
LR_LO = GB_LO + B_VW
GATE_LO = LR_LO + GATE_RANK
FF_CHUNK = 256
HALF = CHUNK // 2
MIXER_TILE = 4 * CHUNK
TOKEN_TILE = 512
BF16_SUBLANES = 16
VMEM_LIMIT = 56 * 1024 * 1024

F32 = jnp.float32
BF16 = jnp.bfloat16


def _dot(a, b):
    return jnp.dot(a, b, preferred_element_type=F32)


def _dot_nt(a, b):
    return lax.dot_general(a, b, (((1,), (1,)), ((), ())), preferred_element_type=F32)


def _rms(x, g):
    return x * lax.rsqrt(jnp.mean(x * x, axis=-1, keepdims=True) + EPS) * g


def _log_sigmoid(x):
    return jnp.minimum(x, 0.0) - jnp.log1p(jnp.exp(-jnp.abs(x)))


def _layernorm(x, g, b):
    mu = jnp.mean(x, axis=-1, keepdims=True)
    xc = x - mu
    var = jnp.mean(xc * xc, axis=-1, keepdims=True)
    return xc * lax.rsqrt(var + EPS) * g + b


def _swiglu_residual(x, g_ref, w_in_ref, w_out_ref, early):
    h = _rms(x, g_ref[...]).astype(BF16)
    acc = None
    for ci, lo in enumerate(range(0, D_FF, FF_CHUNK)):
        gate = _dot(h, w_in_ref[:, lo:lo + FF_CHUNK])
        up = _dot(h, w_in_ref[:, D_FF + lo:D_FF + lo + FF_CHUNK])
        act = (jax.nn.silu(gate) * up).astype(BF16)
        part = _dot(act, w_out_ref[lo:lo + FF_CHUNK, :])
        acc = part if acc is None else acc + part
        if ci == 1:
            early()
    return x + 0.5 * acc


def _cast_blocks(src_refs, dst_refs):
    for src, dst in zip(src_refs, dst_refs):
        dst[...] = src[...].astype(dst.dtype)


def _ffn_stage_kernel(*refs, finish, n_side, n_w, n_out, n_cast):
    it = iter(refs)

    def take(k):
        return [next(it) for _ in range(k)]

    (x_ref,), p_side, (xs_ref,), s_side = take(1), take(n_side), take(1), take(n_side)
    weights, cast_src = take(n_w), take(n_cast)
    p_out, s_out, cast_dst, (carry,) = take(n_out), take(n_out), take(n_cast), take(1)
    _cast_blocks(cast_src, cast_dst)
    step = pl.program_id(0)
    sample_step = pl.num_programs(0) - 1

    def finish_previous_tile():
        finish(carry[...], p_side, weights, p_out)

    @pl.when(step == 0)
    def _():
        carry[...] = jnp.zeros_like(carry)

    @pl.when(step < sample_step)
    def _():
        carry[...] = _swiglu_residual(x_ref[...], *weights[:3], finish_previous_tile)

    @pl.when(step == sample_step)
    def _():
        x_new = _swiglu_residual(xs_ref[...], *weights[:3], finish_previous_tile)
        finish(x_new, s_side, weights, s_out)


def _ffn1_finish(x1, side, weights, outs):
    gmix_ref, (x1_ref, h_ref) = weights[3], outs
    x1_ref[...] = x1
    h_ref[...] = _rms(x1, gmix_ref[...]).astype(BF16)


def _activated_projection(h, weights, put, put_la):
    wt_ref, wgu_ref, bg_ref, lng_ref, lnb_ref = weights

    def proj(lo, width):
        return _dot_nt(h, wt_ref[lo:lo + width, :])

    lr = proj(LR_LO, LANES).astype(BF16)
    logit = _dot(lr, wgu_ref[...]) + bg_ref[...]
    put_la(_log_sigmoid(logit) * (1.0 / GATE_NORM))
    half = D_MODEL // 2
    segments = [(0, 0, jax.nn.gelu),
                (A_WIDTH, A_WIDTH,
                 lambda z: _layernorm(jax.nn.gelu(z), lng_ref[...], lnb_ref[...])),
                (QK_LO, QK_LO, lambda z: z * (B_DK ** -0.5)),
                (QK_LO + B_KW, QK_LO + B_KW, lambda z: z)]
    segments += [(VB_LO + j * half, VB_LO + j * half, lambda z: z) for j in range(2)]
    segments += [(GATE_LO + j * half, LR_LO + j * half, jax.nn.sigmoid) for j in range(4)]
    segments += [(GB_LO + j * half, GB_LO + j * half, jax.nn.silu) for j in range(2)]
    for w_lo, p_lo, act in segments:
        put(p_lo, act(proj(w_lo, half)))


def _gla_prep(la, qk, ones_tril):
    la_hi = la.astype(BF16)
    la_lo = (la - la_hi.astype(F32)).astype(BF16)
    b = _dot(jnp.concatenate([ones_tril, ones_tril], axis=1),
             jnp.concatenate([la_hi, la_lo], axis=0))
    r0 = b[HALF // 2 - 1:HALF // 2, :]
    r1 = b[HALF + HALF // 2 - 1:HALF + HALF // 2, :]
    b_mid = b[HALF - 1:HALF, :]
    b_end = b[CHUNK - 1:CHUNK, :]
    bq0 = b[:HALF] - r0
    bq1 = b[HALF:] - r1
    qs = qk[:, :B_KW]
    k = qk[:, B_KW:]
    a0 = qs[:HALF] * jnp.exp(bq0)
    a1 = qs[HALF:] * jnp.exp(bq1)
    k0 = k[:HALF] * jnp.exp(-bq0)
    k1 = k[HALF:] * jnp.exp(-bq1)
    q_off1 = a1 * jnp.exp(r1 - b_mid)
    k_off0 = k0 * jnp.exp(b_mid - r0)
    q_int = jnp.concatenate([a0 * jnp.exp(r0), q_off1 * jnp.exp(b_mid)], axis=0)
    k_st = jnp.concatenate([k_off0 * jnp.exp(b_end - b_mid), k1 * jnp.exp(b_end - r1)], axis=0)
    return dict(a0=a0.astype(BF16), a1=a1.astype(BF16), k0=k0.astype(BF16), k1=k1.astype(BF16),
                q_off1=q_off1.astype(BF16), k_off0=k_off0.astype(BF16), q_int=q_int.astype(BF16),
                k_st=k_st, decay=jnp.exp(b_end))


def _mixer_kernel(*refs, n_chunks, n_dec, n_cast):
    it = iter(refs)

    def take(k):
        return [next(it) for _ in range(k)]

    h_ref, x1_ref, d_h_ref, d_state_ref = take(4)
    proj_w = take(5)
    ws_ref, bsp_ref, ggla_ref, wpa_ref, wpb_ref, wo_ref, ws0_ref, bs0_ref = take(8)
    cast_src = take(n_cast)
    x2_ref, sout_ref, d_ya_ref, d_yb_ref, d_sout_ref, d_cv_ref, d_gate_ref = take(7)
    cast_dst = take(n_cast)
    s_scr, o_scr, p_scr, la_ref, d_p_scr, d_la_ref = take(6)
    t = pl.program_id(1)
    tt = h_ref.shape[0]

    @pl.when(t == 0)
    def _():
        s_scr[...] = jnp.zeros_like(s_scr)

    def put(lo, val):
        p_scr[:, lo:lo + val.shape[1]] = val[:tt].astype(BF16)
        d_p_scr[:, lo:lo + val.shape[1]] = val[tt:tt + n_dec]

    def put_la(val):
        la_ref[...] = val[:tt]
        d_la_ref[...] = val[tt:tt + n_dec]

    _activated_projection(jnp.concatenate([h_ref[...], d_h_ref[0]], axis=0), proj_w, put, put_la)
    uv_ref, qk_ref, v_ref, gb_ref, sga_ref, sgb_ref = (
        p_scr.at[:, j * D_MODEL:(j + 1) * D_MODEL] for j in range(6))
    d_uv_ref, d_qk_ref, d_v_ref, d_gb_ref = (
        d_p_scr.at[:, j * D_MODEL:(j + 1) * D_MODEL] for j in range(4))

    row = lax.broadcasted_iota(jnp.int32, (CHUNK, CHUNK), 0)
    col = lax.broadcasted_iota(jnp.int32, (CHUNK, CHUNK), 1)
    causal = row >= col
    ones_tril = causal.astype(BF16)
    ws_tril = [jnp.where(causal, ws_ref[g], 0.0).astype(BF16) for g in range(A_GROUPS)]
    zero_blk = jnp.zeros((HALF, B_DK), BF16)
    chunks = [slice(c * CHUNK, (c + 1) * CHUNK) for c in range(n_chunks)]
    heads = [(slice(h * B_DK, (h + 1) * B_DK), slice(h * B_DV, (h + 1) * B_DV))
             for h in range(B_HEADS)]

    ya = []
    for rows in chunks:
        u = uv_ref[rows, :A_WIDTH].astype(F32)
        vn = uv_ref[rows, A_WIDTH:]
        parts = []
        for g in range(A_GROUPS):
            cols = slice(g * A_GROUP_DIM, (g + 1) * A_GROUP_DIM)
            mixed = _dot(ws_tril[g], vn[:, cols]) + bsp_ref[:, g:g + 1]
            parts.append((u[:, cols] * mixed).astype(BF16))
        ya.append(jnp.concatenate(parts, axis=1))

    prep = [_gla_prep(la_ref[rows, :], qk_ref[rows, :].astype(F32), ones_tril) for rows in chunks]
    scores = []
    for p in prep:
        per_head = []
        for kc, _ in heads:
            top = _dot_nt(p["a0"][:, kc], jnp.concatenate([p["k0"][:, kc], zero_blk], axis=0))
            bot = _dot_nt(
                jnp.concatenate([p["q_off1"][:, kc], p["a1"][:, kc]], axis=1),
                jnp.concatenate([jnp.concatenate([p["k_off0"][:, kc], zero_blk], axis=1),
                                 jnp.concatenate([zero_blk, p["k1"][:, kc]], axis=1)], axis=0))
            s = jnp.concatenate([top, bot], axis=0)
            per_head.append(jnp.where(causal, s, 0.0).astype(BF16))
        scores.append(per_head)

    state = [s_scr[h] for h in range(B_HEADS)]
    state_before = []
    for rows, p in zip(chunks, prep):
        state_before.append([s.astype(BF16) for s in state])
        for h, (kc, vc) in enumerate(heads):
            upd = _dot(p["k_st"][:, kc].T.astype(BF16), v_ref[rows, vc])
            decay_col = jnp.broadcast_to(p["decay"][:, kc], (B_DK, B_DK)).T
            state[h] = state[h] * jnp.concatenate([decay_col, decay_col], axis=1) + upd
    for h in range(B_HEADS):
        s_scr[h] = state[h]

    yb = []
    for c, (rows, p) in enumerate(zip(chunks, prep)):
        gb = gb_ref[rows, :].astype(F32)
        parts = []
        for h, (kc, vc) in enumerate(heads):
            o = _dot(jnp.concatenate([scores[c][h], p["q_int"][:, kc]], axis=1),
                     jnp.concatenate([v_ref[rows, vc], state_before[c][h]], axis=0))
            parts.append((_rms(o, ggla_ref[...]) * gb[:, vc]).astype(BF16))
        yb.append(jnp.concatenate(parts, axis=1))

    pa = _dot(jnp.concatenate(ya, axis=0), wpa_ref[...])
    _cast_blocks(cast_src, cast_dst)
    _sample_mixers(d_uv_ref, d_qk_ref, d_v_ref, d_gb_ref, d_la_ref, d_state_ref, ws0_ref, bs0_ref,
                   ggla_ref, d_ya_ref.at[0], d_yb_ref.at[0], d_sout_ref, d_cv_ref.at[0], o_scr,
                   n_dec)
    d_gate_ref[0] = d_p_scr[:, LR_LO:]
    mix = (sga_ref[...].astype(F32) * pa
           + sgb_ref[...].astype(F32) * _dot(jnp.concatenate(yb, axis=0), wpb_ref[...]))
    x2_ref[...] = x1_ref[...] + _dot(mix.astype(BF16), wo_ref[...])

    @pl.when(t == pl.num_programs(1) - 1)
    def _():
        sout_ref[0] = s_scr[...]


def _sample_mixers(uv_ref, qk_ref, v_ref, gb_ref, la_ref, s_ref, ws0_ref, bs0_ref, ggla_ref,
                   ya_ref, yb_ref, sout_ref, cv_ref, o_scr, nb):
    vn = uv_ref[:, A_WIDTH:]
    cv_ref[...] = vn
    ya_ref[...] = uv_ref[:, :A_WIDTH] * (vn * ws0_ref[...] + bs0_ref[...])

    a = jnp.exp(la_ref[...])
    qk = qk_ref[...]
    qs = qk[:, :B_KW]
    k = qk[:, B_KW:]
    v = v_ref[...]
    pad = jnp.zeros((LANES - 3 * nb, B_DK), F32)
    for h in range(B_HEADS):
        kc = slice(h * B_DK, (h + 1) * B_DK)
        vc = slice(h * B_DV, (h + 1) * B_DV)
        xt = jnp.concatenate([a[:, kc], k[:, kc], qs[:, kc], pad], axis=0).T
        for n in range(nb):
            s_new = (s_ref[n, h] * xt[:, n:n + 1]
                     + xt[:, nb + n:nb + n + 1] * v[n:n + 1, vc])
            sout_ref[n, h] = s_new
            o_scr[n:n + 1, vc] = jnp.sum(xt[:, 2 * nb + n:2 * nb + n + 1] * s_new,
                                         axis=0, keepdims=True)
    gb = gb_ref[...]
    for h in range(B_HEADS):
        vc = slice(h * B_DV, (h + 1) * B_DV)
        yb_ref[:, vc] = _rms(o_scr[:, vc], ggla_ref[...]) * gb[:, vc]


def _merge_kernel(ya_ref, yb_ref, ga_ref, gb_ref, x1_ref, wpa_ref, wpb_ref, wo_ref, x2_ref):
    mix = (ga_ref[...].astype(F32) * _dot(ya_ref[...], wpa_ref[...])
           + gb_ref[...].astype(F32) * _dot(yb_ref[...], wpb_ref[...]))
    x2_ref[...] = x1_ref[...] + _dot(mix.astype(BF16), wo_ref[...])


def _ffn2_finish(x3, side, weights, outs, *, final_norm):
    (p_ref,), (y_ref,) = side, outs
    gple_ref, wpg_ref, wple_ref, gfin_ref = weights[3:]
    gate = jax.nn.sigmoid(_dot(_rms(x3, gple_ref[...]).astype(BF16), wpg_ref[...]))
    x4 = x3 + _dot(p_ref[...].astype(BF16), wple_ref[...]) * gate
    y_ref[...] = _rms(x4, gfin_ref[...]) if final_norm else x4


def _resident(shape):
    zeros = (0,) * len(shape)
    return pl.BlockSpec(shape, lambda *_: zeros, pipeline_mode=pl.Buffered(1))


def _rows(tm, width, colblk=0):
    return pl.BlockSpec((tm, width), lambda i: (i, colblk))


def _params(n_axes):
    return pltpu.CompilerParams(dimension_semantics=("arbitrary",) * n_axes,
                                vmem_limit_bytes=VMEM_LIMIT)


def _cast_specs(weights, n_steps, linear_step=lambda i: i):
    specs, shapes = [], []
    for w in weights:
        rows, cols = w.shape
        n_blocks = max(n for n in range(1, n_steps + 1) if rows % (n * BF16_SUBLANES) == 0)
        specs.append(pl.BlockSpec(
            (rows // n_blocks, cols),
            lambda *idx, n=n_blocks: (jnp.minimum(linear_step(*idx), n - 1), 0)))
        shapes.append(jax.ShapeDtypeStruct(w.shape, BF16))
    return specs, shapes


def _ffn_stage_call(name, finish, x, x_s, side, side_s, weights, outputs, to_cast=()):
    m, ms = x.shape[0], x_s.shape[0]
    assert m % TOKEN_TILE == 0
    n = m // TOKEN_TILE

    def tile(cols, lag):
        return pl.BlockSpec((TOKEN_TILE, cols), lambda i: (jnp.clip(i - lag, 0, n - 1), 0))

    def s_spec(cols):
        return pl.BlockSpec((ms, cols), lambda i: (0, 0))

    cast_specs, cast_shapes = _cast_specs(to_cast, n)
    outs = pl.pallas_call(
        functools.partial(_ffn_stage_kernel, finish=finish, n_side=len(side), n_w=len(weights),
                          n_out=len(outputs), n_cast=len(to_cast)),
        grid=(n + 1,),
        in_specs=[tile(D_MODEL, 0)] + [tile(a.shape[1], 1) for a in side]
                 + [s_spec(D_MODEL)] + [s_spec(a.shape[1]) for a in side_s]
                 + [_resident(w.shape) for w in weights] + cast_specs,
        out_specs=[tile(c, 1) for c, _ in outputs] + [s_spec(c) for c, _ in outputs] + cast_specs,
        out_shape=[jax.ShapeDtypeStruct((m, c), dt) for c, dt in outputs]
                  + [jax.ShapeDtypeStruct((ms, c), dt) for c, dt in outputs] + cast_shapes,
        scratch_shapes=[pltpu.VMEM((TOKEN_TILE, D_MODEL), F32)],
        compiler_params=_params(1),
        name=name,
    )(x, *side, x_s, *side_s, *weights, *to_cast)
    k = len(outputs)
    return outs[:k], outs[k:2 * k], outs[2 * k:]


def _mixer(h, x1, n_seq, seq, d_h, d_state, proj_w, ws, bsp_t, ggla, wpa, wpb, wo, ws0, bs0,
           to_cast=()):
    tt = MIXER_TILE
    nt = seq // tt
    m = n_seq * seq
    steps = n_seq * nt
    n_all = d_h.shape[0]
    assert n_all % steps == 0
    n_dec = n_all // steps
    assert n_dec <= BF16_SUBLANES
    d_h3 = jnp.pad(d_h.reshape(steps, n_dec, D_MODEL), ((0, 0), (0, BF16_SUBLANES - n_dec), (0, 0)))

    def seg(colblk, width=D_MODEL):
        return pl.BlockSpec((tt, width), lambda b, t: (b * nt + t, colblk))

    def d_seg(colblk, width=D_MODEL):
        return pl.BlockSpec((1, n_dec, width), lambda b, t: (b * nt + t, 0, colblk))

    d_state_spec = pl.BlockSpec((n_dec, B_HEADS, B_DK, B_DV), lambda b, t: (b * nt + t, 0, 0, 0))
    d_h_spec = pl.BlockSpec((1, BF16_SUBLANES, D_MODEL), lambda b, t: (b * nt + t, 0, 0))
    small = list(proj_w) + [ws, bsp_t, ggla, wpa, wpb, wo, ws0, bs0]
    cast_specs, cast_shapes = _cast_specs(to_cast, steps, lambda b, t: b * nt + t)
    x2, sp, ya, yb, ss, cv, gates, *cast = pl.pallas_call(
        functools.partial(_mixer_kernel, n_chunks=tt // CHUNK, n_dec=n_dec, n_cast=len(to_cast)),
        grid=(n_seq, nt),
        in_specs=[seg(0), seg(0), d_h_spec, d_state_spec]
                 + [_resident(w.shape) for w in small] + cast_specs,
        out_specs=[seg(0),
                   pl.BlockSpec((1, B_HEADS, B_DK, B_DV), lambda b, t: (b, 0, 0, 0)),
                   d_seg(0, A_WIDTH), d_seg(0, B_VW), d_state_spec, d_seg(0, A_WIDTH),
                   d_seg(0, 2 * D_MODEL)] + cast_specs,
        out_shape=[jax.ShapeDtypeStruct((m, D_MODEL), F32),
                   jax.ShapeDtypeStruct((n_seq, B_HEADS, B_DK, B_DV), F32),
                   jax.ShapeDtypeStruct((steps, n_dec, A_WIDTH), F32),
                   jax.ShapeDtypeStruct((steps, n_dec, B_VW), F32),
                   jax.ShapeDtypeStruct(d_state.shape, F32),
                   jax.ShapeDtypeStruct((steps, n_dec, A_WIDTH), F32),
                   jax.ShapeDtypeStruct((steps, n_dec, 2 * D_MODEL), F32)] + cast_shapes,
        scratch_shapes=[pltpu.VMEM((B_HEADS, B_DK, B_DV), F32), pltpu.VMEM((n_dec, B_VW), F32),
                        pltpu.VMEM((tt, P_COLS), BF16), pltpu.VMEM((tt, B_KW), F32),
                        pltpu.VMEM((n_dec, P_COLS), F32), pltpu.VMEM((n_dec, B_KW), F32)],
        compiler_params=_params(2),
        name="mixer",
    )(h, x1, d_h3, d_state, *small, *to_cast)
    return (x2, sp, ya.reshape(n_all, A_WIDTH), yb.reshape(n_all, B_VW), ss,
            cv.reshape(n_all, A_WIDTH), gates.reshape(n_all, 2 * D_MODEL), cast)


def _merge(ya, yb, gates, x1, wpa, wpb, wo):
    tm = m = x1.shape[0]
    return pl.pallas_call(
        _merge_kernel,
        grid=(m // tm,),
        in_specs=[_rows(tm, A_WIDTH), _rows(tm, B_VW), _rows(tm, D_MODEL, 0),
                  _rows(tm, D_MODEL, 1), _rows(tm, D_MODEL), _resident(wpa.shape),
                  _resident(wpb.shape), _resident(wo.shape)],
        out_specs=_rows(tm, D_MODEL),
        out_shape=jax.ShapeDtypeStruct((m, D_MODEL), F32),
        compiler_params=_params(1),
        name="merge",
    )(ya, yb, gates, gates, x1, wpa, wpb, wo)


def kernel(x_prompt, x_sample, p_prompt, p_sample, state_gla, g_ffn1, w_ffn1_in, w_ffn1_out, g_mix,
           w_in, ln_v_g, ln_v_b, w_spatial, b_spatial, w_gate_up, b_gate, g_gla_out, w_proj_a,
           w_proj_b, w_out, g_ffn2, w_ffn2_in, w_ffn2_out, g_ple, w_ple_gate, w_ple, g_final):
    depth = w_in.shape[0]
    n_seq, seq, _ = x_prompt.shape
    n_dec, dec_seq, _ = x_sample.shape
    assert dec_seq == 1 and seq % MIXER_TILE == 0

    def row(vec):
        return vec.reshape(1, -1).astype(F32)

    assert sum(IN_SIZES[:6]) == LR_LO and sum(IN_SIZES) == GATE_LO + 2 * D_MODEL
    xp = x_prompt.reshape(n_seq * seq, D_MODEL)
    xs = x_sample.reshape(n_dec, D_MODEL)
    gfin = row(g_final)
    sp_list, ss_list, vs_list = [], [], []
    for i in range(depth):
        w1i, w1o = w_ffn1_in[i].astype(BF16), w_ffn1_out[i].astype(BF16)
        wgu = jnp.pad(w_gate_up[i], ((0, LANES - GATE_RANK), (0, 0))).astype(BF16)
        g1, gmix, g2, gple = row(g_ffn1[i]), row(g_mix[i]), row(g_ffn2[i]), row(g_ple[i])
        lng, lnb, bg, ggla = row(ln_v_g[i]), row(ln_v_b[i]), row(b_gate[i]), row(g_gla_out[i])
        ws0 = row(jnp.repeat(w_spatial[i][:, 0, 0], A_GROUP_DIM))
        bs0 = row(jnp.repeat(b_spatial[i][:, 0], A_GROUP_DIM))

        wide, act = (D_MODEL, F32), (D_MODEL, BF16)
        (x1p, hp), (x1s, hs), (w_inb, wpa, wpb, wo) = _ffn_stage_call(
            "ffn1", _ffn1_finish, xp, xs, [], [], [g1, w1i, w1o, gmix], [wide, act],
            to_cast=[w_in[i].T, w_proj_a[i], w_proj_b[i], w_out[i]])
        x2p, sp, ya, yb, ss, cv, gates, (w2i, w2o, wpg, wple) = _mixer(
            hp, x1p, n_seq, seq, hs, state_gla[i], [w_inb, wgu, bg, lng, lnb], w_spatial[i],
            b_spatial[i].T, ggla, wpa, wpb, wo, ws0, bs0,
            to_cast=[w_ffn2_in[i], w_ffn2_out[i], w_ple_gate[i], w_ple[i]])
        x2s = _merge(ya.astype(BF16), yb.astype(BF16), gates, x1s, wpa, wpb, wo)

        ple_p = p_prompt[i].reshape(n_seq * seq, PLE_DIM)
        ple_s = p_sample[i].reshape(n_dec, PLE_DIM)
        (xp,), (xs,), _ = _ffn_stage_call(
            "ffn2", functools.partial(_ffn2_finish, final_norm=i == depth - 1), x2p, x2s,
            [ple_p], [ple_s], [g2, w2i, w2o, gple, wpg, wple, gfin], [wide])

        sp_list.append(sp)
        ss_list.append(ss)
        vs_list.append(cv.reshape(n_dec, dec_seq, A_WIDTH))
    return (xp.reshape(n_seq, seq, D_MODEL), xs.reshape(n_dec, dec_seq, D_MODEL),
            jnp.stack(sp_list), jnp.stack(ss_list), jnp.stack(vs_list))
```

```python
import functools

import jax
import jax.numpy as jnp
from jax import lax
from jax.experimental import pallas as pl
from jax.experimental.pallas import tpu as pltpu

D_MODEL = 1024
D_FF = 2816
PLE_DIM = 256
CHUNK = 128
A_GROUPS = 4
A_GROUP_DIM = 128
A_WIDTH = A_GROUPS * A_GROUP_DIM
B_HEADS = 4
B_DK = 128
B_DV = 256
B_KW = B_HEADS * B_DK
B_VW = B_HEADS * B_DV
GATE_RANK = 16
GATE_NORM = 16.0
EPS = 1e-6
IN_SIZES = (A_WIDTH, A_WIDTH, B_KW, B_KW, B_VW, B_VW, GATE_RANK, D_MODEL, D_MODEL)

LANES = 128
P_COLS = 6 * D_MODEL
QK_LO = 2 * A_WIDTH
VB_LO = QK_LO + 2 * B_KW
GB_LO = VB_LO + B_VW
LR_LO = GB_LO + B_VW
GATE_LO = LR_LO + GATE_RANK
FF_CHUNK = 256
OUT_GROUP = 11
HALF = CHUNK // 2
MIXER_TILE = 4 * CHUNK
TOKEN_TILE = 512
BF16_SUBLANES = 16
VMEM_LIMIT = 56 * 1024 * 1024

F32 = jnp.float32
BF16 = jnp.bfloat16


def _dot(a, b):
    return jnp.dot(a, b, preferred_element_type=F32)


def _dot_nt(a, b):
    return lax.dot_general(a, b, (((1,), (1,)), ((), ())), preferred_element_type=F32)


def _rms(x, g):
    return x * lax.rsqrt(jnp.mean(x * x, axis=-1, keepdims=True) + EPS) * g


def _log_sigmoid(x):
    return jnp.minimum(x, 0.0) - jnp.log1p(jnp.exp(-jnp.abs(x)))


def _layernorm(x, g, b):
    mu = jnp.mean(x, axis=-1, keepdims=True)
    xc = x - mu
    var = jnp.mean(xc * xc, axis=-1, keepdims=True)
    return xc * lax.rsqrt(var + EPS) * g + b


def _swiglu_residual(x, g_ref, w_in_ref, w_out_ref, early):
    h = _rms(x, g_ref[...]).astype(BF16)
    acc = None
    acts, group_lo = [], 0
    n_chunks = D_FF // FF_CHUNK
    for ci, lo in enumerate(range(0, D_FF, FF_CHUNK)):
        gate = _dot(h, w_in_ref[:, lo:lo + FF_CHUNK])
        up = _dot(h, w_in_ref[:, D_FF + lo:D_FF + lo + FF_CHUNK])
        acts.append((jax.nn.silu(gate) * up).astype(BF16))
        if len(acts) == OUT_GROUP or ci == n_chunks - 1:
            hi = lo + FF_CHUNK
            part = _dot(jnp.concatenate(acts, axis=1), w_out_ref[group_lo:hi, :])
            acc = part if acc is None else acc + part
            acts, group_lo = [], hi
        if ci == 1:
            early()
    return x + 0.5 * acc


def _cast_blocks(src_refs, dst_refs):
    for src, dst in zip(src_refs, dst_refs):
        dst[...] = src[...].astype(dst.dtype)


def _ffn_stage_kernel(*refs, finish, n_side, n_w, n_out, n_cast):
    it = iter(refs)

    def take(k):
        return [next(it) for _ in range(k)]

    (x_ref,), p_side, (xs_ref,), s_side = take(1), take(n_side), take(1), take(n_side)
    weights, cast_src = take(n_w), take(n_cast)
    p_out, s_out, cast_dst, (carry,) = take(n_out), take(n_out), take(n_cast), take(1)
    _cast_blocks(cast_src, cast_dst)
    step = pl.program_id(0)
    sample_step = pl.num_programs(0) - 1

    def finish_previous_tile():
        finish(carry[...], p_side, weights, p_out)

    @pl.when(step == 0)
    def _():
        carry[...] = jnp.zeros_like(carry)

    @pl.when(step < sample_step)
    def _():
        carry[...] = _swiglu_residual(x_ref[...], *weights[:3], finish_previous_tile)

    @pl.when(step == sample_step)
    def _():
        x_new = _swiglu_residual(xs_ref[...], *weights[:3], finish_previous_tile)
        finish(x_new, s_side, weights, s_out)


def _ffn1_finish(x1, side, weights, outs):
    gmix_ref, (x1_ref, h_ref) = weights[3], outs
    x1_ref[...] = x1
    h_ref[...] = _rms(x1, gmix_ref[...]).astype(BF16)


def _activated_projection(h, weights, put, put_la):
    wt_ref, wgu_ref, bg_ref, lng_ref, lnb_ref = weights

    def proj(lo, width):
        return _dot_nt(h, wt_ref[lo:lo + width, :])

    lr = proj(LR_LO, LANES).astype(BF16)
    logit = _dot(lr, wgu_ref[...]) + bg_ref[...]
    put_la(_log_sigmoid(logit) * (1.0 / GATE_NORM))
    half = D_MODEL // 2
    segments = [(0, 0, jax.nn.gelu),
                (A_WIDTH, A_WIDTH,
                 lambda z: _layernorm(jax.nn.gelu(z), lng_ref[...], lnb_ref[...])),
                (QK_LO, QK_LO, lambda z: z * (B_DK ** -0.5)),
                (QK_LO + B_KW, QK_LO + B_KW, lambda z: z)]
    segments += [(VB_LO + j * half, VB_LO + j * half, lambda z: z) for j in range(2)]
    segments += [(GATE_LO + j * half, LR_LO + j * half, jax.nn.sigmoid) for j in range(4)]
    segments += [(GB_LO + j * half, GB_LO + j * half, jax.nn.silu) for j in range(2)]
    for w_lo, p_lo, act in segments:
        put(p_lo, act(proj(w_lo, half)))


def _gla_prep(la, qk, ones_tril):
    la_hi = la.astype(BF16)
    la_lo = (la - la_hi.astype(F32)).astype(BF16)
    b = _dot(jnp.concatenate([ones_tril, ones_tril], axis=1),
             jnp.concatenate([la_hi, la_lo], axis=0))
    r0 = b[HALF // 2 - 1:HALF // 2, :]
    r1 = b[HALF + HALF // 2 - 1:HALF + HALF // 2, :]
    b_mid = b[HALF - 1:HALF, :]
    b_end = b[CHUNK - 1:CHUNK, :]
    bq0 = b[:HALF] - r0
    bq1 = b[HALF:] - r1
    qs = qk[:, :B_KW]
    k = qk[:, B_KW:]
    a0 = qs[:HALF] * jnp.exp(bq0)
    a1 = qs[HALF:] * jnp.exp(bq1)
    k0 = k[:HALF] * jnp.exp(-bq0)
    k1 = k[HALF:] * jnp.exp(-bq1)
    q_off1 = a1 * jnp.exp(r1 - b_mid)
    k_off0 = k0 * jnp.exp(b_mid - r0)
    q_int = jnp.concatenate([a0 * jnp.exp(r0), q_off1 * jnp.exp(b_mid)], axis=0)
    k_st = jnp.concatenate([k_off0 * jnp.exp(b_end - b_mid), k1 * jnp.exp(b_end - r1)], axis=0)
    return dict(a0=a0.astype(BF16), a1=a1.astype(BF16), k0=k0.astype(BF16), k1=k1.astype(BF16),
                q_off1=q_off1.astype(BF16), k_off0=k_off0.astype(BF16), q_int=q_int.astype(BF16),
                k_st=k_st, decay=jnp.exp(b_end))


def _mixer_kernel(*refs, n_chunks, n_dec, n_cast):
    it = iter(refs)

    def take(k):
        return [next(it) for _ in range(k)]

    h_ref, x1_ref, d_h_ref, d_state_ref = take(4)
    proj_w = take(5)
    ws_ref, bsp_ref, ggla_ref, wpa_ref, wpb_ref, wo_ref, ws0_ref, bs0_ref = take(8)
    cast_src = take(n_cast)
    x2_ref, sout_ref, d_ya_ref, d_yb_ref, d_sout_ref, d_cv_ref, d_gate_ref = take(7)
    cast_dst = take(n_cast)
    s_scr, o_scr, p_scr, la_ref, d_p_scr, d_la_ref = take(6)
    t = pl.program_id(1)
    tt = h_ref.shape[0]

    @pl.when(t == 0)
    def _():
        s_scr[...] = jnp.zeros_like(s_scr)

    def put(lo, val):
        p_scr[:, lo:lo + val.shape[1]] = val[:tt].astype(BF16)
        d_p_scr[:, lo:lo + val.shape[1]] = val[tt:tt + n_dec]

    def put_la(val):
        la_ref[...] = val[:tt]
        d_la_ref[...] = val[tt:tt + n_dec]

    _activated_projection(jnp.concatenate([h_ref[...], d_h_ref[0]], axis=0), proj_w, put, put_la)
    uv_ref, qk_ref, v_ref, gb_ref, sga_ref, sgb_ref = (
        p_scr.at[:, j * D_MODEL:(j + 1) * D_MODEL] for j in range(6))
    d_uv_ref, d_qk_ref, d_v_ref, d_gb_ref = (
        d_p_scr.at[:, j * D_MODEL:(j + 1) * D_MODEL] for j in range(4))

    row = lax.broadcasted_iota(jnp.int32, (CHUNK, CHUNK), 0)
    col = lax.broadcasted_iota(jnp.int32, (CHUNK, CHUNK), 1)
    causal = row >= col
    ones_tril = causal.astype(BF16)
    ws_tril = [jnp.where(causal, ws_ref[g], 0.0).astype(BF16) for g in range(A_GROUPS)]
    zero_blk = jnp.zeros((HALF, B_DK), BF16)
    chunks = [slice(c * CHUNK, (c + 1) * CHUNK) for c in range(n_chunks)]
    heads = [(slice(h * B_DK, (h + 1) * B_DK), slice(h * B_DV, (h + 1) * B_DV))
             for h in range(B_HEADS)]

    ya = []
    for rows in chunks:
        u = uv_ref[rows, :A_WIDTH].astype(F32)
        vn = uv_ref[rows, A_WIDTH:]
        parts = []
        for g in range(A_GROUPS):
            cols = slice(g * A_GROUP_DIM, (g + 1) * A_GROUP_DIM)
            mixed = _dot(ws_tril[g], vn[:, cols]) + bsp_ref[:, g:g + 1]
            parts.append((u[:, cols] * mixed).astype(BF16))
        ya.append(jnp.concatenate(parts, axis=1))

    prep = [_gla_prep(la_ref[rows, :], qk_ref[rows, :].astype(F32), ones_tril) for rows in chunks]
    scores = []
    for p in prep:
        per_head = []
        for kc, _ in heads:
            top = _dot_nt(p["a0"][:, kc], jnp.concatenate([p["k0"][:, kc], zero_blk], axis=0))
            bot = _dot_nt(
                jnp.concatenate([p["q_off1"][:, kc], p["a1"][:, kc]], axis=1),
                jnp.concatenate([jnp.concatenate([p["k_off0"][:, kc], zero_blk], axis=1),
                                 jnp.concatenate([zero_blk, p["k1"][:, kc]], axis=1)], axis=0))
            s = jnp.concatenate([top, bot], axis=0)
            per_head.append(jnp.where(causal, s, 0.0).astype(BF16))
        scores.append(per_head)

    state = [s_scr[h] for h in range(B_HEADS)]
    state_before = []
    for rows, p in zip(chunks, prep):
        state_before.append([s.astype(BF16) for s in state])
        for h, (kc, vc) in enumerate(heads):
            upd = _dot(p["k_st"][:, kc].T.astype(BF16), v_ref[rows, vc])
            decay_col = jnp.broadcast_to(p["decay"][:, kc], (B_DK, B_DK)).T
            state[h] = state[h] * jnp.concatenate([decay_col, decay_col], axis=1) + upd
    for h in range(B_HEADS):
        s_scr[h] = state[h]

    yb = []
    for c, (rows, p) in enumerate(zip(chunks, prep)):
        gb = gb_ref[rows, :].astype(F32)
        parts = []
        for h, (kc, vc) in enumerate(heads):
            o = _dot(jnp.concatenate([scores[c][h], p["q_int"][:, kc]], axis=1),
                     jnp.concatenate([v_ref[rows, vc], state_before[c][h]], axis=0))
            parts.append((_rms(o, ggla_ref[...]) * gb[:, vc]).astype(BF16))
        yb.append(jnp.concatenate(parts, axis=1))

    pa = _dot(jnp.concatenate(ya, axis=0), wpa_ref[...])
    _cast_blocks(cast_src, cast_dst)
    _sample_mixers(d_uv_ref, d_qk_ref, d_v_ref, d_gb_ref, d_la_ref, d_state_ref, ws0_ref, bs0_ref,
                   ggla_ref, d_ya_ref.at[0], d_yb_ref.at[0], d_sout_ref, d_cv_ref.at[0], o_scr,
                   n_dec)
    d_gate_ref[0] = d_p_scr[:, LR_LO:]
    mix = (sga_ref[...].astype(F32) * pa
           + sgb_ref[...].astype(F32) * _dot(jnp.concatenate(yb, axis=0), wpb_ref[...]))
    x2_ref[...] = x1_ref[...] + _dot(mix.astype(BF16), wo_ref[...])

    @pl.when(t == pl.num_programs(1) - 1)
    def _():
        sout_ref[0] = s_scr[...]


def _sample_mixers(uv_ref, qk_ref, v_ref, gb_ref, la_ref, s_ref, ws0_ref, bs0_ref, ggla_ref,
                   ya_ref, yb_ref, sout_ref, cv_ref, o_scr, nb):
    vn = uv_ref[:, A_WIDTH:]
    cv_ref[...] = vn
    ya_ref[...] = uv_ref[:, :A_WIDTH] * (vn * ws0_ref[...] + bs0_ref[...])

    a = jnp.exp(la_ref[...])
    qk = qk_ref[...]
    qs = qk[:, :B_KW]
    k = qk[:, B_KW:]
    v = v_ref[...]
    pad = jnp.zeros((LANES - 3 * nb, B_DK), F32)
    for h in range(B_HEADS):
        kc = slice(h * B_DK, (h + 1) * B_DK)
        vc = slice(h * B_DV, (h + 1) * B_DV)
        xt = jnp.concatenate([a[:, kc], k[:, kc], qs[:, kc], pad], axis=0).T
        for n in range(nb):
            s_new = (s_ref[n, h] * xt[:, n:n + 1]
                     + xt[:, nb + n:nb + n + 1] * v[n:n + 1, vc])
            sout_ref[n, h] = s_new
            o_scr[n:n + 1, vc] = jnp.sum(xt[:, 2 * nb + n:2 * nb + n + 1] * s_new,
                                         axis=0, keepdims=True)
    gb = gb_ref[...]
    for h in range(B_HEADS):
        vc = slice(h * B_DV, (h + 1) * B_DV)
        yb_ref[:, vc] = _rms(o_scr[:, vc], ggla_ref[...]) * gb[:, vc]


def _merge_kernel(ya_ref, yb_ref, ga_ref, gb_ref, x1_ref, wpa_ref, wpb_ref, wo_ref, x2_ref):
    mix = (ga_ref[...].astype(F32) * _dot(ya_ref[...], wpa_ref[...])
           + gb_ref[...].astype(F32) * _dot(yb_ref[...], wpb_ref[...]))
    x2_ref[...] = x1_ref[...] + _dot(mix.astype(BF16), wo_ref[...])


def _ffn2_finish(x3, side, weights, outs, *, final_norm):
    (p_ref,), (y_ref,) = side, outs
    gple_ref, wpg_ref, wple_ref, gfin_ref = weights[3:]
    gate = jax.nn.sigmoid(_dot(_rms(x3, gple_ref[...]).astype(BF16), wpg_ref[...]))
    x4 = x3 + _dot(p_ref[...].astype(BF16), wple_ref[...]) * gate
    y_ref[...] = _rms(x4, gfin_ref[...]) if final_norm else x4


def _resident(shape):
    zeros = (0,) * len(shape)
    return pl.BlockSpec(shape, lambda *_: zeros, pipeline_mode=pl.Buffered(1))


def _rows(tm, width, colblk=0):
    return pl.BlockSpec((tm, width), lambda i: (i, colblk))


def _params(n_axes):
    return pltpu.CompilerParams(dimension_semantics=("arbitrary",) * n_axes,
                                vmem_limit_bytes=VMEM_LIMIT)


def _cast_specs(weights, n_steps, linear_step=lambda i: i):
    specs, shapes = [], []
    for w in weights:
        rows, cols = w.shape
        n_blocks = max(n for n in range(1, n_steps + 1) if rows % (n * BF16_SUBLANES) == 0)
        specs.append(pl.BlockSpec(
            (rows // n_blocks, cols),
            lambda *idx, n=n_blocks: (jnp.minimum(linear_step(*idx), n - 1), 0)))
        shapes.append(jax.ShapeDtypeStruct(w.shape, BF16))
    return specs, shapes


def _ffn_stage_call(name, finish, x, x_s, side, side_s, weights, outputs, to_cast=()):
    m, ms = x.shape[0], x_s.shape[0]
    assert m % TOKEN_TILE == 0
    n = m // TOKEN_TILE

    def tile(cols, lag):
        return pl.BlockSpec((TOKEN_TILE, cols), lambda i: (jnp.clip(i - lag, 0, n - 1), 0))

    def s_spec(cols):
        return pl.BlockSpec((ms, cols), lambda i: (0, 0))

    cast_specs, cast_shapes = _cast_specs(to_cast, n)
    outs = pl.pallas_call(
        functools.partial(_ffn_stage_kernel, finish=finish, n_side=len(side), n_w=len(weights),
                          n_out=len(outputs), n_cast=len(to_cast)),
        grid=(n + 1,),
        in_specs=[tile(D_MODEL, 0)] + [tile(a.shape[1], 1) for a in side]
                 + [s_spec(D_MODEL)] + [s_spec(a.shape[1]) for a in side_s]
                 + [_resident(w.shape) for w in weights] + cast_specs,
        out_specs=[tile(c, 1) for c, _ in outputs] + [s_spec(c) for c, _ in outputs] + cast_specs,
        out_shape=[jax.ShapeDtypeStruct((m, c), dt) for c, dt in outputs]
                  + [jax.ShapeDtypeStruct((ms, c), dt) for c, dt in outputs] + cast_shapes,
        scratch_shapes=[pltpu.VMEM((TOKEN_TILE, D_MODEL), F32)],
        compiler_params=_params(1),
        name=name,
    )(x, *side, x_s, *side_s, *weights, *to_cast)
    k = len(outputs)
    return outs[:k], outs[k:2 * k], outs[2 * k:]


def _mixer(h, x1, n_seq, seq, d_h, d_state, proj_w, ws, bsp_t, ggla, wpa, wpb, wo, ws0, bs0,
           to_cast=()):
    tt = MIXER_TILE
    nt = seq // tt
    m = n_seq * seq
    steps = n_seq * nt
    n_all = d_h.shape[0]
    assert n_all % steps == 0
    n_dec = n_all // steps
    assert n_dec <= BF16_SUBLANES
    d_h3 = jnp.pad(d_h.reshape(steps, n_dec, D_MODEL), ((0, 0), (0, BF16_SUBLANES - n_dec), (0, 0)))

    def seg(colblk, width=D_MODEL):
        return pl.BlockSpec((tt, width), lambda b, t: (b * nt + t, colblk))

    def d_seg(colblk, width=D_MODEL):
        return pl.BlockSpec((1, n_dec, width), lambda b, t: (b * nt + t, 0, colblk))

    d_state_spec = pl.BlockSpec((n_dec, B_HEADS, B_DK, B_DV), lambda b, t: (b * nt + t, 0, 0, 0))
    d_h_spec = pl.BlockSpec((1, BF16_SUBLANES, D_MODEL), lambda b, t: (b * nt + t, 0, 0))
    small = list(proj_w) + [ws, bsp_t, ggla, wpa, wpb, wo, ws0, bs0]
    cast_specs, cast_shapes = _cast_specs(to_cast, steps, lambda b, t: b * nt + t)
    x2, sp, ya, yb, ss, cv, gates, *cast = pl.pallas_call(
        functools.partial(_mixer_kernel, n_chunks=tt // CHUNK, n_dec=n_dec, n_cast=len(to_cast)),
        grid=(n_seq, nt),
        in_specs=[seg(0), seg(0), d_h_spec, d_state_spec]
                 + [_resident(w.shape) for w in small] + cast_specs,
        out_specs=[seg(0),
                   pl.BlockSpec((1, B_HEADS, B_DK, B_DV), lambda b, t: (b, 0, 0, 0)),
                   d_seg(0, A_WIDTH), d_seg(0, B_VW), d_state_spec, d_seg(0, A_WIDTH),
                   d_seg(0, 2 * D_MODEL)] + cast_specs,
        out_shape=[jax.ShapeDtypeStruct((m, D_MODEL), F32),
                   jax.ShapeDtypeStruct((n_seq, B_HEADS, B_DK, B_DV), F32),
                   jax.ShapeDtypeStruct((steps, n_dec, A_WIDTH), F32),
                   jax.ShapeDtypeStruct((steps, n_dec, B_VW), F32),
                   jax.ShapeDtypeStruct(d_state.shape, F32),
                   jax.ShapeDtypeStruct((steps, n_dec, A_WIDTH), F32),
                   jax.ShapeDtypeStruct((steps, n_dec, 2 * D_MODEL), F32)] + cast_shapes,
        scratch_shapes=[pltpu.VMEM((B_HEADS, B_DK, B_DV), F32), pltpu.VMEM((n_dec, B_VW), F32),
                        pltpu.VMEM((tt, P_COLS), BF16), pltpu.VMEM((tt, B_KW), F32),
                        pltpu.VMEM((n_dec, P_COLS), F32), pltpu.VMEM((n_dec, B_KW), F32)],
        compiler_params=_params(2),
        name="mixer",
    )(h, x1, d_h3, d_state, *small, *to_cast)
    return (x2, sp, ya.reshape(n_all, A_WIDTH), yb.reshape(n_all, B_VW), ss,
            cv.reshape(n_all, A_WIDTH), gates.reshape(n_all, 2 * D_MODEL), cast)


def _merge(ya, yb, gates, x1, wpa, wpb, wo):
    tm = m = x1.shape[0]
    return pl.pallas_call(
        _merge_kernel,
        grid=(m // tm,),
        in_specs=[_rows(tm, A_WIDTH), _rows(tm, B_VW), _rows(tm, D_MODEL, 0),
                  _rows(tm, D_MODEL, 1), _rows(tm, D_MODEL), _resident(wpa.shape),
                  _resident(wpb.shape), _resident(wo.shape)],
        out_specs=_rows(tm, D_MODEL),
        out_shape=jax.ShapeDtypeStruct((m, D_MODEL), F32),
        compiler_params=_params(1),
        name="merge",
    )(ya, yb, gates, gates, x1, wpa, wpb, wo)


def kernel(x_prompt, x_sample, p_prompt, p_sample, state_gla, g_ffn1, w_ffn1_in, w_ffn1_out, g_mix,
           w_in, ln_v_g, ln_v_b, w_spatial, b_spatial, w_gate_up, b_gate, g_gla_out, w_proj_a,
           w_proj_b, w_out, g_ffn2, w_ffn2_in, w_ffn2_out, g_ple, w_ple_gate, w_ple, g_final):
    depth = w_in.shape[0]
    n_seq, seq, _ = x_prompt.shape
    n_dec, dec_seq, _ = x_sample.shape
    assert dec_seq == 1 and seq % MIXER_TILE == 0

    def row(vec):
        return vec.reshape(1, -1).astype(F32)

    assert sum(IN_SIZES[:6]) == LR_LO and sum(IN_SIZES) == GATE_LO + 2 * D_MODEL
    xp = x_prompt.reshape(n_seq * seq, D_MODEL)
    xs = x_sample.reshape(n_dec, D_MODEL)
    gfin = row(g_final)
    sp_list, ss_list, vs_list = [], [], []
    for i in range(depth):
        w1i, w1o = w_ffn1_in[i].astype(BF16), w_ffn1_out[i].astype(BF16)
        wgu = jnp.pad(w_gate_up[i], ((0, LANES - GATE_RANK), (0, 0))).astype(BF16)
        g1, gmix, g2, gple = row(g_ffn1[i]), row(g_mix[i]), row(g_ffn2[i]), row(g_ple[i])
        lng, lnb, bg, ggla = row(ln_v_g[i]), row(ln_v_b[i]), row(b_gate[i]), row(g_gla_out[i])
        ws0 = row(jnp.repeat(w_spatial[i][:, 0, 0], A_GROUP_DIM))
        bs0 = row(jnp.repeat(b_spatial[i][:, 0], A_GROUP_DIM))

        wide, act = (D_MODEL, F32), (D_MODEL, BF16)
        (x1p, hp), (x1s, hs), (w_inb, wpa, wpb, wo) = _ffn_stage_call(
            "ffn1", _ffn1_finish, xp, xs, [], [], [g1, w1i, w1o, gmix], [wide, act],
            to_cast=[w_in[i].T, w_proj_a[i], w_proj_b[i], w_out[i]])
        x2p, sp, ya, yb, ss, cv, gates, (w2i, w2o, wpg, wple) = _mixer(
            hp, x1p, n_seq, seq, hs, state_gla[i], [w_inb, wgu, bg, lng, lnb], w_spatial[i],
            b_spatial[i].T, ggla, wpa, wpb, wo, ws0, bs0,
            to_cast=[w_ffn2_in[i], w_ffn2_out[i], w_ple_gate[i], w_ple[i]])
        x2s = _merge(ya.astype(BF16), yb.astype(BF16), gates, x1s, wpa, wpb, wo)

        ple_p = p_prompt[i].reshape(n_seq * seq, PLE_DIM)
        ple_s = p_sample[i].reshape(n_dec, PLE_DIM)
        (xp,), (xs,), _ = _ffn_stage_call(
            "ffn2", functools.partial(_ffn2_finish, final_norm=i == depth - 1), x2p, x2s,
            [ple_p], [ple_s], [g2, w2i, w2o, gple, wpg, wple, gfin], [wide])

        sp_list.append(sp)
        ss_list.append(ss)
        vs_list.append(cv.reshape(n_dec, dec_seq, A_WIDTH))
    return (xp.reshape(n_seq, seq, D_MODEL), xs.reshape(n_dec, dec_seq, D_MODEL),
            jnp.stack(sp_list), jnp.stack(ss_list), jnp.stack(vs_list))
```

```python
import functools

import jax
import jax.numpy as jnp
from jax import lax
from jax.experimental import pallas as pl
from jax.experimental.pallas import tpu as pltpu

D_MODEL = 1024
D_FF = 2816
PLE_DIM = 256
CHUNK = 128
A_GROUPS = 4
A_GROUP_DIM = 128
A_WIDTH = A_GROUPS * A_GROUP_DIM
B_HEADS = 4
B_DK = 128
B_DV = 256
B_KW = B_HEADS * B_DK
B_VW = B_HEADS * B_DV
GATE_RANK = 16
GATE_NORM = 16.0
EPS = 1e-6
IN_SIZES = (A_WIDTH, A_WIDTH, B_KW, B_KW, B_VW, B_VW, GATE_RANK, D_MODEL, D_MODEL)

LANES = 128
P_COLS = 6 * D_MODEL
QK_LO = 2 * A_WIDTH
VB_LO = QK_LO + 2 * B_KW
GB_LO = VB_LO + B_VW
LR_LO = GB_LO + B_VW
GATE_LO = LR_LO + GATE_RANK
FF_CHUNK = 256
LOAD_CHUNKS = 11
HALF = CHUNK // 2
MIXER_TILE = 4 * CHUNK
TOKEN_TILE = 512
BF16_SUBLANES = 16
VMEM_LIMIT = 56 * 1024 * 1024

F32 = jnp.float32
BF16 = jnp.bfloat16


def _dot(a, b):
    return jnp.dot(a, b, preferred_element_type=F32)


def _dot_nt(a, b):
    return lax.dot_general(a, b, (((1,), (1,)), ((), ())), preferred_element_type=F32)


def _rms(x, g):
    return x * lax.rsqrt(jnp.mean(x * x, axis=-1, keepdims=True) + EPS) * g


def _log_sigmoid(x):
    return jnp.minimum(x, 0.0) - jnp.log1p(jnp.exp(-jnp.abs(x)))


def _layernorm(x, g, b):
    mu = jnp.mean(x, axis=-1, keepdims=True)
    xc = x - mu
    var = jnp.mean(xc * xc, axis=-1, keepdims=True)
    return xc * lax.rsqrt(var + EPS) * g + b


def _swiglu_residual(x, g_ref, w_in_ref, w_out_ref, early):
    h = _rms(x, g_ref[...]).astype(BF16)
    acts = []
    for ci, lo in enumerate(range(0, D_FF, FF_CHUNK)):
        gate = _dot(h, w_in_ref[:, lo:lo + FF_CHUNK])
        up = _dot(h, w_in_ref[:, D_FF + lo:D_FF + lo + FF_CHUNK])
        acts.append((jax.nn.silu(gate) * up).astype(BF16))
        if ci == 1:
            early()
    return x + 0.5 * _dot(jnp.concatenate(acts, axis=1), w_out_ref[...])


def _cast_blocks(src_refs, dst_refs):
    for src, dst in zip(src_refs, dst_refs):
        dst[...] = src[...].astype(dst.dtype)


def _load_as_bf16(src_hbm, dst, stage, sem, axis):
    size = src_hbm.shape[axis] // LOAD_CHUNKS

    def piece(ref, c):
        return ref.at[:, pl.ds(c * size, size)] if axis == 1 else ref.at[pl.ds(c * size, size), :]

    def copy(c):
        return pltpu.make_async_copy(piece(src_hbm, c), stage.at[c % 2], sem.at[c % 2])

    copy(0).start()
    for c in range(LOAD_CHUNKS):
        if c + 1 < LOAD_CHUNKS:
            copy(c + 1).start()
        copy(c).wait()
        if axis == 1:
            dst[:, c * size:(c + 1) * size] = stage[c % 2].astype(BF16)
        else:
            dst[c * size:(c + 1) * size, :] = stage[c % 2].astype(BF16)


def _ffn_stage_kernel(*refs, finish, n_side, n_w, n_out, n_cast, own_weights):
    it = iter(refs)

    def take(k):
        return [next(it) for _ in range(k)]

    (x_ref,), p_side, (xs_ref,), s_side = take(1), take(n_side), take(1), take(n_side)
    weights, cast_src = take(n_w), take(n_cast)
    p_out, s_out, cast_dst, (carry,) = take(n_out), take(n_out), take(n_cast), take(1)
    _cast_blocks(cast_src, cast_dst)
    step = pl.program_id(0)
    sample_step = pl.num_programs(0) - 1
    if own_weights:
        w_in_scr, w_out_scr, stage_in, stage_out, sem_in, sem_out = take(6)
        w_in_hbm, w_out_hbm = weights[1], weights[2]
        weights = [weights[0], w_in_scr, w_out_scr] + weights[3:]

        @pl.when(step == 0)
        def _():
            _load_as_bf16(w_in_hbm, w_in_scr, stage_in, sem_in, axis=1)
            _load_as_bf16(w_out_hbm, w_out_scr, stage_out, sem_out, axis=0)

    def finish_previous_tile():
        finish(carry[...], p_side, weights, p_out)

    @pl.when(step == 0)
    def _():
        carry[...] = jnp.zeros_like(carry)

    @pl.when(step < sample_step)
    def _():
        carry[...] = _swiglu_residual(x_ref[...], *weights[:3], finish_previous_tile)

    @pl.when(step == sample_step)
    def _():
        x_new = _swiglu_residual(xs_ref[...], *weights[:3], finish_previous_tile)
        finish(x_new, s_side, weights, s_out)


def _ffn1_finish(x1, side, weights, outs):
    gmix_ref, (x1_ref, h_ref) = weights[3], outs
    x1_ref[...] = x1
    h_ref[...] = _rms(x1, gmix_ref[...]).astype(BF16)


def _activated_projection(h, weights, put, put_la):
    wt_ref, wgu_ref, bg_ref, lng_ref, lnb_ref = weights

    def proj(lo, width):
        return _dot_nt(h, wt_ref[lo:lo + width, :])

    lr = proj(LR_LO, LANES).astype(BF16)
    logit = _dot(lr, wgu_ref[...]) + bg_ref[...]
    put_la(_log_sigmoid(logit) * (1.0 / GATE_NORM))
    half = D_MODEL // 2
    segments = [(0, 0, jax.nn.gelu),
                (A_WIDTH, A_WIDTH,
                 lambda z: _layernorm(jax.nn.gelu(z), lng_ref[...], lnb_ref[...])),
                (QK_LO, QK_LO, lambda z: z * (B_DK ** -0.5)),
                (QK_LO + B_KW, QK_LO + B_KW, lambda z: z)]
    segments += [(VB_LO + j * half, VB_LO + j * half, lambda z: z) for j in range(2)]
    segments += [(GATE_LO + j * half, LR_LO + j * half, jax.nn.sigmoid) for j in range(4)]
    segments += [(GB_LO + j * half, GB_LO + j * half, jax.nn.silu) for j in range(2)]
    for w_lo, p_lo, act in segments:
        put(p_lo, act(proj(w_lo, half)))


def _gla_prep(la, qk, ones_tril):
    la_hi = la.astype(BF16)
    la_lo = (la - la_hi.astype(F32)).astype(BF16)
    b = _dot(jnp.concatenate([ones_tril, ones_tril], axis=1),
             jnp.concatenate([la_hi, la_lo], axis=0))
    r0 = b[HALF // 2 - 1:HALF // 2, :]
    r1 = b[HALF + HALF // 2 - 1:HALF + HALF // 2, :]
    b_mid = b[HALF - 1:HALF, :]
    b_end = b[CHUNK - 1:CHUNK, :]
    bq0 = b[:HALF] - r0
    bq1 = b[HALF:] - r1
    qs = qk[:, :B_KW]
    k = qk[:, B_KW:]
    a0 = qs[:HALF] * jnp.exp(bq0)
    a1 = qs[HALF:] * jnp.exp(bq1)
    k0 = k[:HALF] * jnp.exp(-bq0)
    k1 = k[HALF:] * jnp.exp(-bq1)
    q_off1 = a1 * jnp.exp(r1 - b_mid)
    k_off0 = k0 * jnp.exp(b_mid - r0)
    q_int = jnp.concatenate([a0 * jnp.exp(r0), q_off1 * jnp.exp(b_mid)], axis=0)
    k_st = jnp.concatenate([k_off0 * jnp.exp(b_end - b_mid), k1 * jnp.exp(b_end - r1)], axis=0)
    return dict(a0=a0.astype(BF16), a1=a1.astype(BF16), k0=k0.astype(BF16), k1=k1.astype(BF16),
                q_off1=q_off1.astype(BF16), k_off0=k_off0.astype(BF16), q_int=q_int.astype(BF16),
                k_st=k_st, decay=jnp.exp(b_end))


def _mixer_kernel(*refs, n_chunks, n_dec, n_cast):
    it = iter(refs)

    def take(k):
        return [next(it) for _ in range(k)]

    h_ref, x1_ref, d_h_ref, d_state_ref = take(4)
    proj_w = take(5)
    ws_ref, bsp_ref, ggla_ref, wpa_ref, wpb_ref, wo_ref, ws0_ref, bs0_ref = take(8)
    cast_src = take(n_cast)
    x2_ref, sout_ref, d_ya_ref, d_yb_ref, d_sout_ref, d_cv_ref, d_gate_ref = take(7)
    cast_dst = take(n_cast)
    s_scr, o_scr, p_scr, la_ref, d_p_scr, d_la_ref = take(6)
    t = pl.program_id(1)
    tt = h_ref.shape[0]

    @pl.when(t == 0)
    def _():
        s_scr[...] = jnp.zeros_like(s_scr)

    def put(lo, val):
        p_scr[:, lo:lo + val.shape[1]] = val[:tt].astype(BF16)
        d_p_scr[:, lo:lo + val.shape[1]] = val[tt:tt + n_dec]

    def put_la(val):
        la_ref[...] = val[:tt]
        d_la_ref[...] = val[tt:tt + n_dec]

    _activated_projection(jnp.concatenate([h_ref[...], d_h_ref[0]], axis=0), proj_w, put, put_la)
    uv_ref, qk_ref, v_ref, gb_ref, sga_ref, sgb_ref = (
        p_scr.at[:, j * D_MODEL:(j + 1) * D_MODEL] for j in range(6))
    d_uv_ref, d_qk_ref, d_v_ref, d_gb_ref = (
        d_p_scr.at[:, j * D_MODEL:(j + 1) * D_MODEL] for j in range(4))

    row = lax.broadcasted_iota(jnp.int32, (CHUNK, CHUNK), 0)
    col = lax.broadcasted_iota(jnp.int32, (CHUNK, CHUNK), 1)
    causal = row >= col
    ones_tril = causal.astype(BF16)
    ws_tril = [jnp.where(causal, ws_ref[g], 0.0).astype(BF16) for g in range(A_GROUPS)]
    zero_blk = jnp.zeros((HALF, B_DK), BF16)
    chunks = [slice(c * CHUNK, (c + 1) * CHUNK) for c in range(n_chunks)]
    heads = [(slice(h * B_DK, (h + 1) * B_DK), slice(h * B_DV, (h + 1) * B_DV))
             for h in range(B_HEADS)]

    ya = []
    for rows in chunks:
        u = uv_ref[rows, :A_WIDTH].astype(F32)
        vn = uv_ref[rows, A_WIDTH:]
        parts = []
        for g in range(A_GROUPS):
            cols = slice(g * A_GROUP_DIM, (g + 1) * A_GROUP_DIM)
            mixed = _dot(ws_tril[g], vn[:, cols]) + bsp_ref[:, g:g + 1]
            parts.append((u[:, cols] * mixed).astype(BF16))
        ya.append(jnp.concatenate(parts, axis=1))

    prep = [_gla_prep(la_ref[rows, :], qk_ref[rows, :].astype(F32), ones_tril) for rows in chunks]
    scores = []
    for p in prep:
        per_head = []
        for kc, _ in heads:
            top = _dot_nt(p["a0"][:, kc], jnp.concatenate([p["k0"][:, kc], zero_blk], axis=0))
            bot = _dot_nt(
                jnp.concatenate([p["q_off1"][:, kc], p["a1"][:, kc]], axis=1),
                jnp.concatenate([jnp.concatenate([p["k_off0"][:, kc], zero_blk], axis=1),
                                 jnp.concatenate([zero_blk, p["k1"][:, kc]], axis=1)], axis=0))
            s = jnp.concatenate([top, bot], axis=0)
            per_head.append(jnp.where(causal, s, 0.0).astype(BF16))
        scores.append(per_head)

    state = [s_scr[h] for h in range(B_HEADS)]
    state_before = []
    for rows, p in zip(chunks, prep):
        state_before.append([s.astype(BF16) for s in state])
        for h, (kc, vc) in enumerate(heads):
            upd = _dot(p["k_st"][:, kc].T.astype(BF16), v_ref[rows, vc])
            decay_col = jnp.broadcast_to(p["decay"][:, kc], (B_DK, B_DK)).T
            state[h] = state[h] * jnp.concatenate([decay_col, decay_col], axis=1) + upd
    for h in range(B_HEADS):
        s_scr[h] = state[h]

    yb = []
    for c, (rows, p) in enumerate(zip(chunks, prep)):
        gb = gb_ref[rows, :].astype(F32)
        parts = []
        for h, (kc, vc) in enumerate(heads):
            o = _dot(jnp.concatenate([scores[c][h], p["q_int"][:, kc]], axis=1),
                     jnp.concatenate([v_ref[rows, vc], state_before[c][h]], axis=0))
            parts.append((_rms(o, ggla_ref[...]) * gb[:, vc]).astype(BF16))
        yb.append(jnp.concatenate(parts, axis=1))

    pa = _dot(jnp.concatenate(ya, axis=0), wpa_ref[...])
    _cast_blocks(cast_src, cast_dst)
    _sample_mixers(d_uv_ref, d_qk_ref, d_v_ref, d_gb_ref, d_la_ref, d_state_ref, ws0_ref, bs0_ref,
                   ggla_ref, d_ya_ref.at[0], d_yb_ref.at[0], d_sout_ref, d_cv_ref.at[0], o_scr,
                   n_dec)
    d_gate_ref[0] = d_p_scr[:, LR_LO:]
    mix = (sga_ref[...].astype(F32) * pa
           + sgb_ref[...].astype(F32) * _dot(jnp.concatenate(yb, axis=0), wpb_ref[...]))
    x2_ref[...] = x1_ref[...] + _dot(mix.astype(BF16), wo_ref[...])

    @pl.when(t == pl.num_programs(1) - 1)
    def _():
        sout_ref[0] = s_scr[...]


def _sample_mixers(uv_ref, qk_ref, v_ref, gb_ref, la_ref, s_ref, ws0_ref, bs0_ref, ggla_ref,
                   ya_ref, yb_ref, sout_ref, cv_ref, o_scr, nb):
    vn = uv_ref[:, A_WIDTH:]
    cv_ref[...] = vn
    ya_ref[...] = uv_ref[:, :A_WIDTH] * (vn * ws0_ref[...] + bs0_ref[...])

    a = jnp.exp(la_ref[...])
    qk = qk_ref[...]
    qs = qk[:, :B_KW]
    k = qk[:, B_KW:]
    v = v_ref[...]
    pad = jnp.zeros((LANES - 3 * nb, B_DK), F32)
    for h in range(B_HEADS):
        kc = slice(h * B_DK, (h + 1) * B_DK)
        vc = slice(h * B_DV, (h + 1) * B_DV)
        xt = jnp.concatenate([a[:, kc], k[:, kc], qs[:, kc], pad], axis=0).T
        for n in range(nb):
            s_new = (s_ref[n, h] * xt[:, n:n + 1]
                     + xt[:, nb + n:nb + n + 1] * v[n:n + 1, vc])
            sout_ref[n, h] = s_new
            o_scr[n:n + 1, vc] = jnp.sum(xt[:, 2 * nb + n:2 * nb + n + 1] * s_new,
                                         axis=0, keepdims=True)
    gb = gb_ref[...]
    for h in range(B_HEADS):
        vc = slice(h * B_DV, (h + 1) * B_DV)
        yb_ref[:, vc] = _rms(o_scr[:, vc], ggla_ref[...]) * gb[:, vc]


def _merge_kernel(ya_ref, yb_ref, ga_ref, gb_ref, x1_ref, wpa_ref, wpb_ref, wo_ref, x2_ref):
    mix = (ga_ref[...].astype(F32) * _dot(ya_ref[...], wpa_ref[...])
           + gb_ref[...].astype(F32) * _dot(yb_ref[...], wpb_ref[...]))
    x2_ref[...] = x1_ref[...] + _dot(mix.astype(BF16), wo_ref[...])


def _ffn2_finish(x3, side, weights, outs, *, final_norm):
    (p_ref,), (y_ref,) = side, outs
    gple_ref, wpg_ref, wple_ref, gfin_ref = weights[3:]
    gate = jax.nn.sigmoid(_dot(_rms(x3, gple_ref[...]).astype(BF16), wpg_ref[...]))
    x4 = x3 + _dot(p_ref[...].astype(BF16), wple_ref[...]) * gate
    y_ref[...] = _rms(x4, gfin_ref[...]) if final_norm else x4


def _resident(shape):
    zeros = (0,) * len(shape)
    return pl.BlockSpec(shape, lambda *_: zeros, pipeline_mode=pl.Buffered(1))


def _rows(tm, width, colblk=0):
    return pl.BlockSpec((tm, width), lambda i: (i, colblk))


def _params(n_axes):
    return pltpu.CompilerParams(dimension_semantics=("arbitrary",) * n_axes,
                                vmem_limit_bytes=VMEM_LIMIT)


def _cast_specs(weights, n_steps, linear_step=lambda i: i):
    specs, shapes = [], []
    for w in weights:
        rows, cols = w.shape
        n_blocks = max(n for n in range(1, n_steps + 1) if rows % (n * BF16_SUBLANES) == 0)
        specs.append(pl.BlockSpec(
            (rows // n_blocks, cols),
            lambda *idx, n=n_blocks: (jnp.minimum(linear_step(*idx), n - 1), 0)))
        shapes.append(jax.ShapeDtypeStruct(w.shape, BF16))
    return specs, shapes


def _ffn_stage_call(name, finish, x, x_s, side, side_s, weights, outputs, to_cast=(),
                    own_weights=False):
    m, ms = x.shape[0], x_s.shape[0]
    assert m % TOKEN_TILE == 0
    n = m // TOKEN_TILE
    weight_specs = [_resident(w.shape) for w in weights]
    scratch = [pltpu.VMEM((TOKEN_TILE, D_MODEL), F32)]
    if own_weights:
        w_in, w_out = weights[1], weights[2]
        assert w_in.shape[1] % LOAD_CHUNKS == 0 and w_out.shape[0] % LOAD_CHUNKS == 0
        weight_specs[1] = weight_specs[2] = pl.BlockSpec(memory_space=pl.ANY)
        scratch += [pltpu.VMEM(w_in.shape, BF16), pltpu.VMEM(w_out.shape, BF16),
                    pltpu.VMEM((2, w_in.shape[0], w_in.shape[1] // LOAD_CHUNKS), F32),
                    pltpu.VMEM((2, w_out.shape[0] // LOAD_CHUNKS, w_out.shape[1]), F32),
                    pltpu.SemaphoreType.DMA((2,)), pltpu.SemaphoreType.DMA((2,))]

    def tile(cols, lag):
        return pl.BlockSpec((TOKEN_TILE, cols), lambda i: (jnp.clip(i - lag, 0, n - 1), 0))

    def s_spec(cols):
        return pl.BlockSpec((ms, cols), lambda i: (0, 0))

    cast_specs, cast_shapes = _cast_specs(to_cast, n)
    outs = pl.pallas_call(
        functools.partial(_ffn_stage_kernel, finish=finish, n_side=len(side), n_w=len(weights),
                          n_out=len(outputs), n_cast=len(to_cast), own_weights=own_weights),
        grid=(n + 1,),
        in_specs=[tile(D_MODEL, 0)] + [tile(a.shape[1], 1) for a in side]
                 + [s_spec(D_MODEL)] + [s_spec(a.shape[1]) for a in side_s]
                 + weight_specs + cast_specs,
        out_specs=[tile(c, 1) for c, _ in outputs] + [s_spec(c) for c, _ in outputs] + cast_specs,
        out_shape=[jax.ShapeDtypeStruct((m, c), dt) for c, dt in outputs]
                  + [jax.ShapeDtypeStruct((ms, c), dt) for c, dt in outputs] + cast_shapes,
        scratch_shapes=scratch,
        compiler_params=_params(1),
        name=name,
    )(x, *side, x_s, *side_s, *weights, *to_cast)
    k = len(outputs)
    return outs[:k], outs[k:2 * k], outs[2 * k:]


def _mixer(h, x1, n_seq, seq, d_h, d_state, proj_w, ws, bsp_t, ggla, wpa, wpb, wo, ws0, bs0,
           to_cast=()):
    tt = MIXER_TILE
    nt = seq // tt
    m = n_seq * seq
    steps = n_seq * nt
    n_all = d_h.shape[0]
    assert n_all % steps == 0
    n_dec = n_all // steps
    assert n_dec <= BF16_SUBLANES
    d_h3 = jnp.pad(d_h.reshape(steps, n_dec, D_MODEL), ((0, 0), (0, BF16_SUBLANES - n_dec), (0, 0)))

    def seg(colblk, width=D_MODEL):
        return pl.BlockSpec((tt, width), lambda b, t: (b * nt + t, colblk))

    def d_seg(colblk, width=D_MODEL):
        return pl.BlockSpec((1, n_dec, width), lambda b, t: (b * nt + t, 0, colblk))

    d_state_spec = pl.BlockSpec((n_dec, B_HEADS, B_DK, B_DV), lambda b, t: (b * nt + t, 0, 0, 0))
    d_h_spec = pl.BlockSpec((1, BF16_SUBLANES, D_MODEL), lambda b, t: (b * nt + t, 0, 0))
    small = list(proj_w) + [ws, bsp_t, ggla, wpa, wpb, wo, ws0, bs0]
    cast_specs, cast_shapes = _cast_specs(to_cast, steps, lambda b, t: b * nt + t)
    x2, sp, ya, yb, ss, cv, gates, *cast = pl.pallas_call(
        functools.partial(_mixer_kernel, n_chunks=tt // CHUNK, n_dec=n_dec, n_cast=len(to_cast)),
        grid=(n_seq, nt),
        in_specs=[seg(0), seg(0), d_h_spec, d_state_spec]
                 + [_resident(w.shape) for w in small] + cast_specs,
        out_specs=[seg(0),
                   pl.BlockSpec((1, B_HEADS, B_DK, B_DV), lambda b, t: (b, 0, 0, 0)),
                   d_seg(0, A_WIDTH), d_seg(0, B_VW), d_state_spec, d_seg(0, A_WIDTH),
                   d_seg(0, 2 * D_MODEL)] + cast_specs,
        out_shape=[jax.ShapeDtypeStruct((m, D_MODEL), F32),
                   jax.ShapeDtypeStruct((n_seq, B_HEADS, B_DK, B_DV), F32),
                   jax.ShapeDtypeStruct((steps, n_dec, A_WIDTH), F32),
                   jax.ShapeDtypeStruct((steps, n_dec, B_VW), F32),
                   jax.ShapeDtypeStruct(d_state.shape, F32),
                   jax.ShapeDtypeStruct((steps, n_dec, A_WIDTH), F32),
                   jax.ShapeDtypeStruct((steps, n_dec, 2 * D_MODEL), F32)] + cast_shapes,
        scratch_shapes=[pltpu.VMEM((B_HEADS, B_DK, B_DV), F32), pltpu.VMEM((n_dec, B_VW), F32),
                        pltpu.VMEM((tt, P_COLS), BF16), pltpu.VMEM((tt, B_KW), F32),
                        pltpu.VMEM((n_dec, P_COLS), F32), pltpu.VMEM((n_dec, B_KW), F32)],
        compiler_params=_params(2),
        name="mixer",
    )(h, x1, d_h3, d_state, *small, *to_cast)
    return (x2, sp, ya.reshape(n_all, A_WIDTH), yb.reshape(n_all, B_VW), ss,
            cv.reshape(n_all, A_WIDTH), gates.reshape(n_all, 2 * D_MODEL), cast)


def _merge(ya, yb, gates, x1, wpa, wpb, wo):
    tm = m = x1.shape[0]
    return pl.pallas_call(
        _merge_kernel,
        grid=(m // tm,),
        in_specs=[_rows(tm, A_WIDTH), _rows(tm, B_VW), _rows(tm, D_MODEL, 0),
                  _rows(tm, D_MODEL, 1), _rows(tm, D_MODEL), _resident(wpa.shape),
                  _resident(wpb.shape), _resident(wo.shape)],
        out_specs=_rows(tm, D_MODEL),
        out_shape=jax.ShapeDtypeStruct((m, D_MODEL), F32),
        compiler_params=_params(1),
        name="merge",
    )(ya, yb, gates, gates, x1, wpa, wpb, wo)


def kernel(x_prompt, x_sample, p_prompt, p_sample, state_gla, g_ffn1, w_ffn1_in, w_ffn1_out, g_mix,
           w_in, ln_v_g, ln_v_b, w_spatial, b_spatial, w_gate_up, b_gate, g_gla_out, w_proj_a,
           w_proj_b, w_out, g_ffn2, w_ffn2_in, w_ffn2_out, g_ple, w_ple_gate, w_ple, g_final):
    depth = w_in.shape[0]
    n_seq, seq, _ = x_prompt.shape
    n_dec, dec_seq, _ = x_sample.shape
    assert dec_seq == 1 and seq % MIXER_TILE == 0

    def row(vec):
        return vec.reshape(1, -1).astype(F32)

    assert sum(IN_SIZES[:6]) == LR_LO and sum(IN_SIZES) == GATE_LO + 2 * D_MODEL
    xp = x_prompt.reshape(n_seq * seq, D_MODEL)
    xs = x_sample.reshape(n_dec, D_MODEL)
    gfin = row(g_final)
    sp_list, ss_list, vs_list = [], [], []
    for i in range(depth):
        wgu =jnp.pad(w_gate_up[i], ((0, LANES - GATE_RANK), (0, 0))).astype(BF16)
        g1, gmix, g2, gple = row(g_ffn1[i]), row(g_mix[i]), row(g_ffn2[i]), row(g_ple[i])
        lng, lnb, bg, ggla = row(ln_v_g[i]), row(ln_v_b[i]), row(b_gate[i]), row(g_gla_out[i])
        ws0 = row(jnp.repeat(w_spatial[i][:, 0, 0], A_GROUP_DIM))
        bs0 = row(jnp.repeat(b_spatial[i][:, 0], A_GROUP_DIM))

        wide, act = (D_MODEL, F32), (D_MODEL, BF16)
        (x1p, hp), (x1s, hs), (w_inb, wpa, wpb, wo) = _ffn_stage_call(
            "ffn1", _ffn1_finish, xp, xs, [], [], [g1, w_ffn1_in[i], w_ffn1_out[i], gmix],
            [wide, act], to_cast=[w_in[i].T, w_proj_a[i], w_proj_b[i], w_out[i]], own_weights=True)
        x2p, sp, ya, yb, ss, cv, gates, (w2i, w2o, wpg, wple) = _mixer(
            hp, x1p, n_seq, seq, hs, state_gla[i], [w_inb, wgu, bg, lng, lnb], w_spatial[i],
            b_spatial[i].T, ggla, wpa, wpb, wo, ws0, bs0,
            to_cast=[w_ffn2_in[i], w_ffn2_out[i], w_ple_gate[i], w_ple[i]])
        x2s = _merge(ya.astype(BF16), yb.astype(BF16), gates, x1s, wpa, wpb, wo)

        ple_p = p_prompt[i].reshape(n_seq * seq, PLE_DIM)
        ple_s = p_sample[i].reshape(n_dec, PLE_DIM)
        (xp,), (xs,), _ = _ffn_stage_call(
            "ffn2", functools.partial(_ffn2_finish, final_norm=i == depth - 1), x2p, x2s,
            [ple_p], [ple_s], [g2, w2i, w2o, gple, wpg, wple, gfin], [wide])

        sp_list.append(sp)
        ss_list.append(ss)
        vs_list.append(cv.reshape(n_dec, dec_seq, A_WIDTH))
    return (xp.reshape(n_seq, seq, D_MODEL), xs.reshape(n_dec, dec_seq, D_MODEL),
            jnp.stack(sp_list), jnp.stack(ss_list), jnp.stack(vs_list))
```

```python
import functools

import jax
import jax.numpy as jnp
from jax import lax
from jax.experimental import pallas as pl
from jax.experimental.pallas import tpu as pltpu

D_MODEL = 1024
D_FF = 2816
PLE_DIM = 256
CHUNK = 128
A_GROUPS = 4
A_GROUP_DIM = 128
A_WIDTH = A_GROUPS * A_GROUP_DIM
B_HEADS = 4
B_DK = 128
B_DV = 256
B_KW = B_HEADS * B_DK
B_VW = B_HEADS * B_DV
GATE_RANK = 16
GATE_NORM = 16.0
EPS = 1e-6
IN_SIZES = (A_WIDTH, A_WIDTH, B_KW, B_KW, B_VW, B_VW, GATE_RANK, D_MODEL, D_MODEL)

LANES = 128
P_COLS = 6 * D_MODEL
QK_LO = 2 * A_WIDTH
VB_LO = QK_LO + 2 * B_KW
GB_LO = VB_LO + B_VW
LR_LO = GB_LO + B_VW
GATE_LO = LR_LO + GATE_RANK
FF_CHUNK = 256
LOAD_ROWS = 128
HALF = CHUNK // 2
MIXER_TILE = 4 * CHUNK
TOKEN_TILE = 512
BF16_SUBLANES = 16
VMEM_LIMIT = 56 * 1024 * 1024

F32 = jnp.float32
BF16 = jnp.bfloat16


def _dot(a, b):
    return jnp.dot(a, b, preferred_element_type=F32)


def _dot_nt(a, b):
    return lax.dot_general(a, b, (((1,), (1,)), ((), ())), preferred_element_type=F32)


def _rms(x, g):
    return x * lax.rsqrt(jnp.mean(x * x, axis=-1, keepdims=True) + EPS) * g


def _log_sigmoid(x):
    return jnp.minimum(x, 0.0) - jnp.log1p(jnp.exp(-jnp.abs(x)))


def _layernorm(x, g, b):
    mu = jnp.mean(x, axis=-1, keepdims=True)
    xc = x - mu
    var = jnp.mean(xc * xc, axis=-1, keepdims=True)
    return xc * lax.rsqrt(var + EPS) * g + b


def _swiglu_residual(x, g_ref, w_in_ref, w_out_ref, early):
    h = _rms(x, g_ref[...]).astype(BF16)
    acts = []
    for ci, lo in enumerate(range(0, D_FF, FF_CHUNK)):
        gate = _dot(h, w_in_ref[:, lo:lo + FF_CHUNK])
        up = _dot(h, w_in_ref[:, D_FF + lo:D_FF + lo + FF_CHUNK])
        acts.append((jax.nn.silu(gate) * up).astype(BF16))
        if ci == 1:
            early()
    return x + 0.5 * _dot(jnp.concatenate(acts, axis=1), w_out_ref[...])


def _cast_blocks(src_refs, dst_refs):
    for src, dst in zip(src_refs, dst_refs):
        dst[...] = src[...].astype(dst.dtype)


def _load_as_bf16(src_hbm, dst, stage, sem):
    size = stage.shape[1]
    n_pieces = src_hbm.shape[0] // size

    def copy(c):
        return pltpu.make_async_copy(src_hbm.at[pl.ds(c * size, size), :], stage.at[c % 2],
                                     sem.at[c % 2])

    copy(0).start()
    for c in range(n_pieces):
        if c + 1 < n_pieces:
            copy(c + 1).start()
        copy(c).wait()
        dst[c * size:(c + 1) * size, :] = stage[c % 2].astype(BF16)


def _ffn_stage_kernel(*refs, finish, n_side, n_w, n_out, n_cast, own_weights):
    it = iter(refs)

    def take(k):
        return [next(it) for _ in range(k)]

    (x_ref,), p_side, (xs_ref,), s_side = take(1), take(n_side), take(1), take(n_side)
    weights, cast_src = take(n_w), take(n_cast)
    p_out, s_out, cast_dst, (carry,) = take(n_out), take(n_out), take(n_cast), take(1)
    _cast_blocks(cast_src, cast_dst)
    step = pl.program_id(0)
    sample_step = pl.num_programs(0) - 1
    if own_weights:
        w_in_scr, w_out_scr, stage_in, stage_out, sem_in, sem_out = take(6)
        w_in_hbm, w_out_hbm = weights[1], weights[2]
        weights = [weights[0], w_in_scr, w_out_scr] + weights[3:]

        @pl.when(step == 0)
        def _():
            _load_as_bf16(w_in_hbm, w_in_scr, stage_in, sem_in)
            _load_as_bf16(w_out_hbm, w_out_scr, stage_out, sem_out)

    def finish_previous_tile():
        finish(carry[...], p_side, weights, p_out)

    @pl.when(step == 0)
    def _():
        carry[...] = jnp.zeros_like(carry)

    @pl.when(step < sample_step)
    def _():
        carry[...] = _swiglu_residual(x_ref[...], *weights[:3], finish_previous_tile)

    @pl.when(step == sample_step)
    def _():
        x_new = _swiglu_residual(xs_ref[...], *weights[:3], finish_previous_tile)
        finish(x_new, s_side, weights, s_out)


def _ffn1_finish(x1, side, weights, outs):
    gmix_ref, (x1_ref, h_ref) = weights[3], outs
    x1_ref[...] = x1
    h_ref[...] = _rms(x1, gmix_ref[...]).astype(BF16)


def _activated_projection(h, weights, put, put_la):
    wt_ref, wgu_ref, bg_ref, lng_ref, lnb_ref = weights

    def proj(lo, width):
        return _dot_nt(h, wt_ref[lo:lo + width, :])

    lr = proj(LR_LO, LANES).astype(BF16)
    logit = _dot(lr, wgu_ref[...]) + bg_ref[...]
    put_la(_log_sigmoid(logit) * (1.0 / GATE_NORM))
    half = D_MODEL // 2
    segments = [(0, 0, jax.nn.gelu),
                (A_WIDTH, A_WIDTH,
                 lambda z: _layernorm(jax.nn.gelu(z), lng_ref[...], lnb_ref[...])),
                (QK_LO, QK_LO, lambda z: z * (B_DK ** -0.5)),
                (QK_LO + B_KW, QK_LO + B_KW, lambda z: z)]
    segments += [(VB_LO + j * half, VB_LO + j * half, lambda z: z) for j in range(2)]
    segments += [(GATE_LO + j * half, LR_LO + j * half, jax.nn.sigmoid) for j in range(4)]
    segments += [(GB_LO + j * half, GB_LO + j * half, jax.nn.silu) for j in range(2)]
    for w_lo, p_lo, act in segments:
        put(p_lo, act(proj(w_lo, half)))


def _gla_prep(la, qk, ones_tril):
    la_hi = la.astype(BF16)
    la_lo = (la - la_hi.astype(F32)).astype(BF16)
    b = _dot(jnp.concatenate([ones_tril, ones_tril], axis=1),
             jnp.concatenate([la_hi, la_lo], axis=0))
    r0 = b[HALF // 2 - 1:HALF // 2, :]
    r1 = b[HALF + HALF // 2 - 1:HALF + HALF // 2, :]
    b_mid = b[HALF - 1:HALF, :]
    b_end = b[CHUNK - 1:CHUNK, :]
    bq0 = b[:HALF] - r0
    bq1 = b[HALF:] - r1
    qs = qk[:, :B_KW]
    k = qk[:, B_KW:]
    a0 = qs[:HALF] * jnp.exp(bq0)
    a1 = qs[HALF:] * jnp.exp(bq1)
    k0 = k[:HALF] * jnp.exp(-bq0)
    k1 = k[HALF:] * jnp.exp(-bq1)
    q_off1 = a1 * jnp.exp(r1 - b_mid)
    k_off0 = k0 * jnp.exp(b_mid - r0)
    q_int = jnp.concatenate([a0 * jnp.exp(r0), q_off1 * jnp.exp(b_mid)], axis=0)
    k_st = jnp.concatenate([k_off0 * jnp.exp(b_end - b_mid), k1 * jnp.exp(b_end - r1)], axis=0)
    return dict(a0=a0.astype(BF16), a1=a1.astype(BF16), k0=k0.astype(BF16), k1=k1.astype(BF16),
                q_off1=q_off1.astype(BF16), k_off0=k_off0.astype(BF16), q_int=q_int.astype(BF16),
                k_st=k_st, decay=jnp.exp(b_end))


def _mixer_kernel(*refs, n_chunks, n_dec, n_cast):
    it = iter(refs)

    def take(k):
        return [next(it) for _ in range(k)]

    h_ref, x1_ref, d_h_ref, d_state_ref = take(4)
    proj_w = take(5)
    ws_ref, bsp_ref, ggla_ref, wpa_ref, wpb_ref, wo_ref, ws0_ref, bs0_ref = take(8)
    cast_src = take(n_cast)
    x2_ref, sout_ref, d_ya_ref, d_yb_ref, d_sout_ref, d_cv_ref, d_gate_ref = take(7)
    cast_dst = take(n_cast)
    s_scr, o_scr, p_scr, la_ref, d_p_scr, d_la_ref = take(6)
    t = pl.program_id(1)
    tt = h_ref.shape[0]

    @pl.when(t == 0)
    def _():
        s_scr[...] = jnp.zeros_like(s_scr)

    def put(lo, val):
        p_scr[:, lo:lo + val.shape[1]] = val[:tt].astype(BF16)
        d_p_scr[:, lo:lo + val.shape[1]] = val[tt:tt + n_dec]

    def put_la(val):
        la_ref[...] = val[:tt]
        d_la_ref[...] = val[tt:tt + n_dec]

    _activated_projection(jnp.concatenate([h_ref[...], d_h_ref[0]], axis=0), proj_w, put, put_la)
    uv_ref, qk_ref, v_ref, gb_ref, sga_ref, sgb_ref = (
        p_scr.at[:, j * D_MODEL:(j + 1) * D_MODEL] for j in range(6))
    d_uv_ref, d_qk_ref, d_v_ref, d_gb_ref = (
        d_p_scr.at[:, j * D_MODEL:(j + 1) * D_MODEL] for j in range(4))

    row = lax.broadcasted_iota(jnp.int32, (CHUNK, CHUNK), 0)
    col = lax.broadcasted_iota(jnp.int32, (CHUNK, CHUNK), 1)
    causal = row >= col
    ones_tril = causal.astype(BF16)
    ws_tril = [jnp.where(causal, ws_ref[g], 0.0).astype(BF16) for g in range(A_GROUPS)]
    zero_blk = jnp.zeros((HALF, B_DK), BF16)
    chunks = [slice(c * CHUNK, (c + 1) * CHUNK) for c in range(n_chunks)]
    heads = [(slice(h * B_DK, (h + 1) * B_DK), slice(h * B_DV, (h + 1) * B_DV))
             for h in range(B_HEADS)]

    ya = []
    for rows in chunks:
        u = uv_ref[rows, :A_WIDTH].astype(F32)
        vn = uv_ref[rows, A_WIDTH:]
        parts = []
        for g in range(A_GROUPS):
            cols = slice(g * A_GROUP_DIM, (g + 1) * A_GROUP_DIM)
            mixed = _dot(ws_tril[g], vn[:, cols]) + bsp_ref[:, g:g + 1]
            parts.append((u[:, cols] * mixed).astype(BF16))
        ya.append(jnp.concatenate(parts, axis=1))

    prep = [_gla_prep(la_ref[rows, :], qk_ref[rows, :].astype(F32), ones_tril) for rows in chunks]
    scores = []
    for p in prep:
        per_head = []
        for kc, _ in heads:
            top = _dot_nt(p["a0"][:, kc], jnp.concatenate([p["k0"][:, kc], zero_blk], axis=0))
            bot = _dot_nt(
                jnp.concatenate([p["q_off1"][:, kc], p["a1"][:, kc]], axis=1),
                jnp.concatenate([jnp.concatenate([p["k_off0"][:, kc], zero_blk], axis=1),
                                 jnp.concatenate([zero_blk, p["k1"][:, kc]], axis=1)], axis=0))
            s = jnp.concatenate([top, bot], axis=0)
            per_head.append(jnp.where(causal, s, 0.0).astype(BF16))
        scores.append(per_head)

    state = [s_scr[h] for h in range(B_HEADS)]
    state_before = []
    for rows, p in zip(chunks, prep):
        state_before.append([s.astype(BF16) for s in state])
        for h, (kc, vc) in enumerate(heads):
            upd = _dot(p["k_st"][:, kc].T.astype(BF16), v_ref[rows, vc])
            decay_col = jnp.broadcast_to(p["decay"][:, kc], (B_DK, B_DK)).T
            state[h] = state[h] * jnp.concatenate([decay_col, decay_col], axis=1) + upd
    for h in range(B_HEADS):
        s_scr[h] = state[h]

    yb = []
    for c, (rows, p) in enumerate(zip(chunks, prep)):
        gb = gb_ref[rows, :].astype(F32)
        parts = []
        for h, (kc, vc) in enumerate(heads):
            o = _dot(jnp.concatenate([scores[c][h], p["q_int"][:, kc]], axis=1),
                     jnp.concatenate([v_ref[rows, vc], state_before[c][h]], axis=0))
            parts.append((_rms(o, ggla_ref[...]) * gb[:, vc]).astype(BF16))
        yb.append(jnp.concatenate(parts, axis=1))

    pa = _dot(jnp.concatenate(ya, axis=0), wpa_ref[...])
    _cast_blocks(cast_src, cast_dst)
    _sample_mixers(d_uv_ref, d_qk_ref, d_v_ref, d_gb_ref, d_la_ref, d_state_ref, ws0_ref, bs0_ref,
                   ggla_ref, d_ya_ref.at[0], d_yb_ref.at[0], d_sout_ref, d_cv_ref.at[0], o_scr,
                   n_dec)
    d_gate_ref[0] = d_p_scr[:, LR_LO:]
    mix = (sga_ref[...].astype(F32) * pa
           + sgb_ref[...].astype(F32) * _dot(jnp.concatenate(yb, axis=0), wpb_ref[...]))
    x2_ref[...] = x1_ref[...] + _dot(mix.astype(BF16), wo_ref[...])

    @pl.when(t == pl.num_programs(1) - 1)
    def _():
        sout_ref[0] = s_scr[...]


def _sample_mixers(uv_ref, qk_ref, v_ref, gb_ref, la_ref, s_ref, ws0_ref, bs0_ref, ggla_ref,
                   ya_ref, yb_ref, sout_ref, cv_ref, o_scr, nb):
    vn = uv_ref[:, A_WIDTH:]
    cv_ref[...] = vn
    ya_ref[...] = uv_ref[:, :A_WIDTH] * (vn * ws0_ref[...] + bs0_ref[...])

    a = jnp.exp(la_ref[...])
    qk = qk_ref[...]
    qs = qk[:, :B_KW]
    k = qk[:, B_KW:]
    v = v_ref[...]
    pad = jnp.zeros((LANES - 3 * nb, B_DK), F32)
    for h in range(B_HEADS):
        kc = slice(h * B_DK, (h + 1) * B_DK)
        vc = slice(h * B_DV, (h + 1) * B_DV)
        xt = jnp.concatenate([a[:, kc], k[:, kc], qs[:, kc], pad], axis=0).T
        for n in range(nb):
            s_new = (s_ref[n, h] * xt[:, n:n + 1]
                     + xt[:, nb + n:nb + n + 1] * v[n:n + 1, vc])
            sout_ref[n, h] = s_new
            o_scr[n:n + 1, vc] = jnp.sum(xt[:, 2 * nb + n:2 * nb + n + 1] * s_new,
                                         axis=0, keepdims=True)
    gb = gb_ref[...]
    for h in range(B_HEADS):
        vc = slice(h * B_DV, (h + 1) * B_DV)
        yb_ref[:, vc] = _rms(o_scr[:, vc], ggla_ref[...]) * gb[:, vc]


def _merge_kernel(ya_ref, yb_ref, ga_ref, gb_ref, x1_ref, wpa_ref, wpb_ref, wo_ref, x2_ref):
    mix = (ga_ref[...].astype(F32) * _dot(ya_ref[...], wpa_ref[...])
           + gb_ref[...].astype(F32) * _dot(yb_ref[...], wpb_ref[...]))
    x2_ref[...] = x1_ref[...] + _dot(mix.astype(BF16), wo_ref[...])


def _ffn2_finish(x3, side, weights, outs, *, final_norm):
    (p_ref,), (y_ref,) = side, outs
    gple_ref, wpg_ref, wple_ref, gfin_ref = weights[3:]
    gate = jax.nn.sigmoid(_dot(_rms(x3, gple_ref[...]).astype(BF16), wpg_ref[...]))
    x4 = x3 + _dot(p_ref[...].astype(BF16), wple_ref[...]) * gate
    y_ref[...] = _rms(x4, gfin_ref[...]) if final_norm else x4


def _resident(shape):
    zeros = (0,) * len(shape)
    return pl.BlockSpec(shape, lambda *_: zeros, pipeline_mode=pl.Buffered(1))


def _rows(tm, width, colblk=0):
    return pl.BlockSpec((tm, width), lambda i: (i, colblk))


def _params(n_axes):
    return pltpu.CompilerParams(dimension_semantics=("arbitrary",) * n_axes,
                                vmem_limit_bytes=VMEM_LIMIT)


def _cast_specs(weights, n_steps, linear_step=lambda i: i):
    specs, shapes = [], []
    for w in weights:
        rows, cols = w.shape
        n_blocks = max(n for n in range(1, n_steps + 1) if rows % (n * BF16_SUBLANES) == 0)
        specs.append(pl.BlockSpec(
            (rows // n_blocks, cols),
            lambda *idx, n=n_blocks: (jnp.minimum(linear_step(*idx), n - 1), 0)))
        shapes.append(jax.ShapeDtypeStruct(w.shape, BF16))
    return specs, shapes


def _ffn_stage_call(name, finish, x, x_s, side, side_s, weights, outputs, to_cast=(),
                    own_weights=False):
    m, ms = x.shape[0], x_s.shape[0]
    assert m % TOKEN_TILE == 0
    n = m // TOKEN_TILE
    weight_specs = [_resident(w.shape) for w in weights]
    scratch = [pltpu.VMEM((TOKEN_TILE, D_MODEL), F32)]
    if own_weights:
        w_in, w_out = weights[1], weights[2]
        assert w_in.shape[0] % LOAD_ROWS == 0 and w_out.shape[0] % LOAD_ROWS == 0
        weight_specs[1] = weight_specs[2] = pl.BlockSpec(memory_space=pl.ANY)
        scratch += [pltpu.VMEM(w_in.shape, BF16), pltpu.VMEM(w_out.shape, BF16),
                    pltpu.VMEM((2, LOAD_ROWS, w_in.shape[1]), F32),
                    pltpu.VMEM((2, LOAD_ROWS, w_out.shape[1]), F32),
                    pltpu.SemaphoreType.DMA((2,)), pltpu.SemaphoreType.DMA((2,))]

    def tile(cols, lag):
        return pl.BlockSpec((TOKEN_TILE, cols), lambda i: (jnp.clip(i - lag, 0, n - 1), 0))

    def s_spec(cols):
        return pl.BlockSpec((ms, cols), lambda i: (0, 0))

    cast_specs, cast_shapes = _cast_specs(to_cast, n)
    outs = pl.pallas_call(
        functools.partial(_ffn_stage_kernel, finish=finish, n_side=len(side), n_w=len(weights),
                          n_out=len(outputs), n_cast=len(to_cast), own_weights=own_weights),
        grid=(n + 1,),
        in_specs=[tile(D_MODEL, 0)] + [tile(a.shape[1], 1) for a in side]
                 + [s_spec(D_MODEL)] + [s_spec(a.shape[1]) for a in side_s]
                 + weight_specs + cast_specs,
        out_specs=[tile(c, 1) for c, _ in outputs] + [s_spec(c) for c, _ in outputs] + cast_specs,
        out_shape=[jax.ShapeDtypeStruct((m, c), dt) for c, dt in outputs]
                  + [jax.ShapeDtypeStruct((ms, c), dt) for c, dt in outputs] + cast_shapes,
        scratch_shapes=scratch,
        compiler_params=_params(1),
        name=name,
    )(x, *side, x_s, *side_s, *weights, *to_cast)
    k = len(outputs)
    return outs[:k], outs[k:2 * k], outs[2 * k:]


def _mixer(h, x1, n_seq, seq, d_h, d_state, proj_w, ws, bsp_t, ggla, wpa, wpb, wo, ws0, bs0,
           to_cast=()):
    tt = MIXER_TILE
    nt = seq // tt
    m = n_seq * seq
    steps = n_seq * nt
    n_all = d_h.shape[0]
    assert n_all % steps == 0
    n_dec = n_all // steps
    assert n_dec <= BF16_SUBLANES
    d_h3 = jnp.pad(d_h.reshape(steps, n_dec, D_MODEL), ((0, 0), (0, BF16_SUBLANES - n_dec), (0, 0)))

    def seg(colblk, width=D_MODEL):
        return pl.BlockSpec((tt, width), lambda b, t: (b * nt + t, colblk))

    def d_seg(colblk, width=D_MODEL):
        return pl.BlockSpec((1, n_dec, width), lambda b, t: (b * nt + t, 0, colblk))

    d_state_spec = pl.BlockSpec((n_dec, B_HEADS, B_DK, B_DV), lambda b, t: (b * nt + t, 0, 0, 0))
    d_h_spec = pl.BlockSpec((1, BF16_SUBLANES, D_MODEL), lambda b, t: (b * nt + t, 0, 0))
    small = list(proj_w) + [ws, bsp_t, ggla, wpa, wpb, wo, ws0, bs0]
    cast_specs, cast_shapes = _cast_specs(to_cast, steps, lambda b, t: b * nt + t)
    x2, sp, ya, yb, ss, cv, gates, *cast = pl.pallas_call(
        functools.partial(_mixer_kernel, n_chunks=tt // CHUNK, n_dec=n_dec, n_cast=len(to_cast)),
        grid=(n_seq, nt),
        in_specs=[seg(0), seg(0), d_h_spec, d_state_spec]
                 + [_resident(w.shape) for w in small] + cast_specs,
        out_specs=[seg(0),
                   pl.BlockSpec((1, B_HEADS, B_DK, B_DV), lambda b, t: (b, 0, 0, 0)),
                   d_seg(0, A_WIDTH), d_seg(0, B_VW), d_state_spec, d_seg(0, A_WIDTH),
                   d_seg(0, 2 * D_MODEL)] + cast_specs,
        out_shape=[jax.ShapeDtypeStruct((m, D_MODEL), F32),
                   jax.ShapeDtypeStruct((n_seq, B_HEADS, B_DK, B_DV), F32),
                   jax.ShapeDtypeStruct((steps, n_dec, A_WIDTH), F32),
                   jax.ShapeDtypeStruct((steps, n_dec, B_VW), F32),
                   jax.ShapeDtypeStruct(d_state.shape, F32),
                   jax.ShapeDtypeStruct((steps, n_dec, A_WIDTH), F32),
                   jax.ShapeDtypeStruct((steps, n_dec, 2 * D_MODEL), F32)] + cast_shapes,
        scratch_shapes=[pltpu.VMEM((B_HEADS, B_DK, B_DV), F32), pltpu.VMEM((n_dec, B_VW), F32),
                        pltpu.VMEM((tt, P_COLS), BF16), pltpu.VMEM((tt, B_KW), F32),
                        pltpu.VMEM((n_dec, P_COLS), F32), pltpu.VMEM((n_dec, B_KW), F32)],
        compiler_params=_params(2),
        name="mixer",
    )(h, x1, d_h3, d_state, *small, *to_cast)
    return (x2, sp, ya.reshape(n_all, A_WIDTH), yb.reshape(n_all, B_VW), ss,
            cv.reshape(n_all, A_WIDTH), gates.reshape(n_all, 2 * D_MODEL), cast)


def _merge(ya, yb, gates, x1, wpa, wpb, wo):
    tm = m = x1.shape[0]
    return pl.pallas_call(
        _merge_kernel,
        grid=(m // tm,),
        in_specs=[_rows(tm, A_WIDTH), _rows(tm, B_VW), _rows(tm, D_MODEL, 0),
                  _rows(tm, D_MODEL, 1), _rows(tm, D_MODEL), _resident(wpa.shape),
                  _resident(wpb.shape), _resident(wo.shape)],
        out_specs=_rows(tm, D_MODEL),
        out_shape=jax.ShapeDtypeStruct((m, D_MODEL), F32),
        compiler_params=_params(1),
        name="merge",
    )(ya, yb, gates, gates, x1, wpa, wpb, wo)


def kernel(x_prompt, x_sample, p_prompt, p_sample, state_gla, g_ffn1, w_ffn1_in, w_ffn1_out, g_mix,
           w_in, ln_v_g, ln_v_b, w_spatial, b_spatial, w_gate_up, b_gate, g_gla_out, w_proj_a,
           w_proj_b, w_out, g_ffn2, w_ffn2_in, w_ffn2_out, g_ple, w_ple_gate, w_ple, g_final):
    depth = w_in.shape[0]
    n_seq, seq, _ = x_prompt.shape
    n_dec, dec_seq, _ = x_sample.shape
    assert dec_seq == 1 and seq % MIXER_TILE == 0

    def row(vec):
        return vec.reshape(1, -1).astype(F32)

    assert sum(IN_SIZES[:6]) == LR_LO and sum(IN_SIZES) == GATE_LO + 2 * D_MODEL
    xp = x_prompt.reshape(n_seq * seq, D_MODEL)
    xs = x_sample.reshape(n_dec, D_MODEL)
    gfin = row(g_final)
    sp_list, ss_list, vs_list = [], [], []
    for i in range(depth):
        wgu =jnp.pad(w_gate_up[i], ((0, LANES - GATE_RANK), (0, 0))).astype(BF16)
        g1, gmix, g2, gple = row(g_ffn1[i]), row(g_mix[i]), row(g_ffn2[i]), row(g_ple[i])
        lng, lnb, bg, ggla = row(ln_v_g[i]), row(ln_v_b[i]), row(b_gate[i]), row(g_gla_out[i])
        ws0 = row(jnp.repeat(w_spatial[i][:, 0, 0], A_GROUP_DIM))
        bs0 = row(jnp.repeat(b_spatial[i][:, 0], A_GROUP_DIM))

        wide, act = (D_MODEL, F32), (D_MODEL, BF16)
        (x1p, hp), (x1s, hs), (w_inb, wpa, wpb, wo) = _ffn_stage_call(
            "ffn1", _ffn1_finish, xp, xs, [], [], [g1, w_ffn1_in[i], w_ffn1_out[i], gmix],
            [wide, act], to_cast=[w_in[i].T, w_proj_a[i], w_proj_b[i], w_out[i]], own_weights=True)
        x2p, sp, ya, yb, ss, cv, gates, (w2i, w2o, wpg, wple) = _mixer(
            hp, x1p, n_seq, seq, hs, state_gla[i], [w_inb, wgu, bg, lng, lnb], w_spatial[i],
            b_spatial[i].T, ggla, wpa, wpb, wo, ws0, bs0,
            to_cast=[w_ffn2_in[i], w_ffn2_out[i], w_ple_gate[i], w_ple[i]])
        x2s = _merge(ya.astype(BF16), yb.astype(BF16), gates, x1s, wpa, wpb, wo)

        ple_p = p_prompt[i].reshape(n_seq * seq, PLE_DIM)
        ple_s = p_sample[i].reshape(n_dec, PLE_DIM)
        (xp,), (xs,), _ = _ffn_stage_call(
            "ffn2", functools.partial(_ffn2_finish, final_norm=i == depth - 1), x2p, x2s,
            [ple_p], [ple_s], [g2, w2i, w2o, gple, wpg, wple, gfin], [wide])

        sp_list.append(sp)
        ss_list.append(ss)
        vs_list.append(cv.reshape(n_dec, dec_seq, A_WIDTH))
    return (xp.reshape(n_seq, seq, D_MODEL), xs.reshape(n_dec, dec_seq, D_MODEL),
            jnp.stack(sp_list), jnp.stack(ss_list), jnp.stack(vs_list))
```

```python
import functools

import jax
import jax.numpy as jnp
from jax import lax
from jax.experimental import pallas as pl
from jax.experimental.pallas import tpu as pltpu

D_MODEL = 1024
D_FF = 2816
PLE_DIM = 256
CHUNK = 128
A_GROUPS = 4
A_GROUP_DIM = 128
A_WIDTH = A_GROUPS * A_GROUP_DIM
B_HEADS = 4
B_DK = 128
B_DV = 256
B_KW = B_HEADS * B_DK
B_VW = B_HEADS * B_DV
GATE_RANK = 16
GATE_NORM = 16.0
EPS = 1e-6
IN_SIZES = (A_WIDTH, A_WIDTH, B_KW, B_KW, B_VW, B_VW, GATE_RANK, D_MODEL, D_MODEL)

LANES = 128
P_COLS = 6 * D_MODEL
QK_LO = 2 * A_WIDTH
VB_LO = QK_LO + 2 * B_KW
GB_LO = VB_LO + B_VW
LR_LO = GB_LO + B_VW
GATE_LO = LR_LO + GATE_RANK
FF_CHUNK = 256
LOAD_ROWS = 128
HALF = CHUNK // 2
MIXER_TILE = 4 * CHUNK
TOKEN_TILE = 512
BF16_SUBLANES = 16
VMEM_LIMIT = 56 * 1024 * 1024

F32 = jnp.float32
BF16 = jnp.bfloat16


def _dot(a, b):
    return jnp.dot(a, b, preferred_element_type=F32)


def _dot_nt(a, b):
    return lax.dot_general(a, b, (((1,), (1,)), ((), ())), preferred_element_type=F32)


def _rms(x, g):
    return x * lax.rsqrt(jnp.mean(x * x, axis=-1, keepdims=True) + EPS) * g


def _log_sigmoid(x):
    return jnp.minimum(x, 0.0) - jnp.log1p(jnp.exp(-jnp.abs(x)))


def _layernorm(x, g, b):
    mu = jnp.mean(x, axis=-1, keepdims=True)
    xc = x - mu
    var = jnp.mean(xc * xc, axis=-1, keepdims=True)
    return xc * lax.rsqrt(var + EPS) * g + b


def _swiglu_residual(x, g_ref, w_in_ref, w_out_ref, early):
    h = _rms(x, g_ref[...]).astype(BF16)
    acts = []
    for ci, lo in enumerate(range(0, D_FF, FF_CHUNK)):
        gate = _dot(h, w_in_ref[:, lo:lo + FF_CHUNK])
        up = _dot(h, w_in_ref[:, D_FF + lo:D_FF + lo + FF_CHUNK])
        acts.append((jax.nn.silu(gate) * up).astype(BF16))
        if ci == 1:
            early()
    return x + 0.5 * _dot(jnp.concatenate(acts, axis=1), w_out_ref[...])


def _cast_blocks(src_refs, dst_refs):
    for src, dst in zip(src_refs, dst_refs):
        dst[...] = src[...].astype(dst.dtype)


def _load_as_bf16(src_hbm, dst, stage, sem):
    size = stage.shape[1]
    n_pieces = src_hbm.shape[0] // size

    def copy(c):
        return pltpu.make_async_copy(src_hbm.at[pl.ds(c * size, size), :], stage.at[c % 2],
                                     sem.at[c % 2])

    copy(0).start(priority=0)
    for c in range(n_pieces):
        if c + 1 < n_pieces:
            copy(c + 1).start(priority=(c + 1) % 2)
        copy(c).wait()
        dst[c * size:(c + 1) * size, :] = stage[c % 2].astype(BF16)


def _ffn_stage_kernel(*refs, finish, n_side, n_w, n_out, n_cast, own_weights):
    it = iter(refs)

    def take(k):
        return [next(it) for _ in range(k)]

    (x_ref,), p_side, (xs_ref,), s_side = take(1), take(n_side), take(1), take(n_side)
    weights, cast_src = take(n_w), take(n_cast)
    p_out, s_out, cast_dst, (carry,) = take(n_out), take(n_out), take(n_cast), take(1)
    _cast_blocks(cast_src, cast_dst)
    step = pl.program_id(0)
    sample_step = pl.num_programs(0) - 1
    if own_weights:
        w_in_scr, w_out_scr, stage_in, stage_out, sem_in, sem_out = take(6)
        w_in_hbm, w_out_hbm = weights[1], weights[2]
        weights = [weights[0], w_in_scr, w_out_scr] + weights[3:]

        @pl.when(step == 0)
        def _():
            _load_as_bf16(w_in_hbm, w_in_scr, stage_in, sem_in)
            _load_as_bf16(w_out_hbm, w_out_scr, stage_out, sem_out)

    def finish_previous_tile():
        finish(carry[...], p_side, weights, p_out)

    @pl.when(step == 0)
    def _():
        carry[...] = jnp.zeros_like(carry)

    @pl.when(step < sample_step)
    def _():
        carry[...] = _swiglu_residual(x_ref[...], *weights[:3], finish_previous_tile)

    @pl.when(step == sample_step)
    def _():
        x_new = _swiglu_residual(xs_ref[...], *weights[:3], finish_previous_tile)
        finish(x_new, s_side, weights, s_out)


def _ffn1_finish(x1, side, weights, outs):
    gmix_ref, (x1_ref, h_ref) = weights[3], outs
    x1_ref[...] = x1
    h_ref[...] = _rms(x1, gmix_ref[...]).astype(BF16)


def _activated_projection(h, weights, put, put_la):
    wt_ref, wgu_ref, bg_ref, lng_ref, lnb_ref = weights

    def proj(lo, width):
        return _dot_nt(h, wt_ref[lo:lo + width, :])

    lr = proj(LR_LO, LANES).astype(BF16)
    logit = _dot(lr, wgu_ref[...]) + bg_ref[...]
    put_la(_log_sigmoid(logit) * (1.0 / GATE_NORM))
    half = D_MODEL // 2
    segments = [(0, 0, jax.nn.gelu),
                (A_WIDTH, A_WIDTH,
                 lambda z: _layernorm(jax.nn.gelu(z), lng_ref[...], lnb_ref[...])),
                (QK_LO, QK_LO, lambda z: z * (B_DK ** -0.5)),
                (QK_LO + B_KW, QK_LO + B_KW, lambda z: z)]
    segments += [(VB_LO + j * half, VB_LO + j * half, lambda z: z) for j in range(2)]
    segments += [(GATE_LO + j * half, LR_LO + j * half, jax.nn.sigmoid) for j in range(4)]
    segments += [(GB_LO + j * half, GB_LO + j * half, jax.nn.silu) for j in range(2)]
    for w_lo, p_lo, act in segments:
        put(p_lo, act(proj(w_lo, half)))


def _gla_prep(la, qk, ones_tril):
    la_hi = la.astype(BF16)
    la_lo = (la - la_hi.astype(F32)).astype(BF16)
    b = _dot(jnp.concatenate([ones_tril, ones_tril], axis=1),
             jnp.concatenate([la_hi, la_lo], axis=0))
    r0 = b[HALF // 2 - 1:HALF // 2, :]
    r1 = b[HALF + HALF // 2 - 1:HALF + HALF // 2, :]
    b_mid = b[HALF - 1:HALF, :]
    b_end = b[CHUNK - 1:CHUNK, :]
    bq0 = b[:HALF] - r0
    bq1 = b[HALF:] - r1
    qs = qk[:, :B_KW]
    k = qk[:, B_KW:]
    a0 = qs[:HALF] * jnp.exp(bq0)
    a1 = qs[HALF:] * jnp.exp(bq1)
    k0 = k[:HALF] * jnp.exp(-bq0)
    k1 = k[HALF:] * jnp.exp(-bq1)
    q_off1 = a1 * jnp.exp(r1 - b_mid)
    k_off0 = k0 * jnp.exp(b_mid - r0)
    q_int = jnp.concatenate([a0 * jnp.exp(r0), q_off1 * jnp.exp(b_mid)], axis=0)
    k_st = jnp.concatenate([k_off0 * jnp.exp(b_end - b_mid), k1 * jnp.exp(b_end - r1)], axis=0)
    return dict(a0=a0.astype(BF16), a1=a1.astype(BF16), k0=k0.astype(BF16), k1=k1.astype(BF16),
                q_off1=q_off1.astype(BF16), k_off0=k_off0.astype(BF16), q_int=q_int.astype(BF16),
                k_st=k_st, decay=jnp.exp(b_end))


def _mixer_kernel(*refs, n_chunks, n_dec, n_cast):
    it = iter(refs)

    def take(k):
        return [next(it) for _ in range(k)]

    h_ref, x1_ref, d_h_ref, d_state_ref = take(4)
    proj_w = take(5)
    ws_ref, bsp_ref, ggla_ref, wpa_ref, wpb_ref, wo_ref, ws0_ref, bs0_ref = take(8)
    cast_src = take(n_cast)
    x2_ref, sout_ref, d_ya_ref, d_yb_ref, d_sout_ref, d_cv_ref, d_gate_ref = take(7)
    cast_dst = take(n_cast)
    s_scr, o_scr, p_scr, la_ref, d_p_scr, d_la_ref = take(6)
    t = pl.program_id(1)
    tt = h_ref.shape[0]

    @pl.when(t == 0)
    def _():
        s_scr[...] = jnp.zeros_like(s_scr)

    def put(lo, val):
        p_scr[:, lo:lo + val.shape[1]] = val[:tt].astype(BF16)
        d_p_scr[:, lo:lo + val.shape[1]] = val[tt:tt + n_dec]

    def put_la(val):
        la_ref[...] = val[:tt]
        d_la_ref[...] = val[tt:tt + n_dec]

    _activated_projection(jnp.concatenate([h_ref[...], d_h_ref[0]], axis=0), proj_w, put, put_la)
    uv_ref, qk_ref, v_ref, gb_ref, sga_ref, sgb_ref = (
        p_scr.at[:, j * D_MODEL:(j + 1) * D_MODEL] for j in range(6))
    d_uv_ref, d_qk_ref, d_v_ref, d_gb_ref = (
        d_p_scr.at[:, j * D_MODEL:(j + 1) * D_MODEL] for j in range(4))

    row = lax.broadcasted_iota(jnp.int32, (CHUNK, CHUNK), 0)
    col = lax.broadcasted_iota(jnp.int32, (CHUNK, CHUNK), 1)
    causal = row >= col
    ones_tril = causal.astype(BF16)
    ws_tril = [jnp.where(causal, ws_ref[g], 0.0).astype(BF16) for g in range(A_GROUPS)]
    zero_blk = jnp.zeros((HALF, B_DK), BF16)
    chunks = [slice(c * CHUNK, (c + 1) * CHUNK) for c in range(n_chunks)]
    heads = [(slice(h * B_DK, (h + 1) * B_DK), slice(h * B_DV, (h + 1) * B_DV))
             for h in range(B_HEADS)]

    ya = []
    for rows in chunks:
        u = uv_ref[rows, :A_WIDTH].astype(F32)
        vn = uv_ref[rows, A_WIDTH:]
        parts = []
        for g in range(A_GROUPS):
            cols = slice(g * A_GROUP_DIM, (g + 1) * A_GROUP_DIM)
            mixed = _dot(ws_tril[g], vn[:, cols]) + bsp_ref[:, g:g + 1]
            parts.append((u[:, cols] * mixed).astype(BF16))
        ya.append(jnp.concatenate(parts, axis=1))

    prep = [_gla_prep(la_ref[rows, :], qk_ref[rows, :].astype(F32), ones_tril) for rows in chunks]
    scores = []
    for p in prep:
        per_head = []
        for kc, _ in heads:
            top = _dot_nt(p["a0"][:, kc], jnp.concatenate([p["k0"][:, kc], zero_blk], axis=0))
            bot = _dot_nt(
                jnp.concatenate([p["q_off1"][:, kc], p["a1"][:, kc]], axis=1),
                jnp.concatenate([jnp.concatenate([p["k_off0"][:, kc], zero_blk], axis=1),
                                 jnp.concatenate([zero_blk, p["k1"][:, kc]], axis=1)], axis=0))
            s = jnp.concatenate([top, bot], axis=0)
            per_head.append(jnp.where(causal, s, 0.0).astype(BF16))
        scores.append(per_head)

    state = [s_scr[h] for h in range(B_HEADS)]
    state_before = []
    for rows, p in zip(chunks, prep):
        state_before.append([s.astype(BF16) for s in state])
        for h, (kc, vc) in enumerate(heads):
            upd = _dot(p["k_st"][:, kc].T.astype(BF16), v_ref[rows, vc])
            decay_col = jnp.broadcast_to(p["decay"][:, kc], (B_DK, B_DK)).T
            state[h] = state[h] * jnp.concatenate([decay_col, decay_col], axis=1) + upd
    for h in range(B_HEADS):
        s_scr[h] = state[h]

    yb = []
    for c, (rows, p) in enumerate(zip(chunks, prep)):
        gb = gb_ref[rows, :].astype(F32)
        parts = []
        for h, (kc, vc) in enumerate(heads):
            o = _dot(jnp.concatenate([scores[c][h], p["q_int"][:, kc]], axis=1),
                     jnp.concatenate([v_ref[rows, vc], state_before[c][h]], axis=0))
            parts.append((_rms(o, ggla_ref[...]) * gb[:, vc]).astype(BF16))
        yb.append(jnp.concatenate(parts, axis=1))

    pa = _dot(jnp.concatenate(ya, axis=0), wpa_ref[...])
    _cast_blocks(cast_src, cast_dst)
    _sample_mixers(d_uv_ref, d_qk_ref, d_v_ref, d_gb_ref, d_la_ref, d_state_ref, ws0_ref, bs0_ref,
                   ggla_ref, d_ya_ref.at[0], d_yb_ref.at[0], d_sout_ref, d_cv_ref.at[0], o_scr,
                   n_dec)
    d_gate_ref[0] = d_p_scr[:, LR_LO:]
    mix = (sga_ref[...].astype(F32) * pa
           + sgb_ref[...].astype(F32) * _dot(jnp.concatenate(yb, axis=0), wpb_ref[...]))
    x2_ref[...] = x1_ref[...] + _dot(mix.astype(BF16), wo_ref[...])

    @pl.when(t == pl.num_programs(1) - 1)
    def _():
        sout_ref[0] = s_scr[...]


def _sample_mixers(uv_ref, qk_ref, v_ref, gb_ref, la_ref, s_ref, ws0_ref, bs0_ref, ggla_ref,
                   ya_ref, yb_ref, sout_ref, cv_ref, o_scr, nb):
    vn = uv_ref[:, A_WIDTH:]
    cv_ref[...] = vn
    ya_ref[...] = uv_ref[:, :A_WIDTH] * (vn * ws0_ref[...] + bs0_ref[...])

    a = jnp.exp(la_ref[...])
    qk = qk_ref[...]
    qs = qk[:, :B_KW]
    k = qk[:, B_KW:]
    v = v_ref[...]
    pad = jnp.zeros((LANES - 3 * nb, B_DK), F32)
    for h in range(B_HEADS):
        kc = slice(h * B_DK, (h + 1) * B_DK)
        vc = slice(h * B_DV, (h + 1) * B_DV)
        xt = jnp.concatenate([a[:, kc], k[:, kc], qs[:, kc], pad], axis=0).T
        for n in range(nb):
            s_new = (s_ref[n, h] * xt[:, n:n + 1]
                     + xt[:, nb + n:nb + n + 1] * v[n:n + 1, vc])
            sout_ref[n, h] = s_new
            o_scr[n:n + 1, vc] = jnp.sum(xt[:, 2 * nb + n:2 * nb + n + 1] * s_new,
                                         axis=0, keepdims=True)
    gb = gb_ref[...]
    for h in range(B_HEADS):
        vc = slice(h * B_DV, (h + 1) * B_DV)
        yb_ref[:, vc] = _rms(o_scr[:, vc], ggla_ref[...]) * gb[:, vc]


def _merge_kernel(ya_ref, yb_ref, ga_ref, gb_ref, x1_ref, wpa_ref, wpb_ref, wo_ref, x2_ref):
    mix = (ga_ref[...].astype(F32) * _dot(ya_ref[...], wpa_ref[...])
           + gb_ref[...].astype(F32) * _dot(yb_ref[...], wpb_ref[...]))
    x2_ref[...] = x1_ref[...] + _dot(mix.astype(BF16), wo_ref[...])


def _ffn2_finish(x3, side, weights, outs, *, final_norm):
    (p_ref,), (y_ref,) = side, outs
    gple_ref, wpg_ref, wple_ref, gfin_ref = weights[3:]
    gate = jax.nn.sigmoid(_dot(_rms(x3, gple_ref[...]).astype(BF16), wpg_ref[...]))
    x4 = x3 + _dot(p_ref[...].astype(BF16), wple_ref[...]) * gate
    y_ref[...] = _rms(x4, gfin_ref[...]) if final_norm else x4


def _resident(shape):
    zeros = (0,) * len(shape)
    return pl.BlockSpec(shape, lambda *_: zeros, pipeline_mode=pl.Buffered(1))


def _rows(tm, width, colblk=0):
    return pl.BlockSpec((tm, width), lambda i: (i, colblk))


def _params(n_axes):
    return pltpu.CompilerParams(dimension_semantics=("arbitrary",) * n_axes,
                                vmem_limit_bytes=VMEM_LIMIT)


def _cast_specs(weights, n_steps, linear_step=lambda i: i):
    specs, shapes = [], []
    for w in weights:
        rows, cols = w.shape
        n_blocks = max(n for n in range(1, n_steps + 1) if rows % (n * BF16_SUBLANES) == 0)
        specs.append(pl.BlockSpec(
            (rows // n_blocks, cols),
            lambda *idx, n=n_blocks: (jnp.minimum(linear_step(*idx), n - 1), 0)))
        shapes.append(jax.ShapeDtypeStruct(w.shape, BF16))
    return specs, shapes


def _ffn_stage_call(name, finish, x, x_s, side, side_s, weights, outputs, to_cast=(),
                    own_weights=False):
    m, ms = x.shape[0], x_s.shape[0]
    assert m % TOKEN_TILE == 0
    n = m // TOKEN_TILE
    weight_specs = [_resident(w.shape) for w in weights]
    scratch = [pltpu.VMEM((TOKEN_TILE, D_MODEL), F32)]
    if own_weights:
        w_in, w_out = weights[1], weights[2]
        assert w_in.shape[0] % LOAD_ROWS == 0 and w_out.shape[0] % LOAD_ROWS == 0
        weight_specs[1] = weight_specs[2] = pl.BlockSpec(memory_space=pl.ANY)
        scratch += [pltpu.VMEM(w_in.shape, BF16), pltpu.VMEM(w_out.shape, BF16),
                    pltpu.VMEM((2, LOAD_ROWS, w_in.shape[1]), F32),
                    pltpu.VMEM((2, LOAD_ROWS, w_out.shape[1]), F32),
                    pltpu.SemaphoreType.DMA((2,)), pltpu.SemaphoreType.DMA((2,))]

    def tile(cols, lag):
        return pl.BlockSpec((TOKEN_TILE, cols), lambda i: (jnp.clip(i - lag, 0, n - 1), 0))

    def s_spec(cols):
        return pl.BlockSpec((ms, cols), lambda i: (0, 0))

    cast_specs, cast_shapes = _cast_specs(to_cast, n)
    outs = pl.pallas_call(
        functools.partial(_ffn_stage_kernel, finish=finish, n_side=len(side), n_w=len(weights),
                          n_out=len(outputs), n_cast=len(to_cast), own_weights=own_weights),
        grid=(n + 1,),
        in_specs=[tile(D_MODEL, 0)] + [tile(a.shape[1], 1) for a in side]
                 + [s_spec(D_MODEL)] + [s_spec(a.shape[1]) for a in side_s]
                 + weight_specs + cast_specs,
        out_specs=[tile(c, 1) for c, _ in outputs] + [s_spec(c) for c, _ in outputs] + cast_specs,
        out_shape=[jax.ShapeDtypeStruct((m, c), dt) for c, dt in outputs]
                  + [jax.ShapeDtypeStruct((ms, c), dt) for c, dt in outputs] + cast_shapes,
        scratch_shapes=scratch,
        compiler_params=_params(1),
        name=name,
    )(x, *side, x_s, *side_s, *weights, *to_cast)
    k = len(outputs)
    return outs[:k], outs[k:2 * k], outs[2 * k:]


def _mixer(h, x1, n_seq, seq, d_h, d_state, proj_w, ws, bsp_t, ggla, wpa, wpb, wo, ws0, bs0,
           to_cast=()):
    tt = MIXER_TILE
    nt = seq // tt
    m = n_seq * seq
    steps = n_seq * nt
    n_all = d_h.shape[0]
    assert n_all % steps == 0
    n_dec = n_all // steps
    assert n_dec <= BF16_SUBLANES
    d_h3 = jnp.pad(d_h.reshape(steps, n_dec, D_MODEL), ((0, 0), (0, BF16_SUBLANES - n_dec), (0, 0)))

    def seg(colblk, width=D_MODEL):
        return pl.BlockSpec((tt, width), lambda b, t: (b * nt + t, colblk))

    def d_seg(colblk, width=D_MODEL):
        return pl.BlockSpec((1, n_dec, width), lambda b, t: (b * nt + t, 0, colblk))

    d_state_spec = pl.BlockSpec((n_dec, B_HEADS, B_DK, B_DV), lambda b, t: (b * nt + t, 0, 0, 0))
    d_h_spec = pl.BlockSpec((1, BF16_SUBLANES, D_MODEL), lambda b, t: (b * nt + t, 0, 0))
    small = list(proj_w) + [ws, bsp_t, ggla, wpa, wpb, wo, ws0, bs0]
    cast_specs, cast_shapes = _cast_specs(to_cast, steps, lambda b, t: b * nt + t)
    x2, sp, ya, yb, ss, cv, gates, *cast = pl.pallas_call(
        functools.partial(_mixer_kernel, n_chunks=tt // CHUNK, n_dec=n_dec, n_cast=len(to_cast)),
        grid=(n_seq, nt),
        in_specs=[seg(0), seg(0), d_h_spec, d_state_spec]
                 + [_resident(w.shape) for w in small] + cast_specs,
        out_specs=[seg(0),
                   pl.BlockSpec((1, B_HEADS, B_DK, B_DV), lambda b, t: (b, 0, 0, 0)),
                   d_seg(0, A_WIDTH), d_seg(0, B_VW), d_state_spec, d_seg(0, A_WIDTH),
                   d_seg(0, 2 * D_MODEL)] + cast_specs,
        out_shape=[jax.ShapeDtypeStruct((m, D_MODEL), F32),
                   jax.ShapeDtypeStruct((n_seq, B_HEADS, B_DK, B_DV), F32),
                   jax.ShapeDtypeStruct((steps, n_dec, A_WIDTH), F32),
                   jax.ShapeDtypeStruct((steps, n_dec, B_VW), F32),
                   jax.ShapeDtypeStruct(d_state.shape, F32),
                   jax.ShapeDtypeStruct((steps, n_dec, A_WIDTH), F32),
                   jax.ShapeDtypeStruct((steps, n_dec, 2 * D_MODEL), F32)] + cast_shapes,
        scratch_shapes=[pltpu.VMEM((B_HEADS, B_DK, B_DV), F32), pltpu.VMEM((n_dec, B_VW), F32),
                        pltpu.VMEM((tt, P_COLS), BF16), pltpu.VMEM((tt, B_KW), F32),
                        pltpu.VMEM((n_dec, P_COLS), F32), pltpu.VMEM((n_dec, B_KW), F32)],
        compiler_params=_params(2),
        name="mixer",
    )(h, x1, d_h3, d_state, *small, *to_cast)
    return (x2, sp, ya.reshape(n_all, A_WIDTH), yb.reshape(n_all, B_VW), ss,
            cv.reshape(n_all, A_WIDTH), gates.reshape(n_all, 2 * D_MODEL), cast)


def _merge(ya, yb, gates, x1, wpa, wpb, wo):
    tm = m = x1.shape[0]
    return pl.pallas_call(
        _merge_kernel,
        grid=(m // tm,),
        in_specs=[_rows(tm, A_WIDTH), _rows(tm, B_VW), _rows(tm, D_MODEL, 0),
                  _rows(tm, D_MODEL, 1), _rows(tm, D_MODEL), _resident(wpa.shape),
                  _resident(wpb.shape), _resident(wo.shape)],
        out_specs=_rows(tm, D_MODEL),
        out_shape=jax.ShapeDtypeStruct((m, D_MODEL), F32),
        compiler_params=_params(1),
        name="merge",
    )(ya, yb, gates, gates, x1, wpa, wpb, wo)


def kernel(x_prompt, x_sample, p_prompt, p_sample, state_gla, g_ffn1, w_ffn1_in, w_ffn1_out, g_mix,
           w_in, ln_v_g, ln_v_b, w_spatial, b_spatial, w_gate_up, b_gate, g_gla_out, w_proj_a,
           w_proj_b, w_out, g_ffn2, w_ffn2_in, w_ffn2_out, g_ple, w_ple_gate, w_ple, g_final):
    depth = w_in.shape[0]
    n_seq, seq, _ = x_prompt.shape
    n_dec, dec_seq, _ = x_sample.shape
    assert dec_seq == 1 and seq % MIXER_TILE == 0

    def row(vec):
        return vec.reshape(1, -1).astype(F32)

    assert sum(IN_SIZES[:6]) == LR_LO and sum(IN_SIZES) == GATE_LO + 2 * D_MODEL
    xp = x_prompt.reshape(n_seq * seq, D_MODEL)
    xs = x_sample.reshape(n_dec, D_MODEL)
    gfin = row(g_final)
    sp_list, ss_list, vs_list = [], [], []
    for i in range(depth):
        wgu =jnp.pad(w_gate_up[i], ((0, LANES - GATE_RANK), (0, 0))).astype(BF16)
        g1, gmix, g2, gple = row(g_ffn1[i]), row(g_mix[i]), row(g_ffn2[i]), row(g_ple[i])
        lng, lnb, bg, ggla = row(ln_v_g[i]), row(ln_v_b[i]), row(b_gate[i]), row(g_gla_out[i])
        ws0 = row(jnp.repeat(w_spatial[i][:, 0, 0], A_GROUP_DIM))
        bs0 = row(jnp.repeat(b_spatial[i][:, 0], A_GROUP_DIM))

        wide, act = (D_MODEL, F32), (D_MODEL, BF16)
        (x1p, hp), (x1s, hs), (w_inb, wpa, wpb, wo) = _ffn_stage_call(
            "ffn1", _ffn1_finish, xp, xs, [], [], [g1, w_ffn1_in[i], w_ffn1_out[i], gmix],
            [wide, act], to_cast=[w_in[i].T, w_proj_a[i], w_proj_b[i], w_out[i]], own_weights=True)
        x2p, sp, ya, yb, ss, cv, gates, (w2i, w2o, wpg, wple) = _mixer(
            hp, x1p, n_seq, seq, hs, state_gla[i], [w_inb, wgu, bg, lng, lnb], w_spatial[i],
            b_spatial[i].T, ggla, wpa, wpb, wo, ws0, bs0,
            to_cast=[w_ffn2_in[i], w_ffn2_out[i], w_ple_gate[i], w_ple[i]])
        x2s = _merge(ya.astype(BF16), yb.astype(BF16), gates, x1s, wpa, wpb, wo)

        ple_p = p_prompt[i].reshape(n_seq * seq, PLE_DIM)
        ple_s = p_sample[i].reshape(n_dec, PLE_DIM)
        (xp,), (xs,), _ = _ffn_stage_call(
            "ffn2", functools.partial(_ffn2_finish, final_norm=i == depth - 1), x2p, x2s,
            [ple_p], [ple_s], [g2, w2i, w2o, gple, wpg, wple, gfin], [wide])

        sp_list.append(sp)
        ss_list.append(ss)
        vs_list.append(cv.reshape(n_dec, dec_seq, A_WIDTH))
    return (xp.reshape(n_seq, seq, D_MODEL), xs.reshape(n_dec, dec_seq, D_MODEL),
            jnp.stack(sp_list), jnp.stack(ss_list), jnp.stack(vs_list))
```

```python
import functools

import jax
import jax.numpy as jnp
from jax import lax
from jax.experimental import pallas as pl
from jax.experimental.pallas import tpu as pltpu

D_MODEL = 1024
D_FF = 2816
PLE_DIM = 256
CHUNK = 128
A_GROUPS = 4
A_GROUP_DIM = 128
A_WIDTH = A_GROUPS * A_GROUP_DIM
B_HEADS = 4
B_DK = 128
B_DV = 256
B_KW = B_HEADS * B_DK
B_VW = B_HEADS * B_DV
GATE_RANK = 16
GATE_NORM = 16.0
EPS = 1e-6
IN_SIZES = (A_WIDTH, A_WIDTH, B_KW, B_KW, B_VW, B_VW, GATE_RANK, D_MODEL, D_MODEL)

LANES = 128
P_COLS = 6 * D_MODEL
QK_LO = 2 * A_WIDTH
VB_LO = QK_LO + 2 * B_KW
GB_LO = VB_LO + B_VW
LR_LO = GB_LO + B_VW
GATE_LO = LR_LO + GATE_RANK
FF_CHUNK = 256
HALF = CHUNK // 2
MIXER_TILE = 4 * CHUNK
TOKEN_TILE = 512
BF16_SUBLANES = 16
VMEM_LIMIT = 56 * 1024 * 1024

F32 = jnp.float32
BF16 = jnp.bfloat16


def _dot(a, b):
    return jnp.dot(a, b, preferred_element_type=F32)


def _dot_nt(a, b):
    return lax.dot_general(a, b, (((1,), (1,)), ((), ())), preferred_element_type=F32)


def _rms(x, g):
    return x * lax.rsqrt(jnp.mean(x * x, axis=-1, keepdims=True) + EPS) * g


def _log_sigmoid(x):
    return jnp.minimum(x, 0.0) - jnp.log1p(jnp.exp(-jnp.abs(x)))


def _layernorm(x, g, b):
    mu = jnp.mean(x, axis=-1, keepdims=True)
    xc = x - mu
    var = jnp.mean(xc * xc, axis=-1, keepdims=True)
    return xc * lax.rsqrt(var + EPS) * g + b


def _swiglu_residual(x, g_ref, w_in_ref, w_out_ref, early, before_chunk=None):
    h = _rms(x, g_ref[...]).astype(BF16)
    acts = []
    for ci, lo in enumerate(range(0, D_FF, FF_CHUNK)):
        if before_chunk is not None:
            before_chunk(ci)
        gate = _dot(h, w_in_ref[:, lo:lo + FF_CHUNK])
        up = _dot(h, w_in_ref[:, D_FF + lo:D_FF + lo + FF_CHUNK])
        acts.append((jax.nn.silu(gate) * up).astype(BF16))
        if ci == 1:
            early()
    return x + 0.5 * _dot(jnp.concatenate(acts, axis=1), w_out_ref[...])


def _cast_blocks(src_refs, dst_refs):
    for src, dst in zip(src_refs, dst_refs):
        dst[...] = src[...].astype(dst.dtype)


class _WeightStream:
    def __init__(self, w_in_hbm, w_out_hbm, w_in_scr, w_out_scr, stage_in, stage_out, sems):
        self.w_in_hbm, self.w_out_hbm = w_in_hbm, w_out_hbm
        self.w_in_scr, self.w_out_scr = w_in_scr, w_out_scr
        self.stage_in, self.stage_out, self.sems = stage_in, stage_out, sems

    def _copies(self, ci):
        slot, lo = ci % 2, ci * FF_CHUNK
        cols = [pl.ds(lo, FF_CHUNK), pl.ds(D_FF + lo, FF_CHUNK)]
        return ([pltpu.make_async_copy(self.w_in_hbm.at[:, c], self.stage_in.at[slot, j],
                                       self.sems.at[slot, j]) for j, c in enumerate(cols)]
                + [pltpu.make_async_copy(self.w_out_hbm.at[pl.ds(lo, FF_CHUNK), :],
                                         self.stage_out.at[slot], self.sems.at[slot, 2])])

    def start(self, ci):
        for copy in self._copies(ci):
            copy.start()

    def finish(self, ci):
        slot, lo = ci % 2, ci * FF_CHUNK
        for copy in self._copies(ci):
            copy.wait()
        self.w_in_scr[:, lo:lo + FF_CHUNK] = self.stage_in[slot, 0].astype(BF16)
        self.w_in_scr[:, D_FF + lo:D_FF + lo + FF_CHUNK] = self.stage_in[slot, 1].astype(BF16)
        self.w_out_scr[lo:lo + FF_CHUNK, :] = self.stage_out[slot].astype(BF16)
        if ci + 2 < D_FF // FF_CHUNK:
            self.start(ci + 2)


def _ffn_stage_kernel(*refs, finish, n_side, n_w, n_out, n_cast, own_weights, defer):
    it = iter(refs)

    def take(k):
        return [next(it) for _ in range(k)]

    (x_ref,), p_side, (xs_ref,), s_side = take(1), take(n_side), take(1), take(n_side)
    weights, cast_src = take(n_w), take(n_cast)
    p_out, s_out, cast_dst = take(n_out), take(n_out), take(n_cast)
    carry = next(it) if defer else None
    _cast_blocks(cast_src, cast_dst)
    step = pl.program_id(0)
    sample_step = pl.num_programs(0) - 1
    stream = None
    if own_weights:
        w_in_scr, w_out_scr, stage_in, stage_out, sems = take(5)
        stream = _WeightStream(weights[1], weights[2], w_in_scr, w_out_scr, stage_in, stage_out,
                               sems)
        weights = [weights[0], w_in_scr, w_out_scr] + weights[3:]

    def finish_previous_tile():
        if defer:
            finish(carry[...], p_side, weights, p_out)

    def prompt_tile(before_chunk=None):
        x_new = _swiglu_residual(x_ref[...], *weights[:3], finish_previous_tile, before_chunk)
        if defer:
            carry[...] = x_new
        else:
            finish(x_new, p_side, weights, p_out)

    if defer:
        @pl.when(step == 0)
        def _():
            carry[...] = jnp.zeros_like(carry)

    if own_weights:
        @pl.when(step == 0)
        def _():
            stream.start(0)
            stream.start(1)
            prompt_tile(before_chunk=stream.finish)

    @pl.when((step < sample_step) & (step > 0) if own_weights else step < sample_step)
    def _():
        prompt_tile()

    @pl.when(step == sample_step)
    def _():
        x_new = _swiglu_residual(xs_ref[...], *weights[:3], finish_previous_tile)
        finish(x_new, s_side, weights, s_out)


def _ffn1_finish(x1, side, weights, outs):
    gmix_ref, (x1_ref, h_ref) = weights[3], outs
    x1_ref[...] = x1
    h_ref[...] = _rms(x1, gmix_ref[...]).astype(BF16)


def _activated_projection(h, weights, put, put_la):
    wt_ref, wgu_ref, bg_ref, lng_ref, lnb_ref = weights

    def proj(lo, width):
        return _dot_nt(h, wt_ref[lo:lo + width, :])

    lr = proj(LR_LO, LANES).astype(BF16)
    logit = _dot(lr, wgu_ref[...]) + bg_ref[...]
    put_la(_log_sigmoid(logit) * (1.0 / GATE_NORM))
    half = D_MODEL // 2
    segments = [(0, 0, jax.nn.gelu),
                (A_WIDTH, A_WIDTH,
                 lambda z: _layernorm(jax.nn.gelu(z), lng_ref[...], lnb_ref[...])),
                (QK_LO, QK_LO, lambda z: z * (B_DK ** -0.5)),
                (QK_LO + B_KW, QK_LO + B_KW, lambda z: z)]
    segments += [(VB_LO + j * half, VB_LO + j * half, lambda z: z) for j in range(2)]
    segments += [(GATE_LO + j * half, LR_LO + j * half, jax.nn.sigmoid) for j in range(4)]
    segments += [(GB_LO + j * half, GB_LO + j * half, jax.nn.silu) for j in range(2)]
    for w_lo, p_lo, act in segments:
        put(p_lo, act(proj(w_lo, half)))


def _gla_prep(la, qk, ones_tril):
    la_hi = la.astype(BF16)
    la_lo = (la - la_hi.astype(F32)).astype(BF16)
    b = _dot(jnp.concatenate([ones_tril, ones_tril], axis=1),
             jnp.concatenate([la_hi, la_lo], axis=0))
    r0 = b[HALF // 2 - 1:HALF // 2, :]
    r1 = b[HALF + HALF // 2 - 1:HALF + HALF // 2, :]
    b_mid = b[HALF - 1:HALF, :]
    b_end = b[CHUNK - 1:CHUNK, :]
    bq0 = b[:HALF] - r0
    bq1 = b[HALF:] - r1
    qs = qk[:, :B_KW]
    k = qk[:, B_KW:]
    a0 = qs[:HALF] * jnp.exp(bq0)
    a1 = qs[HALF:] * jnp.exp(bq1)
    k0 = k[:HALF] * jnp.exp(-bq0)
    k1 = k[HALF:] * jnp.exp(-bq1)
    q_off1 = a1 * jnp.exp(r1 - b_mid)
    k_off0 = k0 * jnp.exp(b_mid - r0)
    q_int = jnp.concatenate([a0 * jnp.exp(r0), q_off1 * jnp.exp(b_mid)], axis=0)
    k_st = jnp.concatenate([k_off0 * jnp.exp(b_end - b_mid), k1 * jnp.exp(b_end - r1)], axis=0)
    return dict(a0=a0.astype(BF16), a1=a1.astype(BF16), k0=k0.astype(BF16), k1=k1.astype(BF16),
                q_off1=q_off1.astype(BF16), k_off0=k_off0.astype(BF16), q_int=q_int.astype(BF16),
                k_st=k_st, decay=jnp.exp(b_end))


def _mixer_kernel(*refs, n_chunks, n_dec, n_cast):
    it = iter(refs)

    def take(k):
        return [next(it) for _ in range(k)]

    h_ref, x1_ref, d_h_ref, d_state_ref = take(4)
    proj_w = take(5)
    ws_ref, bsp_ref, ggla_ref, wpa_ref, wpb_ref, wo_ref, ws0_ref, bs0_ref = take(8)
    cast_src = take(n_cast)
    x2_ref, sout_ref, d_ya_ref, d_yb_ref, d_sout_ref, d_cv_ref, d_gate_ref = take(7)
    cast_dst = take(n_cast)
    s_scr, o_scr, p_scr, la_ref, d_p_scr, d_la_ref = take(6)
    t = pl.program_id(1)
    tt = h_ref.shape[0]

    @pl.when(t == 0)
    def _():
        s_scr[...] = jnp.zeros_like(s_scr)

    def put(lo, val):
        p_scr[:, lo:lo + val.shape[1]] = val[:tt].astype(BF16)
        d_p_scr[:, lo:lo + val.shape[1]] = val[tt:tt + n_dec]

    def put_la(val):
        la_ref[...] = val[:tt]
        d_la_ref[...] = val[tt:tt + n_dec]

    _activated_projection(jnp.concatenate([h_ref[...], d_h_ref[0]], axis=0), proj_w, put, put_la)
    uv_ref, qk_ref, v_ref, gb_ref, sga_ref, sgb_ref = (
        p_scr.at[:, j * D_MODEL:(j + 1) * D_MODEL] for j in range(6))
    d_uv_ref, d_qk_ref, d_v_ref, d_gb_ref = (
        d_p_scr.at[:, j * D_MODEL:(j + 1) * D_MODEL] for j in range(4))

    row = lax.broadcasted_iota(jnp.int32, (CHUNK, CHUNK), 0)
    col = lax.broadcasted_iota(jnp.int32, (CHUNK, CHUNK), 1)
    causal = row >= col
    ones_tril = causal.astype(BF16)
    ws_tril = [jnp.where(causal, ws_ref[g], 0.0).astype(BF16) for g in range(A_GROUPS)]
    zero_blk = jnp.zeros((HALF, B_DK), BF16)
    chunks = [slice(c * CHUNK, (c + 1) * CHUNK) for c in range(n_chunks)]
    heads = [(slice(h * B_DK, (h + 1) * B_DK), slice(h * B_DV, (h + 1) * B_DV))
             for h in range(B_HEADS)]

    ya = []
    for rows in chunks:
        u = uv_ref[rows, :A_WIDTH].astype(F32)
        vn = uv_ref[rows, A_WIDTH:]
        parts = []
        for g in range(A_GROUPS):
            cols = slice(g * A_GROUP_DIM, (g + 1) * A_GROUP_DIM)
            mixed = _dot(ws_tril[g], vn[:, cols]) + bsp_ref[:, g:g + 1]
            parts.append((u[:, cols] * mixed).astype(BF16))
        ya.append(jnp.concatenate(parts, axis=1))

    prep = [_gla_prep(la_ref[rows, :], qk_ref[rows, :].astype(F32), ones_tril) for rows in chunks]
    scores = []
    for p in prep:
        per_head = []
        for kc, _ in heads:
            top = _dot_nt(p["a0"][:, kc], jnp.concatenate([p["k0"][:, kc], zero_blk], axis=0))
            bot = _dot_nt(
                jnp.concatenate([p["q_off1"][:, kc], p["a1"][:, kc]], axis=1),
                jnp.concatenate([jnp.concatenate([p["k_off0"][:, kc], zero_blk], axis=1),
                                 jnp.concatenate([zero_blk, p["k1"][:, kc]], axis=1)], axis=0))
            s = jnp.concatenate([top, bot], axis=0)
            per_head.append(jnp.where(causal, s, 0.0).astype(BF16))
        scores.append(per_head)

    state = [s_scr[h] for h in range(B_HEADS)]
    state_before = []
    for rows, p in zip(chunks, prep):
        state_before.append([s.astype(BF16) for s in state])
        for h, (kc, vc) in enumerate(heads):
            upd = _dot(p["k_st"][:, kc].T.astype(BF16), v_ref[rows, vc])
            decay_col = jnp.broadcast_to(p["decay"][:, kc], (B_DK, B_DK)).T
            state[h] = state[h] * jnp.concatenate([decay_col, decay_col], axis=1) + upd
    for h in range(B_HEADS):
        s_scr[h] = state[h]

    yb = []
    for c, (rows, p) in enumerate(zip(chunks, prep)):
        gb = gb_ref[rows, :].astype(F32)
        parts = []
        for h, (kc, vc) in enumerate(heads):
            o = _dot(jnp.concatenate([scores[c][h], p["q_int"][:, kc]], axis=1),
                     jnp.concatenate([v_ref[rows, vc], state_before[c][h]], axis=0))
            parts.append((_rms(o, ggla_ref[...]) * gb[:, vc]).astype(BF16))
        yb.append(jnp.concatenate(parts, axis=1))

    pa = _dot(jnp.concatenate(ya, axis=0), wpa_ref[...])
    _cast_blocks(cast_src, cast_dst)
    _sample_mixers(d_uv_ref, d_qk_ref, d_v_ref, d_gb_ref, d_la_ref, d_state_ref, ws0_ref, bs0_ref,
                   ggla_ref, d_ya_ref.at[0], d_yb_ref.at[0], d_sout_ref, d_cv_ref.at[0], o_scr,
                   n_dec)
    d_gate_ref[0] = d_p_scr[:, LR_LO:]
    mix = (sga_ref[...].astype(F32) * pa
           + sgb_ref[...].astype(F32) * _dot(jnp.concatenate(yb, axis=0), wpb_ref[...]))
    x2_ref[...] = x1_ref[...] + _dot(mix.astype(BF16), wo_ref[...])

    @pl.when(t == pl.num_programs(1) - 1)
    def _():
        sout_ref[0] = s_scr[...]


def _sample_mixers(uv_ref, qk_ref, v_ref, gb_ref, la_ref, s_ref, ws0_ref, bs0_ref, ggla_ref,
                   ya_ref, yb_ref, sout_ref, cv_ref, o_scr, nb):
    vn = uv_ref[:, A_WIDTH:]
    cv_ref[...] = vn
    ya_ref[...] = uv_ref[:, :A_WIDTH] * (vn * ws0_ref[...] + bs0_ref[...])

    a = jnp.exp(la_ref[...])
    qk = qk_ref[...]
    qs = qk[:, :B_KW]
    k = qk[:, B_KW:]
    v = v_ref[...]
    pad = jnp.zeros((LANES - 3 * nb, B_DK), F32)
    for h in range(B_HEADS):
        kc = slice(h * B_DK, (h + 1) * B_DK)
        vc = slice(h * B_DV, (h + 1) * B_DV)
        xt = jnp.concatenate([a[:, kc], k[:, kc], qs[:, kc], pad], axis=0).T
        for n in range(nb):
            s_new = (s_ref[n, h] * xt[:, n:n + 1]
                     + xt[:, nb + n:nb + n + 1] * v[n:n + 1, vc])
            sout_ref[n, h] = s_new
            o_scr[n:n + 1, vc] = jnp.sum(xt[:, 2 * nb + n:2 * nb + n + 1] * s_new,
                                         axis=0, keepdims=True)
    gb = gb_ref[...]
    for h in range(B_HEADS):
        vc = slice(h * B_DV, (h + 1) * B_DV)
        yb_ref[:, vc] = _rms(o_scr[:, vc], ggla_ref[...]) * gb[:, vc]


def _merge_kernel(ya_ref, yb_ref, ga_ref, gb_ref, x1_ref, wpa_ref, wpb_ref, wo_ref, x2_ref):
    mix = (ga_ref[...].astype(F32) * _dot(ya_ref[...], wpa_ref[...])
           + gb_ref[...].astype(F32) * _dot(yb_ref[...], wpb_ref[...]))
    x2_ref[...] = x1_ref[...] + _dot(mix.astype(BF16), wo_ref[...])


def _ffn2_finish(x3, side, weights, outs, *, final_norm):
    (p_ref,), (y_ref,) = side, outs
    gple_ref, wpg_ref, wple_ref, gfin_ref = weights[3:]
    gate = jax.nn.sigmoid(_dot(_rms(x3, gple_ref[...]).astype(BF16), wpg_ref[...]))
    x4 = x3 + _dot(p_ref[...].astype(BF16), wple_ref[...]) * gate
    y_ref[...] = _rms(x4, gfin_ref[...]) if final_norm else x4


def _resident(shape):
    zeros = (0,) * len(shape)
    return pl.BlockSpec(shape, lambda *_: zeros, pipeline_mode=pl.Buffered(1))


def _rows(tm, width, colblk=0):
    return pl.BlockSpec((tm, width), lambda i: (i, colblk))


def _params(n_axes):
    return pltpu.CompilerParams(dimension_semantics=("arbitrary",) * n_axes,
                                vmem_limit_bytes=VMEM_LIMIT)


def _cast_specs(weights, n_steps, linear_step=lambda i: i):
    specs, shapes = [], []
    for w in weights:
        rows, cols = w.shape
        n_blocks = max(n for n in range(1, n_steps + 1) if rows % (n * BF16_SUBLANES) == 0)
        specs.append(pl.BlockSpec(
            (rows // n_blocks, cols),
            lambda *idx, n=n_blocks: (jnp.minimum(linear_step(*idx), n - 1), 0)))
        shapes.append(jax.ShapeDtypeStruct(w.shape, BF16))
    return specs, shapes


def _ffn_stage_call(name, finish, x, x_s, side, side_s, weights, outputs, to_cast=(),
                    own_weights=False, defer=True):
    m, ms = x.shape[0], x_s.shape[0]
    assert m % TOKEN_TILE == 0
    n = m // TOKEN_TILE
    weight_specs = [_resident(w.shape) for w in weights]
    scratch = [pltpu.VMEM((TOKEN_TILE, D_MODEL), F32)] if defer else []
    lag = 1 if defer else 0
    if own_weights:
        w_in, w_out = weights[1], weights[2]
        weight_specs[1] = weight_specs[2] = pl.BlockSpec(memory_space=pl.ANY)
        scratch += [pltpu.VMEM(w_in.shape, BF16), pltpu.VMEM(w_out.shape, BF16),
                    pltpu.VMEM((2, 2, w_in.shape[0], FF_CHUNK), F32),
                    pltpu.VMEM((2, FF_CHUNK, w_out.shape[1]), F32),
                    pltpu.SemaphoreType.DMA((2, 3))]

    def tile(cols, lag):
        return pl.BlockSpec((TOKEN_TILE, cols), lambda i: (jnp.clip(i - lag, 0, n - 1), 0))

    def s_spec(cols):
        return pl.BlockSpec((ms, cols), lambda i: (0, 0))

    cast_specs, cast_shapes = _cast_specs(to_cast, n)
    outs = pl.pallas_call(
        functools.partial(_ffn_stage_kernel, finish=finish, n_side=len(side), n_w=len(weights),
                          n_out=len(outputs), n_cast=len(to_cast), own_weights=own_weights,
                          defer=defer),
        grid=(n + 1,),
        in_specs=[tile(D_MODEL, 0)] + [tile(a.shape[1], lag) for a in side]
                 + [s_spec(D_MODEL)] + [s_spec(a.shape[1]) for a in side_s]
                 + weight_specs + cast_specs,
        out_specs=[tile(c, lag) for c, _ in outputs] + [s_spec(c) for c, _ in outputs] + cast_specs,
        out_shape=[jax.ShapeDtypeStruct((m, c), dt) for c, dt in outputs]
                  + [jax.ShapeDtypeStruct((ms, c), dt) for c, dt in outputs] + cast_shapes,
        scratch_shapes=scratch,
        compiler_params=_params(1),
        name=name,
    )(x, *side, x_s, *side_s, *weights, *to_cast)
    k = len(outputs)
    return outs[:k], outs[k:2 * k], outs[2 * k:]


def _mixer(h, x1, n_seq, seq, d_h, d_state, proj_w, ws, bsp_t, ggla, wpa, wpb, wo, ws0, bs0,
           to_cast=()):
    tt = MIXER_TILE
    nt = seq // tt
    m = n_seq * seq
    steps = n_seq * nt
    n_all = d_h.shape[0]
    assert n_all % steps == 0
    n_dec = n_all // steps
    assert n_dec <= BF16_SUBLANES
    d_h3 = jnp.pad(d_h.reshape(steps, n_dec, D_MODEL), ((0, 0), (0, BF16_SUBLANES - n_dec), (0, 0)))

    def seg(colblk, width=D_MODEL):
        return pl.BlockSpec((tt, width), lambda b, t: (b * nt + t, colblk))

    def d_seg(colblk, width=D_MODEL):
        return pl.BlockSpec((1, n_dec, width), lambda b, t: (b * nt + t, 0, colblk))

    d_state_spec = pl.BlockSpec((n_dec, B_HEADS, B_DK, B_DV), lambda b, t: (b * nt + t, 0, 0, 0))
    d_h_spec = pl.BlockSpec((1, BF16_SUBLANES, D_MODEL), lambda b, t: (b * nt + t, 0, 0))
    small = list(proj_w) + [ws, bsp_t, ggla, wpa, wpb, wo, ws0, bs0]
    cast_specs, cast_shapes = _cast_specs(to_cast, steps, lambda b, t: b * nt + t)
    x2, sp, ya, yb, ss, cv, gates, *cast = pl.pallas_call(
        functools.partial(_mixer_kernel, n_chunks=tt // CHUNK, n_dec=n_dec, n_cast=len(to_cast)),
        grid=(n_seq, nt),
        in_specs=[seg(0), seg(0), d_h_spec, d_state_spec]
                 + [_resident(w.shape) for w in small] + cast_specs,
        out_specs=[seg(0),
                   pl.BlockSpec((1, B_HEADS, B_DK, B_DV), lambda b, t: (b, 0, 0, 0)),
                   d_seg(0, A_WIDTH), d_seg(0, B_VW), d_state_spec, d_seg(0, A_WIDTH),
                   d_seg(0, 2 * D_MODEL)] + cast_specs,
        out_shape=[jax.ShapeDtypeStruct((m, D_MODEL), F32),
                   jax.ShapeDtypeStruct((n_seq, B_HEADS, B_DK, B_DV), F32),
                   jax.ShapeDtypeStruct((steps, n_dec, A_WIDTH), F32),
                   jax.ShapeDtypeStruct((steps, n_dec, B_VW), F32),
                   jax.ShapeDtypeStruct(d_state.shape, F32),
                   jax.ShapeDtypeStruct((steps, n_dec, A_WIDTH), F32),
                   jax.ShapeDtypeStruct((steps, n_dec, 2 * D_MODEL), F32)] + cast_shapes,
        scratch_shapes=[pltpu.VMEM((B_HEADS, B_DK, B_DV), F32), pltpu.VMEM((n_dec, B_VW), F32),
                        pltpu.VMEM((tt, P_COLS), BF16), pltpu.VMEM((tt, B_KW), F32),
                        pltpu.VMEM((n_dec, P_COLS), F32), pltpu.VMEM((n_dec, B_KW), F32)],
        compiler_params=_params(2),
        name="mixer",
    )(h, x1, d_h3, d_state, *small, *to_cast)
    return (x2, sp, ya.reshape(n_all, A_WIDTH), yb.reshape(n_all, B_VW), ss,
            cv.reshape(n_all, A_WIDTH), gates.reshape(n_all, 2 * D_MODEL), cast)


def _merge(ya, yb, gates, x1, wpa, wpb, wo):
    tm = m = x1.shape[0]
    return pl.pallas_call(
        _merge_kernel,
        grid=(m // tm,),
        in_specs=[_rows(tm, A_WIDTH), _rows(tm, B_VW), _rows(tm, D_MODEL, 0),
                  _rows(tm, D_MODEL, 1), _rows(tm, D_MODEL), _resident(wpa.shape),
                  _resident(wpb.shape), _resident(wo.shape)],
        out_specs=_rows(tm, D_MODEL),
        out_shape=jax.ShapeDtypeStruct((m, D_MODEL), F32),
        compiler_params=_params(1),
        name="merge",
    )(ya, yb, gates, gates, x1, wpa, wpb, wo)


def kernel(x_prompt, x_sample, p_prompt, p_sample, state_gla, g_ffn1, w_ffn1_in, w_ffn1_out, g_mix,
           w_in, ln_v_g, ln_v_b, w_spatial, b_spatial, w_gate_up, b_gate, g_gla_out, w_proj_a,
           w_proj_b, w_out, g_ffn2, w_ffn2_in, w_ffn2_out, g_ple, w_ple_gate, w_ple, g_final):
    depth = w_in.shape[0]
    n_seq, seq, _ = x_prompt.shape
    n_dec, dec_seq, _ = x_sample.shape
    assert dec_seq == 1 and seq % MIXER_TILE == 0

    def row(vec):
        return vec.reshape(1, -1).astype(F32)

    assert sum(IN_SIZES[:6]) == LR_LO and sum(IN_SIZES) == GATE_LO + 2 * D_MODEL
    xp = x_prompt.reshape(n_seq * seq, D_MODEL)
    xs = x_sample.reshape(n_dec, D_MODEL)
    gfin = row(g_final)
    sp_list, ss_list, vs_list = [], [], []
    for i in range(depth):
        wgu =jnp.pad(w_gate_up[i], ((0, LANES - GATE_RANK), (0, 0))).astype(BF16)
        g1, gmix, g2, gple = row(g_ffn1[i]), row(g_mix[i]), row(g_ffn2[i]), row(g_ple[i])
        lng, lnb, bg, ggla = row(ln_v_g[i]), row(ln_v_b[i]), row(b_gate[i]), row(g_gla_out[i])
        ws0 = row(jnp.repeat(w_spatial[i][:, 0, 0], A_GROUP_DIM))
        bs0 = row(jnp.repeat(b_spatial[i][:, 0], A_GROUP_DIM))

        wide, act = (D_MODEL, F32), (D_MODEL, BF16)
        (x1p, hp), (x1s, hs), (w_inb, wpa, wpb, wo) = _ffn_stage_call(
            "ffn1", _ffn1_finish, xp, xs, [], [], [g1, w_ffn1_in[i], w_ffn1_out[i], gmix],
            [wide, act], to_cast=[w_in[i].T, w_proj_a[i], w_proj_b[i], w_out[i]], own_weights=True,
            defer=False)
        x2p, sp, ya, yb, ss, cv, gates, (w2i, w2o, wpg, wple) = _mixer(
            hp, x1p, n_seq, seq, hs, state_gla[i], [w_inb, wgu, bg, lng, lnb], w_spatial[i],
            b_spatial[i].T, ggla, wpa, wpb, wo, ws0, bs0,
            to_cast=[w_ffn2_in[i], w_ffn2_out[i], w_ple_gate[i], w_ple[i]])
        x2s = _merge(ya.astype(BF16), yb.astype(BF16), gates, x1s, wpa, wpb, wo)

        ple_p = p_prompt[i].reshape(n_seq * seq, PLE_DIM)
        ple_s = p_sample[i].reshape(n_dec, PLE_DIM)
        (xp,), (xs,), _ = _ffn_stage_call(
            "ffn2", functools.partial(_ffn2_finish, final_norm=i == depth - 1), x2p, x2s,
            [ple_p], [ple_s], [g2, w2i, w2o, gple, wpg, wple, gfin], [wide])

        sp_list.append(sp)
        ss_list.append(ss)
        vs_list.append(cv.reshape(n_dec, dec_seq, A_WIDTH))
    return (xp.reshape(n_seq, seq, D_MODEL), xs.reshape(n_dec, dec_seq, D_MODEL),
            jnp.stack(sp_list), jnp.stack(ss_list), jnp.stack(vs_list))
```

```python
import functools

import jax
import jax.numpy as jnp
from jax import lax
from jax.experimental import pallas as pl
from jax.experimental.pallas import tpu as pltpu

D_MODEL = 1024
D_FF = 2816
PLE_DIM = 256
CHUNK = 128
A_GROUPS = 4
A_GROUP_DIM = 128
A_WIDTH = A_GROUPS * A_GROUP_DIM
B_HEADS = 4
B_DK = 128
B_DV = 256
B_KW = B_HEADS * B_DK
B_VW = B_HEADS * B_DV
GATE_RANK = 16
GATE_NORM = 16.0
EPS = 1e-6
IN_SIZES = (A_WIDTH, A_WIDTH, B_KW, B_KW, B_VW, B_VW, GATE_RANK, D_MODEL, D_MODEL)

LANES = 128
P_COLS = 6 * D_MODEL
QK_LO = 2 * A_WIDTH
VB_LO = QK_LO + 2 * B_KW
GB_LO = VB_LO + B_VW
LR_LO = GB_LO + B_VW
GATE_LO = LR_LO + GATE_RANK
FF_CHUNK = 256
HALF = CHUNK // 2
MIXER_TILE = 4 * CHUNK
TOKEN_TILE = 512
BF16_SUBLANES = 16
VMEM_LIMIT = 56 * 1024 * 1024

F32 = jnp.float32
BF16 = jnp.bfloat16


def _dot(a, b):
    return jnp.dot(a, b, preferred_element_type=F32)


def _dot_nt(a, b):
    return lax.dot_general(a, b, (((1,), (1,)), ((), ())), preferred_element_type=F32)


def _rms(x, g):
    return x * lax.rsqrt(jnp.mean(x * x, axis=-1, keepdims=True) + EPS) * g


def _log_sigmoid(x):
    return jnp.minimum(x, 0.0) - jnp.log1p(jnp.exp(-jnp.abs(x)))


def _layernorm(x, g, b):
    mu = jnp.mean(x, axis=-1, keepdims=True)
    xc = x - mu
    var = jnp.mean(xc * xc, axis=-1, keepdims=True)
    return xc * lax.rsqrt(var + EPS) * g + b


def _swiglu_residual(x, g_ref, w_in_ref, w_out_ref, early, before_chunk=None):
    h = _rms(x, g_ref[...]).astype(BF16)
    acts = []
    for ci, lo in enumerate(range(0, D_FF, FF_CHUNK)):
        if before_chunk is not None:
            before_chunk(ci)
        gate = _dot(h, w_in_ref[:, lo:lo + FF_CHUNK])
        up = _dot(h, w_in_ref[:, D_FF + lo:D_FF + lo + FF_CHUNK])
        acts.append((jax.nn.silu(gate) * up).astype(BF16))
        if ci == 1:
            early()
    return x + 0.5 * _dot(jnp.concatenate(acts, axis=1), w_out_ref[...])


def _cast_blocks(src_refs, dst_refs):
    for src, dst in zip(src_refs, dst_refs):
        dst[...] = src[...].astype(dst.dtype)


class _WeightStream:
    def __init__(self, w_in_hbm, w_out_hbm, w_in_scr, w_out_scr, stage_in, stage_out, sems):
        self.w_in_hbm, self.w_out_hbm = w_in_hbm, w_out_hbm
        self.w_in_scr, self.w_out_scr = w_in_scr, w_out_scr
        self.stage_in, self.stage_out, self.sems = stage_in, stage_out, sems

    def _copies(self, ci):
        slot, lo = ci % 2, ci * FF_CHUNK
        cols = [pl.ds(lo, FF_CHUNK), pl.ds(D_FF + lo, FF_CHUNK)]
        return ([pltpu.make_async_copy(self.w_in_hbm.at[:, c], self.stage_in.at[slot, j],
                                       self.sems.at[slot, j]) for j, c in enumerate(cols)]
                + [pltpu.make_async_copy(self.w_out_hbm.at[pl.ds(lo, FF_CHUNK), :],
                                         self.stage_out.at[slot], self.sems.at[slot, 2])])

    def start(self, ci):
        for copy in self._copies(ci):
            copy.start()

    def finish(self, ci):
        slot, lo = ci % 2, ci * FF_CHUNK
        for copy in self._copies(ci):
            copy.wait()
        self.w_in_scr[:, lo:lo + FF_CHUNK] = self.stage_in[slot, 0].astype(BF16)
        self.w_in_scr[:, D_FF + lo:D_FF + lo + FF_CHUNK] = self.stage_in[slot, 1].astype(BF16)
        self.w_out_scr[lo:lo + FF_CHUNK, :] = self.stage_out[slot].astype(BF16)
        if ci + 2 < D_FF // FF_CHUNK:
            self.start(ci + 2)


def _ffn_stage_kernel(*refs, finish, n_side, n_w, n_out, n_cast, own_weights):
    it = iter(refs)

    def take(k):
        return [next(it) for _ in range(k)]

    (x_ref,), p_side, (xs_ref,), s_side = take(1), take(n_side), take(1), take(n_side)
    weights, cast_src = take(n_w), take(n_cast)
    p_out, s_out, cast_dst, (carry,) = take(n_out), take(n_out), take(n_cast), take(1)
    step = pl.program_id(0)
    sample_step = pl.num_programs(0) - 1
    stream = None
    if own_weights:
        w_in_scr, w_out_scr, stage_in, stage_out, sems = take(5)
        stream = _WeightStream(weights[1], weights[2], w_in_scr, w_out_scr, stage_in, stage_out,
                               sems)
        weights = [weights[0], w_in_scr, w_out_scr] + weights[3:]

    def finish_previous_tile():
        _cast_blocks(cast_src, cast_dst)
        finish(carry[...], p_side, weights, p_out)

    @pl.when(step == 0)
    def _():
        carry[...] = jnp.zeros_like(carry)

    if own_weights:
        @pl.when(step == 0)
        def _():
            stream.start(0)
            stream.start(1)
            carry[...] = _swiglu_residual(x_ref[...], *weights[:3], finish_previous_tile,
                                          before_chunk=stream.finish)

    @pl.when((step < sample_step) & (step > 0) if own_weights else step < sample_step)
    def _():
        carry[...] = _swiglu_residual(x_ref[...], *weights[:3], finish_previous_tile)

    @pl.when(step == sample_step)
    def _():
        x_new = _swiglu_residual(xs_ref[...], *weights[:3], finish_previous_tile)
        finish(x_new, s_side, weights, s_out)


def _ffn1_finish(x1, side, weights, outs):
    gmix_ref, (x1_ref, h_ref) = weights[3], outs
    x1_ref[...] = x1
    h_ref[...] = _rms(x1, gmix_ref[...]).astype(BF16)


def _activated_projection(h, weights, put, put_la):
    wt_ref, wgu_ref, bg_ref, lng_ref, lnb_ref = weights

    def proj(lo, width):
        return _dot_nt(h, wt_ref[lo:lo + width, :])

    lr = proj(LR_LO, LANES).astype(BF16)
    logit = _dot(lr, wgu_ref[...]) + bg_ref[...]
    put_la(_log_sigmoid(logit) * (1.0 / GATE_NORM))
    half = D_MODEL // 2
    segments = [(0, 0, jax.nn.gelu),
                (A_WIDTH, A_WIDTH,
                 lambda z: _layernorm(jax.nn.gelu(z), lng_ref[...], lnb_ref[...])),
                (QK_LO, QK_LO, lambda z: z * (B_DK ** -0.5)),
                (QK_LO + B_KW, QK_LO + B_KW, lambda z: z)]
    segments += [(VB_LO + j * half, VB_LO + j * half, lambda z: z) for j in range(2)]
    segments += [(GATE_LO + j * half, LR_LO + j * half, jax.nn.sigmoid) for j in range(4)]
    segments += [(GB_LO + j * half, GB_LO + j * half, jax.nn.silu) for j in range(2)]
    for w_lo, p_lo, act in segments:
        put(p_lo, act(proj(w_lo, half)))


def _gla_prep(la, qk, ones_tril):
    la_hi = la.astype(BF16)
    la_lo = (la - la_hi.astype(F32)).astype(BF16)
    b = _dot(jnp.concatenate([ones_tril, ones_tril], axis=1),
             jnp.concatenate([la_hi, la_lo], axis=0))
    r0 = b[HALF // 2 - 1:HALF // 2, :]
    r1 = b[HALF + HALF // 2 - 1:HALF + HALF // 2, :]
    b_mid = b[HALF - 1:HALF, :]
    b_end = b[CHUNK - 1:CHUNK, :]
    bq0 = b[:HALF] - r0
    bq1 = b[HALF:] - r1
    qs = qk[:, :B_KW]
    k = qk[:, B_KW:]
    a0 = qs[:HALF] * jnp.exp(bq0)
    a1 = qs[HALF:] * jnp.exp(bq1)
    k0 = k[:HALF] * jnp.exp(-bq0)
    k1 = k[HALF:] * jnp.exp(-bq1)
    q_off1 = a1 * jnp.exp(r1 - b_mid)
    k_off0 = k0 * jnp.exp(b_mid - r0)
    q_int = jnp.concatenate([a0 * jnp.exp(r0), q_off1 * jnp.exp(b_mid)], axis=0)
    k_st = jnp.concatenate([k_off0 * jnp.exp(b_end - b_mid), k1 * jnp.exp(b_end - r1)], axis=0)
    return dict(a0=a0.astype(BF16), a1=a1.astype(BF16), k0=k0.astype(BF16), k1=k1.astype(BF16),
                q_off1=q_off1.astype(BF16), k_off0=k_off0.astype(BF16), q_int=q_int.astype(BF16),
                k_st=k_st, decay=jnp.exp(b_end))


def _mixer_kernel(*refs, n_chunks, n_dec, n_cast):
    it = iter(refs)

    def take(k):
        return [next(it) for _ in range(k)]

    h_ref, x1_ref, d_h_ref, d_state_ref = take(4)
    proj_w = take(5)
    ws_ref, bsp_ref, ggla_ref, wpa_ref, wpb_ref, wo_ref, ws0_ref, bs0_ref = take(8)
    cast_src = take(n_cast)
    x2_ref, sout_ref, d_ya_ref, d_yb_ref, d_sout_ref, d_cv_ref, d_gate_ref = take(7)
    cast_dst = take(n_cast)
    s_scr, o_scr, p_scr, la_ref, d_p_scr, d_la_ref = take(6)
    t = pl.program_id(1)
    tt = h_ref.shape[0]

    @pl.when(t == 0)
    def _():
        s_scr[...] = jnp.zeros_like(s_scr)

    def put(lo, val):
        p_scr[:, lo:lo + val.shape[1]] = val[:tt].astype(BF16)
        d_p_scr[:, lo:lo + val.shape[1]] = val[tt:tt + n_dec]

    def put_la(val):
        la_ref[...] = val[:tt]
        d_la_ref[...] = val[tt:tt + n_dec]

    _activated_projection(jnp.concatenate([h_ref[...], d_h_ref[0]], axis=0), proj_w, put, put_la)
    uv_ref, qk_ref, v_ref, gb_ref, sga_ref, sgb_ref = (
        p_scr.at[:, j * D_MODEL:(j + 1) * D_MODEL] for j in range(6))
    d_uv_ref, d_qk_ref, d_v_ref, d_gb_ref = (
        d_p_scr.at[:, j * D_MODEL:(j + 1) * D_MODEL] for j in range(4))

    row = lax.broadcasted_iota(jnp.int32, (CHUNK, CHUNK), 0)
    col = lax.broadcasted_iota(jnp.int32, (CHUNK, CHUNK), 1)
    causal = row >= col
    ones_tril = causal.astype(BF16)
    ws_tril = [jnp.where(causal, ws_ref[g], 0.0).astype(BF16) for g in range(A_GROUPS)]
    zero_blk = jnp.zeros((HALF, B_DK), BF16)
    chunks = [slice(c * CHUNK, (c + 1) * CHUNK) for c in range(n_chunks)]
    heads = [(slice(h * B_DK, (h + 1) * B_DK), slice(h * B_DV, (h + 1) * B_DV))
             for h in range(B_HEADS)]

    ya = []
    for rows in chunks:
        u = uv_ref[rows, :A_WIDTH].astype(F32)
        vn = uv_ref[rows, A_WIDTH:]
        parts = []
        for g in range(A_GROUPS):
            cols = slice(g * A_GROUP_DIM, (g + 1) * A_GROUP_DIM)
            mixed = _dot(ws_tril[g], vn[:, cols]) + bsp_ref[:, g:g + 1]
            parts.append((u[:, cols] * mixed).astype(BF16))
        ya.append(jnp.concatenate(parts, axis=1))

    prep = [_gla_prep(la_ref[rows, :], qk_ref[rows, :].astype(F32), ones_tril) for rows in chunks]
    scores = []
    for p in prep:
        per_head = []
        for kc, _ in heads:
            top = _dot_nt(p["a0"][:, kc], jnp.concatenate([p["k0"][:, kc], zero_blk], axis=0))
            bot = _dot_nt(
                jnp.concatenate([p["q_off1"][:, kc], p["a1"][:, kc]], axis=1),
                jnp.concatenate([jnp.concatenate([p["k_off0"][:, kc], zero_blk], axis=1),
                                 jnp.concatenate([zero_blk, p["k1"][:, kc]], axis=1)], axis=0))
            s = jnp.concatenate([top, bot], axis=0)
            per_head.append(jnp.where(causal, s, 0.0).astype(BF16))
        scores.append(per_head)

    state = [s_scr[h] for h in range(B_HEADS)]
    state_before = []
    for rows, p in zip(chunks, prep):
        state_before.append([s.astype(BF16) for s in state])
        for h, (kc, vc) in enumerate(heads):
            upd = _dot(p["k_st"][:, kc].T.astype(BF16), v_ref[rows, vc])
            decay_col = jnp.broadcast_to(p["decay"][:, kc], (B_DK, B_DK)).T
            state[h] = state[h] * jnp.concatenate([decay_col, decay_col], axis=1) + upd
    for h in range(B_HEADS):
        s_scr[h] = state[h]

    yb = []
    for c, (rows, p) in enumerate(zip(chunks, prep)):
        gb = gb_ref[rows, :].astype(F32)
        parts = []
        for h, (kc, vc) in enumerate(heads):
            o = _dot(jnp.concatenate([scores[c][h], p["q_int"][:, kc]], axis=1),
                     jnp.concatenate([v_ref[rows, vc], state_before[c][h]], axis=0))
            parts.append((_rms(o, ggla_ref[...]) * gb[:, vc]).astype(BF16))
        yb.append(jnp.concatenate(parts, axis=1))

    pa = _dot(jnp.concatenate(ya, axis=0), wpa_ref[...])
    _cast_blocks(cast_src, cast_dst)
    _sample_mixers(d_uv_ref, d_qk_ref, d_v_ref, d_gb_ref, d_la_ref, d_state_ref, ws0_ref, bs0_ref,
                   ggla_ref, d_ya_ref.at[0], d_yb_ref.at[0], d_sout_ref, d_cv_ref.at[0], o_scr,
                   n_dec)
    d_gate_ref[0] = d_p_scr[:, LR_LO:]
    mix = (sga_ref[...].astype(F32) * pa
           + sgb_ref[...].astype(F32) * _dot(jnp.concatenate(yb, axis=0), wpb_ref[...]))
    x2_ref[...] = x1_ref[...] + _dot(mix.astype(BF16), wo_ref[...])

    @pl.when(t == pl.num_programs(1) - 1)
    def _():
        sout_ref[0] = s_scr[...]


def _sample_mixers(uv_ref, qk_ref, v_ref, gb_ref, la_ref, s_ref, ws0_ref, bs0_ref, ggla_ref,
                   ya_ref, yb_ref, sout_ref, cv_ref, o_scr, nb):
    vn = uv_ref[:, A_WIDTH:]
    cv_ref[...] = vn
    ya_ref[...] = uv_ref[:, :A_WIDTH] * (vn * ws0_ref[...] + bs0_ref[...])

    a = jnp.exp(la_ref[...])
    qk = qk_ref[...]
    qs = qk[:, :B_KW]
    k = qk[:, B_KW:]
    v = v_ref[...]
    pad = jnp.zeros((LANES - 3 * nb, B_DK), F32)
    for h in range(B_HEADS):
        kc = slice(h * B_DK, (h + 1) * B_DK)
        vc = slice(h * B_DV, (h + 1) * B_DV)
        xt = jnp.concatenate([a[:, kc], k[:, kc], qs[:, kc], pad], axis=0).T
        for n in range(nb):
            s_new = (s_ref[n, h] * xt[:, n:n + 1]
                     + xt[:, nb + n:nb + n + 1] * v[n:n + 1, vc])
            sout_ref[n, h] = s_new
            o_scr[n:n + 1, vc] = jnp.sum(xt[:, 2 * nb + n:2 * nb + n + 1] * s_new,
                                         axis=0, keepdims=True)
    gb = gb_ref[...]
    for h in range(B_HEADS):
        vc = slice(h * B_DV, (h + 1) * B_DV)
        yb_ref[:, vc] = _rms(o_scr[:, vc], ggla_ref[...]) * gb[:, vc]


def _merge_kernel(ya_ref, yb_ref, ga_ref, gb_ref, x1_ref, wpa_ref, wpb_ref, wo_ref, x2_ref):
    mix = (ga_ref[...].astype(F32) * _dot(ya_ref[...], wpa_ref[...])
           + gb_ref[...].astype(F32) * _dot(yb_ref[...], wpb_ref[...]))
    x2_ref[...] = x1_ref[...] + _dot(mix.astype(BF16), wo_ref[...])


def _ffn2_finish(x3, side, weights, outs, *, final_norm):
    (p_ref,), (y_ref,) = side, outs
    gple_ref, wpg_ref, wple_ref, gfin_ref = weights[3:]
    gate = jax.nn.sigmoid(_dot(_rms(x3, gple_ref[...]).astype(BF16), wpg_ref[...]))
    x4 = x3 + _dot(p_ref[...].astype(BF16), wple_ref[...]) * gate
    y_ref[...] = _rms(x4, gfin_ref[...]) if final_norm else x4


def _resident(shape):
    zeros = (0,) * len(shape)
    return pl.BlockSpec(shape, lambda *_: zeros, pipeline_mode=pl.Buffered(1))


def _rows(tm, width, colblk=0):
    return pl.BlockSpec((tm, width), lambda i: (i, colblk))


def _params(n_axes):
    return pltpu.CompilerParams(dimension_semantics=("arbitrary",) * n_axes,
                                vmem_limit_bytes=VMEM_LIMIT)


def _cast_specs(weights, n_steps, linear_step=lambda i: i):
    specs, shapes = [], []
    for w in weights:
        rows, cols = w.shape
        n_blocks = max(n for n in range(1, n_steps + 1) if rows % (n * BF16_SUBLANES) == 0)
        specs.append(pl.BlockSpec(
            (rows // n_blocks, cols),
            lambda *idx, n=n_blocks: (jnp.minimum(linear_step(*idx), n - 1), 0)))
        shapes.append(jax.ShapeDtypeStruct(w.shape, BF16))
    return specs, shapes


def _ffn_stage_call(name, finish, x, x_s, side, side_s, weights, outputs, to_cast=(),
                    own_weights=False):
    m, ms = x.shape[0], x_s.shape[0]
    assert m % TOKEN_TILE == 0
    n = m // TOKEN_TILE
    weight_specs = [_resident(w.shape) for w in weights]
    scratch = [pltpu.VMEM((TOKEN_TILE, D_MODEL), F32)]
    if own_weights:
        w_in, w_out = weights[1], weights[2]
        weight_specs[1] = weight_specs[2] = pl.BlockSpec(memory_space=pl.ANY)
        scratch += [pltpu.VMEM(w_in.shape, BF16), pltpu.VMEM(w_out.shape, BF16),
                    pltpu.VMEM((2, 2, w_in.shape[0], FF_CHUNK), F32),
                    pltpu.VMEM((2, FF_CHUNK, w_out.shape[1]), F32),
                    pltpu.SemaphoreType.DMA((2, 3))]

    def tile(cols, lag):
        return pl.BlockSpec((TOKEN_TILE, cols), lambda i: (jnp.clip(i - lag, 0, n - 1), 0))

    def s_spec(cols):
        return pl.BlockSpec((ms, cols), lambda i: (0, 0))

    cast_specs, cast_shapes = _cast_specs(to_cast, n)
    outs = pl.pallas_call(
        functools.partial(_ffn_stage_kernel, finish=finish, n_side=len(side), n_w=len(weights),
                          n_out=len(outputs), n_cast=len(to_cast), own_weights=own_weights),
        grid=(n + 1,),
        in_specs=[tile(D_MODEL, 0)] + [tile(a.shape[1], 1) for a in side]
                 + [s_spec(D_MODEL)] + [s_spec(a.shape[1]) for a in side_s]
                 + weight_specs + cast_specs,
        out_specs=[tile(c, 1) for c, _ in outputs] + [s_spec(c) for c, _ in outputs] + cast_specs,
        out_shape=[jax.ShapeDtypeStruct((m, c), dt) for c, dt in outputs]
                  + [jax.ShapeDtypeStruct((ms, c), dt) for c, dt in outputs] + cast_shapes,
        scratch_shapes=scratch,
        compiler_params=_params(1),
        name=name,
    )(x, *side, x_s, *side_s, *weights, *to_cast)
    k = len(outputs)
    return outs[:k], outs[k:2 * k], outs[2 * k:]


def _mixer(h, x1, n_seq, seq, d_h, d_state, proj_w, ws, bsp_t, ggla, wpa, wpb, wo, ws0, bs0,
           to_cast=()):
    tt = MIXER_TILE
    nt = seq // tt
    m = n_seq * seq
    steps = n_seq * nt
    n_all = d_h.shape[0]
    assert n_all % steps == 0
    n_dec = n_all // steps
    assert n_dec <= BF16_SUBLANES
    d_h3 = jnp.pad(d_h.reshape(steps, n_dec, D_MODEL), ((0, 0), (0, BF16_SUBLANES - n_dec), (0, 0)))

    def seg(colblk, width=D_MODEL):
        return pl.BlockSpec((tt, width), lambda b, t: (b * nt + t, colblk))

    def d_seg(colblk, width=D_MODEL):
        return pl.BlockSpec((1, n_dec, width), lambda b, t: (b * nt + t, 0, colblk))

    d_state_spec = pl.BlockSpec((n_dec, B_HEADS, B_DK, B_DV), lambda b, t: (b * nt + t, 0, 0, 0))
    d_h_spec = pl.BlockSpec((1, BF16_SUBLANES, D_MODEL), lambda b, t: (b * nt + t, 0, 0))
    small = list(proj_w) + [ws, bsp_t, ggla, wpa, wpb, wo, ws0, bs0]
    cast_specs, cast_shapes = _cast_specs(to_cast, steps, lambda b, t: b * nt + t)
    x2, sp, ya, yb, ss, cv, gates, *cast = pl.pallas_call(
        functools.partial(_mixer_kernel, n_chunks=tt // CHUNK, n_dec=n_dec, n_cast=len(to_cast)),
        grid=(n_seq, nt),
        in_specs=[seg(0), seg(0), d_h_spec, d_state_spec]
                 + [_resident(w.shape) for w in small] + cast_specs,
        out_specs=[seg(0),
                   pl.BlockSpec((1, B_HEADS, B_DK, B_DV), lambda b, t: (b, 0, 0, 0)),
                   d_seg(0, A_WIDTH), d_seg(0, B_VW), d_state_spec, d_seg(0, A_WIDTH),
                   d_seg(0, 2 * D_MODEL)] + cast_specs,
        out_shape=[jax.ShapeDtypeStruct((m, D_MODEL), F32),
                   jax.ShapeDtypeStruct((n_seq, B_HEADS, B_DK, B_DV), F32),
                   jax.ShapeDtypeStruct((steps, n_dec, A_WIDTH), F32),
                   jax.ShapeDtypeStruct((steps, n_dec, B_VW), F32),
                   jax.ShapeDtypeStruct(d_state.shape, F32),
                   jax.ShapeDtypeStruct((steps, n_dec, A_WIDTH), F32),
                   jax.ShapeDtypeStruct((steps, n_dec, 2 * D_MODEL), F32)] + cast_shapes,
        scratch_shapes=[pltpu.VMEM((B_HEADS, B_DK, B_DV), F32), pltpu.VMEM((n_dec, B_VW), F32),
                        pltpu.VMEM((tt, P_COLS), BF16), pltpu.VMEM((tt, B_KW), F32),
                        pltpu.VMEM((n_dec, P_COLS), F32), pltpu.VMEM((n_dec, B_KW), F32)],
        compiler_params=_params(2),
        name="mixer",
    )(h, x1, d_h3, d_state, *small, *to_cast)
    return (x2, sp, ya.reshape(n_all, A_WIDTH), yb.reshape(n_all, B_VW), ss,
            cv.reshape(n_all, A_WIDTH), gates.reshape(n_all, 2 * D_MODEL), cast)


def _merge(ya, yb, gates, x1, wpa, wpb, wo):
    tm = m = x1.shape[0]
    return pl.pallas_call(
        _merge_kernel,
        grid=(m // tm,),
        in_specs=[_rows(tm, A_WIDTH), _rows(tm, B_VW), _rows(tm, D_MODEL, 0),
                  _rows(tm, D_MODEL, 1), _rows(tm, D_MODEL), _resident(wpa.shape),
                  _resident(wpb.shape), _resident(wo.shape)],
        out_specs=_rows(tm, D_MODEL),
        out_shape=jax.ShapeDtypeStruct((m, D_MODEL), F32),
        compiler_params=_params(1),
        name="merge",
    )(ya, yb, gates, gates, x1, wpa, wpb, wo)


def kernel(x_prompt, x_sample, p_prompt, p_sample, state_gla, g_ffn1, w_ffn1_in, w_ffn1_out, g_mix,
           w_in, ln_v_g, ln_v_b, w_spatial, b_spatial, w_gate_up, b_gate, g_gla_out, w_proj_a,
           w_proj_b, w_out, g_ffn2, w_ffn2_in, w_ffn2_out, g_ple, w_ple_gate, w_ple, g_final):
    depth = w_in.shape[0]
    n_seq, seq, _ = x_prompt.shape
    n_dec, dec_seq, _ = x_sample.shape
    assert dec_seq == 1 and seq % MIXER_TILE == 0

    def row(vec):
        return vec.reshape(1, -1).astype(F32)

    assert sum(IN_SIZES[:6]) == LR_LO and sum(IN_SIZES) == GATE_LO + 2 * D_MODEL
    xp = x_prompt.reshape(n_seq * seq, D_MODEL)
    xs = x_sample.reshape(n_dec, D_MODEL)
    gfin = row(g_final)
    sp_list, ss_list, vs_list = [], [], []
    for i in range(depth):
        wgu =jnp.pad(w_gate_up[i], ((0, LANES - GATE_RANK), (0, 0))).astype(BF16)
        g1, gmix, g2, gple = row(g_ffn1[i]), row(g_mix[i]), row(g_ffn2[i]), row(g_ple[i])
        lng, lnb, bg, ggla = row(ln_v_g[i]), row(ln_v_b[i]), row(b_gate[i]), row(g_gla_out[i])
        ws0 = row(jnp.repeat(w_spatial[i][:, 0, 0], A_GROUP_DIM))
        bs0 = row(jnp.repeat(b_spatial[i][:, 0], A_GROUP_DIM))

        wide, act = (D_MODEL, F32), (D_MODEL, BF16)
        (x1p, hp), (x1s, hs), (w_inb, wpa, wpb, wo) = _ffn_stage_call(
            "ffn1", _ffn1_finish, xp, xs, [], [], [g1, w_ffn1_in[i], w_ffn1_out[i], gmix],
            [wide, act], to_cast=[w_in[i].T, w_proj_a[i], w_proj_b[i], w_out[i]], own_weights=True)
        x2p, sp, ya, yb, ss, cv, gates, (w2i, w2o, wpg, wple) = _mixer(
            hp, x1p, n_seq, seq, hs, state_gla[i], [w_inb, wgu, bg, lng, lnb], w_spatial[i],
            b_spatial[i].T, ggla, wpa, wpb, wo, ws0, bs0,
            to_cast=[w_ffn2_in[i], w_ffn2_out[i], w_ple_gate[i], w_ple[i]])
        x2s = _merge(ya.astype(BF16), yb.astype(BF16), gates, x1s, wpa, wpb, wo)

        ple_p = p_prompt[i].reshape(n_seq * seq, PLE_DIM)
        ple_s = p_sample[i].reshape(n_dec, PLE_DIM)
        (xp,), (xs,), _ = _ffn_stage_call(
            "ffn2", functools.partial(_ffn2_finish, final_norm=i == depth - 1), x2p, x2s,
            [ple_p], [ple_s], [g2, w2i, w2o, gple, wpg, wple, gfin], [wide])

        sp_list.append(sp)
        ss_list.append(ss)
        vs_list.append(cv.reshape(n_dec, dec_seq, A_WIDTH))
    return (xp.reshape(n_seq, seq, D_MODEL), xs.reshape(n_dec, dec_seq, D_MODEL),
            jnp.stack(sp_list), jnp.stack(ss_list), jnp.stack(vs_list))
```

```python
import functools

import jax
import jax.numpy as jnp
from jax import lax
from jax.experimental import pallas as pl
from jax.experimental.pallas import tpu as pltpu

D_MODEL = 1024
D_FF = 2816
PLE_DIM = 256
CHUNK = 128
A_GROUPS = 4
A_GROUP_DIM = 128
A_WIDTH = A_GROUPS * A_GROUP_DIM
B_HEADS = 4
B_DK = 128
B_DV = 256
B_KW = B_HEADS * B_DK
B_VW = B_HEADS * B_DV
GATE_RANK = 16
GATE_NORM = 16.0
EPS = 1e-6
IN_SIZES = (A_WIDTH, A_WIDTH, B_KW, B_KW, B_VW, B_VW, GATE_RANK, D_MODEL, D_MODEL)

LANES = 128
P_COLS = 6 * D_MODEL
QK_LO = 2 * A_WIDTH
VB_LO = QK_LO + 2 * B_KW
GB_LO = VB_LO + B_VW
LR_LO = GB_LO + B_VW
GATE_LO = LR_LO + GATE_RANK
FF_CHUNK = 256
HALF = CHUNK // 2
MIXER_TILE = 4 * CHUNK
TOKEN_TILE = 512
BF16_SUBLANES = 16
VMEM_LIMIT = 56 * 1024 * 1024

F32 = jnp.float32
BF16 = jnp.bfloat16


def _dot(a, b):
    return jnp.dot(a, b, preferred_element_type=F32)


def _dot_nt(a, b):
    return lax.dot_general(a, b, (((1,), (1,)), ((), ())), preferred_element_type=F32)


def _rms(x, g):
    return x * lax.rsqrt(jnp.mean(x * x, axis=-1, keepdims=True) + EPS) * g


def _log_sigmoid(x):
    return jnp.minimum(x, 0.0) - jnp.log1p(jnp.exp(-jnp.abs(x)))


def _layernorm(x, g, b):
    mu = jnp.mean(x, axis=-1, keepdims=True)
    xc = x - mu
    var = jnp.mean(xc * xc, axis=-1, keepdims=True)
    return xc * lax.rsqrt(var + EPS) * g + b


def _swiglu_residual(x, g_ref, w_in_ref, w_out_ref, early, before_chunk=None):
    h = _rms(x, g_ref[...]).astype(BF16)
    acts = []
    for ci, lo in enumerate(range(0, D_FF, FF_CHUNK)):
        if before_chunk is not None:
            before_chunk(ci)
        gate = _dot(h, w_in_ref[:, lo:lo + FF_CHUNK])
        up = _dot(h, w_in_ref[:, D_FF + lo:D_FF + lo + FF_CHUNK])
        acts.append((jax.nn.silu(gate) * up).astype(BF16))
        if ci == 1:
            early()
    return x + 0.5 * _dot(jnp.concatenate(acts, axis=1), w_out_ref[...])


def _cast_blocks(src_refs, dst_refs):
    for src, dst in zip(src_refs, dst_refs):
        dst[...] = src[...].astype(dst.dtype)


class _WeightStream:
    def __init__(self, w_in_hbm, w_out_hbm, w_in_scr, w_out_scr, stage_in, stage_out, sems):
        self.w_in_hbm, self.w_out_hbm = w_in_hbm, w_out_hbm
        self.w_in_scr, self.w_out_scr = w_in_scr, w_out_scr
        self.stage_in, self.stage_out, self.sems = stage_in, stage_out, sems

    def _copies(self, ci):
        slot, lo = ci % 2, ci * FF_CHUNK
        cols = [pl.ds(lo, FF_CHUNK), pl.ds(D_FF + lo, FF_CHUNK)]
        return ([pltpu.make_async_copy(self.w_in_hbm.at[:, c], self.stage_in.at[slot, j],
                                       self.sems.at[slot, j]) for j, c in enumerate(cols)]
                + [pltpu.make_async_copy(self.w_out_hbm.at[pl.ds(lo, FF_CHUNK), :],
                                         self.stage_out.at[slot], self.sems.at[slot, 2])])

    def start(self, ci):
        for copy in self._copies(ci):
            copy.start()

    def finish(self, ci):
        slot, lo = ci % 2, ci * FF_CHUNK
        for copy in self._copies(ci):
            copy.wait()
        self.w_in_scr[:, lo:lo + FF_CHUNK] = self.stage_in[slot, 0].astype(BF16)
        self.w_in_scr[:, D_FF + lo:D_FF + lo + FF_CHUNK] = self.stage_in[slot, 1].astype(BF16)
        self.w_out_scr[lo:lo + FF_CHUNK, :] = self.stage_out[slot].astype(BF16)
        if ci + 2 < D_FF // FF_CHUNK:
            self.start(ci + 2)


def _ffn_stage_kernel(*refs, finish, n_side, n_w, n_out, n_cast, own_weights, defer):
    it = iter(refs)

    def take(k):
        return [next(it) for _ in range(k)]

    (x_ref,), p_side, (xs_ref,), s_side = take(1), take(n_side), take(1), take(n_side)
    weights, cast_src = take(n_w), take(n_cast)
    p_out, s_out, cast_dst = take(n_out), take(n_out), take(n_cast)
    carry = next(it) if defer else None
    step = pl.program_id(0)
    sample_step = pl.num_programs(0) - 1
    stream = None
    if own_weights:
        w_in_scr, w_out_scr, stage_in, stage_out, sems = take(5)
        stream = _WeightStream(weights[1], weights[2], w_in_scr, w_out_scr, stage_in, stage_out,
                               sems)
        weights = [weights[0], w_in_scr, w_out_scr] + weights[3:]

    def finish_previous_tile():
        _cast_blocks(cast_src, cast_dst)
        if defer:
            finish(carry[...], p_side, weights, p_out)

    def prompt_tile(before_chunk=None):
        x_new = _swiglu_residual(x_ref[...], *weights[:3], finish_previous_tile, before_chunk)
        if defer:
            carry[...] = x_new
        else:
            finish(x_new, p_side, weights, p_out)

    if defer:
        @pl.when(step == 0)
        def _():
            carry[...] = jnp.zeros_like(carry)

    if own_weights:
        @pl.when(step == 0)
        def _():
            stream.start(0)
            stream.start(1)
            prompt_tile(before_chunk=stream.finish)

    @pl.when((step < sample_step) & (step > 0) if own_weights else step < sample_step)
    def _():
        prompt_tile()

    @pl.when(step == sample_step)
    def _():
        x_new = _swiglu_residual(xs_ref[...], *weights[:3], finish_previous_tile)
        finish(x_new, s_side, weights, s_out)


def _ffn1_finish(x1, side, weights, outs):
    gmix_ref, (x1_ref, h_ref) = weights[3], outs
    x1_ref[...] = x1
    h_ref[...] = _rms(x1, gmix_ref[...]).astype(BF16)


def _activated_projection(h, weights, put, put_la):
    wt_ref, wgu_ref, bg_ref, lng_ref, lnb_ref = weights

    def proj(lo, width):
        return _dot_nt(h, wt_ref[lo:lo + width, :])

    lr = proj(LR_LO, LANES).astype(BF16)
    logit = _dot(lr, wgu_ref[...]) + bg_ref[...]
    put_la(_log_sigmoid(logit) * (1.0 / GATE_NORM))
    half = D_MODEL // 2
    segments = [(0, 0, jax.nn.gelu),
                (A_WIDTH, A_WIDTH,
                 lambda z: _layernorm(jax.nn.gelu(z), lng_ref[...], lnb_ref[...])),
                (QK_LO, QK_LO, lambda z: z * (B_DK ** -0.5)),
                (QK_LO + B_KW, QK_LO + B_KW, lambda z: z)]
    segments += [(VB_LO + j * half, VB_LO + j * half, lambda z: z) for j in range(2)]
    segments += [(GATE_LO + j * half, LR_LO + j * half, jax.nn.sigmoid) for j in range(4)]
    segments += [(GB_LO + j * half, GB_LO + j * half, jax.nn.silu) for j in range(2)]
    for w_lo, p_lo, act in segments:
        put(p_lo, act(proj(w_lo, half)))


def _gla_prep(la, qk, ones_tril):
    la_hi = la.astype(BF16)
    la_lo = (la - la_hi.astype(F32)).astype(BF16)
    b = _dot(jnp.concatenate([ones_tril, ones_tril], axis=1),
             jnp.concatenate([la_hi, la_lo], axis=0))
    r0 = b[HALF // 2 - 1:HALF // 2, :]
    r1 = b[HALF + HALF // 2 - 1:HALF + HALF // 2, :]
    b_mid = b[HALF - 1:HALF, :]
    b_end = b[CHUNK - 1:CHUNK, :]
    bq0 = b[:HALF] - r0
    bq1 = b[HALF:] - r1
    qs = qk[:, :B_KW]
    k = qk[:, B_KW:]
    a0 = qs[:HALF] * jnp.exp(bq0)
    a1 = qs[HALF:] * jnp.exp(bq1)
    k0 = k[:HALF] * jnp.exp(-bq0)
    k1 = k[HALF:] * jnp.exp(-bq1)
    q_off1 = a1 * jnp.exp(r1 - b_mid)
    k_off0 = k0 * jnp.exp(b_mid - r0)
    q_int = jnp.concatenate([a0 * jnp.exp(r0), q_off1 * jnp.exp(b_mid)], axis=0)
    k_st = jnp.concatenate([k_off0 * jnp.exp(b_end - b_mid), k1 * jnp.exp(b_end - r1)], axis=0)
    return dict(a0=a0.astype(BF16), a1=a1.astype(BF16), k0=k0.astype(BF16), k1=k1.astype(BF16),
                q_off1=q_off1.astype(BF16), k_off0=k_off0.astype(BF16), q_int=q_int.astype(BF16),
                k_st=k_st, decay=jnp.exp(b_end))


def _mixer_kernel(*refs, n_chunks, n_dec, n_cast):
    it = iter(refs)

    def take(k):
        return [next(it) for _ in range(k)]

    h_ref, x1_ref, d_h_ref, d_state_ref = take(4)
    proj_w = take(5)
    ws_ref, bsp_ref, ggla_ref, wpa_ref, wpb_ref, wo_ref, ws0_ref, bs0_ref = take(8)
    cast_src = take(n_cast)
    x2_ref, sout_ref, d_ya_ref, d_yb_ref, d_sout_ref, d_cv_ref, d_gate_ref = take(7)
    cast_dst = take(n_cast)
    s_scr, o_scr, p_scr, la_ref, d_p_scr, d_la_ref = take(6)
    t = pl.program_id(1)
    tt = h_ref.shape[0]

    @pl.when(t == 0)
    def _():
        s_scr[...] = jnp.zeros_like(s_scr)

    def put(lo, val):
        p_scr[:, lo:lo + val.shape[1]] = val[:tt].astype(BF16)
        d_p_scr[:, lo:lo + val.shape[1]] = val[tt:tt + n_dec]

    def put_la(val):
        la_ref[...] = val[:tt]
        d_la_ref[...] = val[tt:tt + n_dec]

    _activated_projection(jnp.concatenate([h_ref[...], d_h_ref[0]], axis=0), proj_w, put, put_la)
    uv_ref, qk_ref, v_ref, gb_ref, sga_ref, sgb_ref = (
        p_scr.at[:, j * D_MODEL:(j + 1) * D_MODEL] for j in range(6))
    d_uv_ref, d_qk_ref, d_v_ref, d_gb_ref = (
        d_p_scr.at[:, j * D_MODEL:(j + 1) * D_MODEL] for j in range(4))

    row = lax.broadcasted_iota(jnp.int32, (CHUNK, CHUNK), 0)
    col = lax.broadcasted_iota(jnp.int32, (CHUNK, CHUNK), 1)
    causal = row >= col
    ones_tril = causal.astype(BF16)
    ws_tril = [jnp.where(causal, ws_ref[g], 0.0).astype(BF16) for g in range(A_GROUPS)]
    zero_blk = jnp.zeros((HALF, B_DK), BF16)
    chunks = [slice(c * CHUNK, (c + 1) * CHUNK) for c in range(n_chunks)]
    heads = [(slice(h * B_DK, (h + 1) * B_DK), slice(h * B_DV, (h + 1) * B_DV))
             for h in range(B_HEADS)]

    ya = []
    for rows in chunks:
        u = uv_ref[rows, :A_WIDTH].astype(F32)
        vn = uv_ref[rows, A_WIDTH:]
        parts = []
        for g in range(A_GROUPS):
            cols = slice(g * A_GROUP_DIM, (g + 1) * A_GROUP_DIM)
            mixed = _dot(ws_tril[g], vn[:, cols]) + bsp_ref[:, g:g + 1]
            parts.append((u[:, cols] * mixed).astype(BF16))
        ya.append(jnp.concatenate(parts, axis=1))

    prep = [_gla_prep(la_ref[rows, :], qk_ref[rows, :].astype(F32), ones_tril) for rows in chunks]
    scores = []
    for p in prep:
        per_head = []
        for kc, _ in heads:
            top = _dot_nt(p["a0"][:, kc], jnp.concatenate([p["k0"][:, kc], zero_blk], axis=0))
            bot = _dot_nt(
                jnp.concatenate([p["q_off1"][:, kc], p["a1"][:, kc]], axis=1),
                jnp.concatenate([jnp.concatenate([p["k_off0"][:, kc], zero_blk], axis=1),
                                 jnp.concatenate([zero_blk, p["k1"][:, kc]], axis=1)], axis=0))
            s = jnp.concatenate([top, bot], axis=0)
            per_head.append(jnp.where(causal, s, 0.0).astype(BF16))
        scores.append(per_head)

    state = [s_scr[h] for h in range(B_HEADS)]
    state_before = []
    for rows, p in zip(chunks, prep):
        state_before.append([s.astype(BF16) for s in state])
        for h, (kc, vc) in enumerate(heads):
            upd = _dot(p["k_st"][:, kc].T.astype(BF16), v_ref[rows, vc])
            decay_col = jnp.broadcast_to(p["decay"][:, kc], (B_DK, B_DK)).T
            state[h] = state[h] * jnp.concatenate([decay_col, decay_col], axis=1) + upd
    for h in range(B_HEADS):
        s_scr[h] = state[h]

    yb = []
    for c, (rows, p) in enumerate(zip(chunks, prep)):
        gb = gb_ref[rows, :].astype(F32)
        parts = []
        for h, (kc, vc) in enumerate(heads):
            o = _dot(jnp.concatenate([scores[c][h], p["q_int"][:, kc]], axis=1),
                     jnp.concatenate([v_ref[rows, vc], state_before[c][h]], axis=0))
            parts.append((_rms(o, ggla_ref[...]) * gb[:, vc]).astype(BF16))
        yb.append(jnp.concatenate(parts, axis=1))

    pa = _dot(jnp.concatenate(ya, axis=0), wpa_ref[...])
    _cast_blocks(cast_src, cast_dst)
    _sample_mixers(d_uv_ref, d_qk_ref, d_v_ref, d_gb_ref, d_la_ref, d_state_ref, ws0_ref, bs0_ref,
                   ggla_ref, d_ya_ref.at[0], d_yb_ref.at[0], d_sout_ref, d_cv_ref.at[0], o_scr,
                   n_dec)
    d_gate_ref[0] = d_p_scr[:, LR_LO:]
    mix = (sga_ref[...].astype(F32) * pa
           + sgb_ref[...].astype(F32) * _dot(jnp.concatenate(yb, axis=0), wpb_ref[...]))
    x2_ref[...] = x1_ref[...] + _dot(mix.astype(BF16), wo_ref[...])

    @pl.when(t == pl.num_programs(1) - 1)
    def _():
        sout_ref[0] = s_scr[...]


def _sample_mixers(uv_ref, qk_ref, v_ref, gb_ref, la_ref, s_ref, ws0_ref, bs0_ref, ggla_ref,
                   ya_ref, yb_ref, sout_ref, cv_ref, o_scr, nb):
    vn = uv_ref[:, A_WIDTH:]
    cv_ref[...] = vn
    ya_ref[...] = uv_ref[:, :A_WIDTH] * (vn * ws0_ref[...] + bs0_ref[...])

    a = jnp.exp(la_ref[...])
    qk = qk_ref[...]
    qs = qk[:, :B_KW]
    k = qk[:, B_KW:]
    v = v_ref[...]
    pad = jnp.zeros((LANES - 3 * nb, B_DK), F32)
    for h in range(B_HEADS):
        kc = slice(h * B_DK, (h + 1) * B_DK)
        vc = slice(h * B_DV, (h + 1) * B_DV)
        xt = jnp.concatenate([a[:, kc], k[:, kc], qs[:, kc], pad], axis=0).T
        for n in range(nb):
            s_new = (s_ref[n, h] * xt[:, n:n + 1]
                     + xt[:, nb + n:nb + n + 1] * v[n:n + 1, vc])
            sout_ref[n, h] = s_new
            o_scr[n:n + 1, vc] = jnp.sum(xt[:, 2 * nb + n:2 * nb + n + 1] * s_new,
                                         axis=0, keepdims=True)
    gb = gb_ref[...]
    for h in range(B_HEADS):
        vc = slice(h * B_DV, (h + 1) * B_DV)
        yb_ref[:, vc] = _rms(o_scr[:, vc], ggla_ref[...]) * gb[:, vc]


def _merge_kernel(ya_ref, yb_ref, ga_ref, gb_ref, x1_ref, wpa_ref, wpb_ref, wo_ref, x2_ref):
    mix = (ga_ref[...].astype(F32) * _dot(ya_ref[...], wpa_ref[...])
           + gb_ref[...].astype(F32) * _dot(yb_ref[...], wpb_ref[...]))
    x2_ref[...] = x1_ref[...] + _dot(mix.astype(BF16), wo_ref[...])


def _ffn2_finish(x3, side, weights, outs, *, final_norm):
    (p_ref,), (y_ref,) = side, outs
    gple_ref, wpg_ref, wple_ref, gfin_ref = weights[3:]
    gate = jax.nn.sigmoid(_dot(_rms(x3, gple_ref[...]).astype(BF16), wpg_ref[...]))
    x4 = x3 + _dot(p_ref[...].astype(BF16), wple_ref[...]) * gate
    y_ref[...] = _rms(x4, gfin_ref[...]) if final_norm else x4


def _resident(shape):
    zeros = (0,) * len(shape)
    return pl.BlockSpec(shape, lambda *_: zeros, pipeline_mode=pl.Buffered(1))


def _rows(tm, width, colblk=0):
    return pl.BlockSpec((tm, width), lambda i: (i, colblk))


def _params(n_axes):
    return pltpu.CompilerParams(dimension_semantics=("arbitrary",) * n_axes,
                                vmem_limit_bytes=VMEM_LIMIT)


def _cast_specs(weights, n_steps, linear_step=lambda i: i):
    specs, shapes = [], []
    for w in weights:
        rows, cols = w.shape
        n_blocks = max(n for n in range(1, n_steps + 1) if rows % (n * BF16_SUBLANES) == 0)
        specs.append(pl.BlockSpec(
            (rows // n_blocks, cols),
            lambda *idx, n=n_blocks: (jnp.minimum(linear_step(*idx), n - 1), 0)))
        shapes.append(jax.ShapeDtypeStruct(w.shape, BF16))
    return specs, shapes


def _ffn_stage_call(name, finish, x, x_s, side, side_s, weights, outputs, to_cast=(),
                    own_weights=False, defer=True):
    m, ms = x.shape[0], x_s.shape[0]
    assert m % TOKEN_TILE == 0
    n = m // TOKEN_TILE
    weight_specs = [_resident(w.shape) for w in weights]
    scratch = [pltpu.VMEM((TOKEN_TILE, D_MODEL), F32)] if defer else []
    lag = 1 if defer else 0
    if own_weights:
        w_in, w_out = weights[1], weights[2]
        weight_specs[1] = weight_specs[2] = pl.BlockSpec(memory_space=pl.ANY)
        scratch += [pltpu.VMEM(w_in.shape, BF16), pltpu.VMEM(w_out.shape, BF16),
                    pltpu.VMEM((2, 2, w_in.shape[0], FF_CHUNK), F32),
                    pltpu.VMEM((2, FF_CHUNK, w_out.shape[1]), F32),
                    pltpu.SemaphoreType.DMA((2, 3))]

    def tile(cols, lag):
        return pl.BlockSpec((TOKEN_TILE, cols), lambda i: (jnp.clip(i - lag, 0, n - 1), 0))

    def s_spec(cols):
        return pl.BlockSpec((ms, cols), lambda i: (0, 0))

    cast_specs, cast_shapes = _cast_specs(to_cast, n)
    outs = pl.pallas_call(
        functools.partial(_ffn_stage_kernel, finish=finish, n_side=len(side), n_w=len(weights),
                          n_out=len(outputs), n_cast=len(to_cast), own_weights=own_weights,
                          defer=defer),
        grid=(n + 1,),
        in_specs=[tile(D_MODEL, 0)] + [tile(a.shape[1], lag) for a in side]
                 + [s_spec(D_MODEL)] + [s_spec(a.shape[1]) for a in side_s]
                 + weight_specs + cast_specs,
        out_specs=[tile(c, lag) for c, _ in outputs] + [s_spec(c) for c, _ in outputs] + cast_specs,
        out_shape=[jax.ShapeDtypeStruct((m, c), dt) for c, dt in outputs]
                  + [jax.ShapeDtypeStruct((ms, c), dt) for c, dt in outputs] + cast_shapes,
        scratch_shapes=scratch,
        compiler_params=_params(1),
        name=name,
    )(x, *side, x_s, *side_s, *weights, *to_cast)
    k = len(outputs)
    return outs[:k], outs[k:2 * k], outs[2 * k:]


def _mixer(h, x1, n_seq, seq, d_h, d_state, proj_w, ws, bsp_t, ggla, wpa, wpb, wo, ws0, bs0,
           to_cast=()):
    tt = MIXER_TILE
    nt = seq // tt
    m = n_seq * seq
    steps = n_seq * nt
    n_all = d_h.shape[0]
    assert n_all % steps == 0
    n_dec = n_all // steps
    assert n_dec <= BF16_SUBLANES
    d_h3 = jnp.pad(d_h.reshape(steps, n_dec, D_MODEL), ((0, 0), (0, BF16_SUBLANES - n_dec), (0, 0)))

    def seg(colblk, width=D_MODEL):
        return pl.BlockSpec((tt, width), lambda b, t: (b * nt + t, colblk))

    def d_seg(colblk, width=D_MODEL):
        return pl.BlockSpec((1, n_dec, width), lambda b, t: (b * nt + t, 0, colblk))

    d_state_spec = pl.BlockSpec((n_dec, B_HEADS, B_DK, B_DV), lambda b, t: (b * nt + t, 0, 0, 0))
    d_h_spec = pl.BlockSpec((1, BF16_SUBLANES, D_MODEL), lambda b, t: (b * nt + t, 0, 0))
    small = list(proj_w) + [ws, bsp_t, ggla, wpa, wpb, wo, ws0, bs0]
    cast_specs, cast_shapes = _cast_specs(to_cast, steps, lambda b, t: b * nt + t)
    x2, sp, ya, yb, ss, cv, gates, *cast = pl.pallas_call(
        functools.partial(_mixer_kernel, n_chunks=tt // CHUNK, n_dec=n_dec, n_cast=len(to_cast)),
        grid=(n_seq, nt),
        in_specs=[seg(0), seg(0), d_h_spec, d_state_spec]
                 + [_resident(w.shape) for w in small] + cast_specs,
        out_specs=[seg(0),
                   pl.BlockSpec((1, B_HEADS, B_DK, B_DV), lambda b, t: (b, 0, 0, 0)),
                   d_seg(0, A_WIDTH), d_seg(0, B_VW), d_state_spec, d_seg(0, A_WIDTH),
                   d_seg(0, 2 * D_MODEL)] + cast_specs,
        out_shape=[jax.ShapeDtypeStruct((m, D_MODEL), F32),
                   jax.ShapeDtypeStruct((n_seq, B_HEADS, B_DK, B_DV), F32),
                   jax.ShapeDtypeStruct((steps, n_dec, A_WIDTH), F32),
                   jax.ShapeDtypeStruct((steps, n_dec, B_VW), F32),
                   jax.ShapeDtypeStruct(d_state.shape, F32),
                   jax.ShapeDtypeStruct((steps, n_dec, A_WIDTH), F32),
                   jax.ShapeDtypeStruct((steps, n_dec, 2 * D_MODEL), F32)] + cast_shapes,
        scratch_shapes=[pltpu.VMEM((B_HEADS, B_DK, B_DV), F32), pltpu.VMEM((n_dec, B_VW), F32),
                        pltpu.VMEM((tt, P_COLS), BF16), pltpu.VMEM((tt, B_KW), F32),
                        pltpu.VMEM((n_dec, P_COLS), F32), pltpu.VMEM((n_dec, B_KW), F32)],
        compiler_params=_params(2),
        name="mixer",
    )(h, x1, d_h3, d_state, *small, *to_cast)
    return (x2, sp, ya.reshape(n_all, A_WIDTH), yb.reshape(n_all, B_VW), ss,
            cv.reshape(n_all, A_WIDTH), gates.reshape(n_all, 2 * D_MODEL), cast)


def _merge(ya, yb, gates, x1, wpa, wpb, wo):
    tm = m = x1.shape[0]
    return pl.pallas_call(
        _merge_kernel,
        grid=(m // tm,),
        in_specs=[_rows(tm, A_WIDTH), _rows(tm, B_VW), _rows(tm, D_MODEL, 0),
                  _rows(tm, D_MODEL, 1), _rows(tm, D_MODEL), _resident(wpa.shape),
                  _resident(wpb.shape), _resident(wo.shape)],
        out_specs=_rows(tm, D_MODEL),
        out_shape=jax.ShapeDtypeStruct((m, D_MODEL), F32),
        compiler_params=_params(1),
        name="merge",
    )(ya, yb, gates, gates, x1, wpa, wpb, wo)


def kernel(x_prompt, x_sample, p_prompt, p_sample, state_gla, g_ffn1, w_ffn1_in, w_ffn1_out, g_mix,
           w_in, ln_v_g, ln_v_b, w_spatial, b_spatial, w_gate_up, b_gate, g_gla_out, w_proj_a,
           w_proj_b, w_out, g_ffn2, w_ffn2_in, w_ffn2_out, g_ple, w_ple_gate, w_ple, g_final):
    depth = w_in.shape[0]
    n_seq, seq, _ = x_prompt.shape
    n_dec, dec_seq, _ = x_sample.shape
    assert dec_seq == 1 and seq % MIXER_TILE == 0

    def row(vec):
        return vec.reshape(1, -1).astype(F32)

    assert sum(IN_SIZES[:6]) == LR_LO and sum(IN_SIZES) == GATE_LO + 2 * D_MODEL
    xp = x_prompt.reshape(n_seq * seq, D_MODEL)
    xs = x_sample.reshape(n_dec, D_MODEL)
    gfin = row(g_final)
    sp_list, ss_list, vs_list = [], [], []
    for i in range(depth):
        wgu =jnp.pad(w_gate_up[i], ((0, LANES - GATE_RANK), (0, 0))).astype(BF16)
        g1, gmix, g2, gple = row(g_ffn1[i]), row(g_mix[i]), row(g_ffn2[i]), row(g_ple[i])
        lng, lnb, bg, ggla = row(ln_v_g[i]), row(ln_v_b[i]), row(b_gate[i]), row(g_gla_out[i])
        ws0 = row(jnp.repeat(w_spatial[i][:, 0, 0], A_GROUP_DIM))
        bs0 = row(jnp.repeat(b_spatial[i][:, 0], A_GROUP_DIM))

        wide, act = (D_MODEL, F32), (D_MODEL, BF16)
        (x1p, hp), (x1s, hs), (w_inb, wpa, wpb, wo) = _ffn_stage_call(
            "ffn1", _ffn1_finish, xp, xs, [], [], [g1, w_ffn1_in[i], w_ffn1_out[i], gmix],
            [wide, act], to_cast=[w_in[i].T, w_proj_a[i], w_proj_b[i], w_out[i]], own_weights=True,
            defer=False)
        x2p, sp, ya, yb, ss, cv, gates, (w2i, w2o, wpg, wple) = _mixer(
            hp, x1p, n_seq, seq, hs, state_gla[i], [w_inb, wgu, bg, lng, lnb], w_spatial[i],
            b_spatial[i].T, ggla, wpa, wpb, wo, ws0, bs0,
            to_cast=[w_ffn2_in[i], w_ffn2_out[i], w_ple_gate[i], w_ple[i]])
        x2s = _merge(ya.astype(BF16), yb.astype(BF16), gates, x1s, wpa, wpb, wo)

        ple_p = p_prompt[i].reshape(n_seq * seq, PLE_DIM)
        ple_s = p_sample[i].reshape(n_dec, PLE_DIM)
        (xp,), (xs,), _ = _ffn_stage_call(
            "ffn2", functools.partial(_ffn2_finish, final_norm=i == depth - 1), x2p, x2s,
            [ple_p], [ple_s], [g2, w2i, w2o, gple, wpg, wple, gfin], [wide])

        sp_list.append(sp)
        ss_list.append(ss)
        vs_list.append(cv.reshape(n_dec, dec_seq, A_WIDTH))
    return (xp.reshape(n_seq, seq, D_MODEL), xs.reshape(n_dec, dec_seq, D_MODEL),
            jnp.stack(sp_list), jnp.stack(ss_list), jnp.stack(vs_list))
```

```python
import functools

import jax
import jax.numpy as jnp
from jax import lax
from jax.experimental import pallas as pl
from jax.experimental.pallas import tpu as pltpu

D_MODEL = 1024
D_FF = 2816
PLE_DIM = 256
CHUNK = 128
A_GROUPS = 4
A_GROUP_DIM = 128
A_WIDTH = A_GROUPS * A_GROUP_DIM
B_HEADS = 4
B_DK = 128
B_DV = 256
B_KW = B_HEADS * B_DK
B_VW = B_HEADS * B_DV
GATE_RANK = 16
GATE_NORM = 16.0
EPS = 1e-6
IN_SIZES = (A_WIDTH, A_WIDTH, B_KW, B_KW, B_VW, B_VW, GATE_RANK, D_MODEL, D_MODEL)

LANES = 128
P_COLS = 6 * D_MODEL
QK_LO = 2 * A_WIDTH
VB_LO = QK_LO + 2 * B_KW
GB_LO = VB_LO + B_VW
LR_LO = GB_LO + B_VW
GATE_LO = LR_LO + GATE_RANK
FF_CHUNK = 256
HALF = CHUNK // 2
MIXER_TILE = 4 * CHUNK
TOKEN_TILE = 512
BF16_SUBLANES = 16
VMEM_LIMIT = 56 * 1024 * 1024

F32 = jnp.float32
BF16 = jnp.bfloat16


def _dot(a, b):
    return jnp.dot(a, b, preferred_element_type=F32)


def _dot_nt(a, b):
    return lax.dot_general(a, b, (((1,), (1,)), ((), ())), preferred_element_type=F32)


def _rms(x, g):
    return x * lax.rsqrt(jnp.mean(x * x, axis=-1, keepdims=True) + EPS) * g


def _log_sigmoid(x):
    return jnp.minimum(x, 0.0) - jnp.log1p(jnp.exp(-jnp.abs(x)))


def _layernorm(x, g, b):
    mu = jnp.mean(x, axis=-1, keepdims=True)
    xc = x - mu
    var = jnp.mean(xc * xc, axis=-1, keepdims=True)
    return xc * lax.rsqrt(var + EPS) * g + b


def _swiglu_residual(x, g_ref, w_in_ref, w_out_ref, early, before_chunk=None):
    h = _rms(x, g_ref[...]).astype(BF16)
    acts = []
    for ci, lo in enumerate(range(0, D_FF, FF_CHUNK)):
        if before_chunk is not None:
            before_chunk(ci)
        gate = _dot(h, w_in_ref[:, lo:lo + FF_CHUNK])
        up = _dot(h, w_in_ref[:, D_FF + lo:D_FF + lo + FF_CHUNK])
        acts.append((jax.nn.silu(gate) * up).astype(BF16))
        if ci == 1:
            early()
    return x + 0.5 * _dot(jnp.concatenate(acts, axis=1), w_out_ref[...])


def _cast_blocks(src_refs, dst_refs):
    for src, dst in zip(src_refs, dst_refs):
        dst[...] = src[...].astype(dst.dtype)


class _WeightStream:
    def __init__(self, w_in_hbm, w_out_hbm, w_in_scr, w_out_scr, stage_in, stage_out, sems):
        self.w_in_hbm, self.w_out_hbm = w_in_hbm, w_out_hbm
        self.w_in_scr, self.w_out_scr = w_in_scr, w_out_scr
        self.stage_in, self.stage_out, self.sems = stage_in, stage_out, sems

    def _copies(self, ci):
        slot, lo = ci % 2, ci * FF_CHUNK
        cols = [pl.ds(lo, FF_CHUNK), pl.ds(D_FF + lo, FF_CHUNK)]
        return ([pltpu.make_async_copy(self.w_in_hbm.at[:, c], self.stage_in.at[slot, j],
                                       self.sems.at[slot, j]) for j, c in enumerate(cols)]
                + [pltpu.make_async_copy(self.w_out_hbm.at[pl.ds(lo, FF_CHUNK), :],
                                         self.stage_out.at[slot], self.sems.at[slot, 2])])

    def start(self, ci):
        for copy in self._copies(ci):
            copy.start()

    def finish(self, ci):
        slot, lo = ci % 2, ci * FF_CHUNK
        for copy in self._copies(ci):
            copy.wait()
        self.w_in_scr[:, lo:lo + FF_CHUNK] = self.stage_in[slot, 0].astype(BF16)
        self.w_in_scr[:, D_FF + lo:D_FF + lo + FF_CHUNK] = self.stage_in[slot, 1].astype(BF16)
        self.w_out_scr[lo:lo + FF_CHUNK, :] = self.stage_out[slot].astype(BF16)
        if ci + 2 < D_FF // FF_CHUNK:
            self.start(ci + 2)


def _ffn_stage_kernel(*refs, finish, n_side, n_w, n_out, n_cast, own_weights, defer):
    it = iter(refs)

    def take(k):
        return [next(it) for _ in range(k)]

    (x_ref,), p_side, (xs_ref,), s_side = take(1), take(n_side), take(1), take(n_side)
    weights, cast_src = take(n_w), take(n_cast)
    p_out, s_out, cast_dst = take(n_out), take(n_out), take(n_cast)
    carry = next(it) if defer else None
    step = pl.program_id(0)
    sample_step = pl.num_programs(0) - 1
    stream = None
    if own_weights:
        w_in_scr, w_out_scr, stage_in, stage_out, sems = take(5)
        stream = _WeightStream(weights[1], weights[2], w_in_scr, w_out_scr, stage_in, stage_out,
                               sems)
        weights = [weights[0], w_in_scr, w_out_scr] + weights[3:]

    def finish_previous_tile():
        _cast_blocks(cast_src, cast_dst)
        if defer:
            finish(carry[...], p_side, weights, p_out)

    def prompt_tile(before_chunk=None):
        x_new = _swiglu_residual(x_ref[...], *weights[:3], finish_previous_tile, before_chunk)
        if defer:
            carry[...] = x_new
        else:
            finish(x_new, p_side, weights, p_out)

    if defer:
        @pl.when(step == 0)
        def _():
            carry[...] = jnp.zeros_like(carry)

    if own_weights:
        @pl.when(step == 0)
        def _():
            stream.start(0)
            stream.start(1)
            prompt_tile(before_chunk=stream.finish)

    @pl.when((step < sample_step) & (step > 0) if own_weights else step < sample_step)
    def _():
        prompt_tile()

    @pl.when(step == sample_step)
    def _():
        x_new = _swiglu_residual(xs_ref[...], *weights[:3], finish_previous_tile)
        finish(x_new, s_side, weights, s_out)


def _ffn1_finish(x1, side, weights, outs):
    gmix_ref, (x1_ref, h_ref) = weights[3], outs
    x1_ref[...] = x1
    h_ref[...] = _rms(x1, gmix_ref[...]).astype(BF16)


def _activated_projection(h, weights, put, put_la):
    wt_ref, wgu_ref, bg_ref, lng_ref, lnb_ref = weights

    def proj(lo, width):
        return _dot_nt(h, wt_ref[lo:lo + width, :])

    lr = proj(LR_LO, LANES).astype(BF16)
    logit = _dot(lr, wgu_ref[...]) + bg_ref[...]
    put_la(_log_sigmoid(logit) * (1.0 / GATE_NORM))
    half = D_MODEL // 2
    segments = [(0, 0, jax.nn.gelu),
                (A_WIDTH, A_WIDTH,
                 lambda z: _layernorm(jax.nn.gelu(z), lng_ref[...], lnb_ref[...])),
                (QK_LO, QK_LO, lambda z: z * (B_DK ** -0.5)),
                (QK_LO + B_KW, QK_LO + B_KW, lambda z: z)]
    segments += [(VB_LO + j * half, VB_LO + j * half, lambda z: z) for j in range(2)]
    segments += [(GATE_LO + j * half, LR_LO + j * half, jax.nn.sigmoid) for j in range(4)]
    segments += [(GB_LO + j * half, GB_LO + j * half, jax.nn.silu) for j in range(2)]
    for w_lo, p_lo, act in segments:
        put(p_lo, act(proj(w_lo, half)))


def _gla_prep(la, qk, ones_tril):
    la_hi = la.astype(BF16)
    la_lo = (la - la_hi.astype(F32)).astype(BF16)
    b = _dot(jnp.concatenate([ones_tril, ones_tril], axis=1),
             jnp.concatenate([la_hi, la_lo], axis=0))
    r0 = b[HALF // 2 - 1:HALF // 2, :]
    r1 = b[HALF + HALF // 2 - 1:HALF + HALF // 2, :]
    b_mid = b[HALF - 1:HALF, :]
    b_end = b[CHUNK - 1:CHUNK, :]
    bq0 = b[:HALF] - r0
    bq1 = b[HALF:] - r1
    qs = qk[:, :B_KW]
    k = qk[:, B_KW:]
    a0 = qs[:HALF] * jnp.exp(bq0)
    a1 = qs[HALF:] * jnp.exp(bq1)
    k0 = k[:HALF] * jnp.exp(-bq0)
    k1 = k[HALF:] * jnp.exp(-bq1)
    q_off1 = a1 * jnp.exp(r1 - b_mid)
    k_off0 = k0 * jnp.exp(b_mid - r0)
    q_int = jnp.concatenate([a0 * jnp.exp(r0), q_off1 * jnp.exp(b_mid)], axis=0)
    k_st = jnp.concatenate([k_off0 * jnp.exp(b_end - b_mid), k1 * jnp.exp(b_end - r1)], axis=0)
    return dict(a0=a0.astype(BF16), a1=a1.astype(BF16), k0=k0.astype(BF16), k1=k1.astype(BF16),
                q_off1=q_off1.astype(BF16), k_off0=k_off0.astype(BF16), q_int=q_int.astype(BF16),
                k_st=k_st, decay=jnp.exp(b_end))


def _mixer_kernel(*refs, n_chunks, n_dec, n_cast):
    it = iter(refs)

    def take(k):
        return [next(it) for _ in range(k)]

    h_ref, x1_ref, d_h_ref, d_state_ref = take(4)
    proj_w = take(5)
    ws_ref, bsp_ref, ggla_ref, wpa_ref, wpb_ref, wo_ref, ws0_ref, bs0_ref = take(8)
    cast_src = take(n_cast)
    x2_ref, sout_ref, d_ya_ref, d_yb_ref, d_sout_ref, d_cv_ref, d_gate_ref = take(7)
    cast_dst = take(n_cast)
    s_scr, o_scr, p_scr, la_ref, d_p_scr, d_la_ref = take(6)
    t = pl.program_id(1)
    tt = h_ref.shape[0]

    @pl.when(t == 0)
    def _():
        s_scr[...] = jnp.zeros_like(s_scr)

    def put(lo, val):
        p_scr[:, lo:lo + val.shape[1]] = val[:tt].astype(BF16)
        d_p_scr[:, lo:lo + val.shape[1]] = val[tt:tt + n_dec]

    def put_la(val):
        la_ref[...] = val[:tt]
        d_la_ref[...] = val[tt:tt + n_dec]

    _activated_projection(jnp.concatenate([h_ref[...], d_h_ref[0]], axis=0), proj_w, put, put_la)
    uv_ref, qk_ref, v_ref, gb_ref, sga_ref, sgb_ref = (
        p_scr.at[:, j * D_MODEL:(j + 1) * D_MODEL] for j in range(6))
    d_uv_ref, d_qk_ref, d_v_ref, d_gb_ref = (
        d_p_scr.at[:, j * D_MODEL:(j + 1) * D_MODEL] for j in range(4))

    row = lax.broadcasted_iota(jnp.int32, (CHUNK, CHUNK), 0)
    col = lax.broadcasted_iota(jnp.int32, (CHUNK, CHUNK), 1)
    causal = row >= col
    ones_tril = causal.astype(BF16)
    ws_tril = [jnp.where(causal, ws_ref[g], 0.0).astype(BF16) for g in range(A_GROUPS)]
    zero_blk = jnp.zeros((HALF, B_DK), BF16)
    zero_sq = jnp.zeros((CHUNK, A_GROUP_DIM), BF16)
    chunks = [slice(c * CHUNK, (c + 1) * CHUNK) for c in range(n_chunks)]
    heads = [(slice(h * B_DK, (h + 1) * B_DK), slice(h * B_DV, (h + 1) * B_DV))
             for h in range(B_HEADS)]

    ya = []
    for rows in chunks:
        u = uv_ref[rows, :A_WIDTH].astype(F32)
        vn = uv_ref[rows, A_WIDTH:]
        parts = []
        for g in range(0, A_GROUPS, 2):
            c0, c1 = (slice(j * A_GROUP_DIM, (j + 1) * A_GROUP_DIM) for j in (g, g + 1))
            rhs = jnp.concatenate([jnp.concatenate([vn[:, c0], zero_sq], axis=1),
                                   jnp.concatenate([zero_sq, vn[:, c1]], axis=1)], axis=0)
            mixed = _dot(jnp.concatenate([ws_tril[g], ws_tril[g + 1]], axis=1), rhs)
            for j, cols in enumerate((c0, c1)):
                mixed_g = mixed[:, j * A_GROUP_DIM:(j + 1) * A_GROUP_DIM] + bsp_ref[:, g + j:g + j + 1]
                parts.append((u[:, cols] * mixed_g).astype(BF16))
        ya.append(jnp.concatenate(parts, axis=1))

    prep = [_gla_prep(la_ref[rows, :], qk_ref[rows, :].astype(F32), ones_tril) for rows in chunks]
    scores = []
    for p in prep:
        per_head = []
        for kc, _ in heads:
            top = _dot_nt(p["a0"][:, kc], jnp.concatenate([p["k0"][:, kc], zero_blk], axis=0))
            bot = _dot_nt(
                jnp.concatenate([p["q_off1"][:, kc], p["a1"][:, kc]], axis=1),
                jnp.concatenate([jnp.concatenate([p["k_off0"][:, kc], zero_blk], axis=1),
                                 jnp.concatenate([zero_blk, p["k1"][:, kc]], axis=1)], axis=0))
            s = jnp.concatenate([top, bot], axis=0)
            per_head.append(jnp.where(causal, s, 0.0).astype(BF16))
        scores.append(per_head)

    state = [s_scr[h] for h in range(B_HEADS)]
    state_before = []
    for rows, p in zip(chunks, prep):
        state_before.append([s.astype(BF16) for s in state])
        for h, (kc, vc) in enumerate(heads):
            upd = _dot(p["k_st"][:, kc].T.astype(BF16), v_ref[rows, vc])
            decay_col = jnp.broadcast_to(p["decay"][:, kc], (B_DK, B_DK)).T
            state[h] = state[h] * jnp.concatenate([decay_col, decay_col], axis=1) + upd
    for h in range(B_HEADS):
        s_scr[h] = state[h]

    yb = []
    for c, (rows, p) in enumerate(zip(chunks, prep)):
        gb = gb_ref[rows, :].astype(F32)
        parts = []
        for h, (kc, vc) in enumerate(heads):
            o = _dot(jnp.concatenate([scores[c][h], p["q_int"][:, kc]], axis=1),
                     jnp.concatenate([v_ref[rows, vc], state_before[c][h]], axis=0))
            parts.append((_rms(o, ggla_ref[...]) * gb[:, vc]).astype(BF16))
        yb.append(jnp.concatenate(parts, axis=1))

    pa = _dot(jnp.concatenate(ya, axis=0), wpa_ref[...])
    _cast_blocks(cast_src, cast_dst)
    _sample_mixers(d_uv_ref, d_qk_ref, d_v_ref, d_gb_ref, d_la_ref, d_state_ref, ws0_ref, bs0_ref,
                   ggla_ref, d_ya_ref.at[0], d_yb_ref.at[0], d_sout_ref, d_cv_ref.at[0], o_scr,
                   n_dec)
    d_gate_ref[0] = d_p_scr[:, LR_LO:]
    mix = (sga_ref[...].astype(F32) * pa
           + sgb_ref[...].astype(F32) * _dot(jnp.concatenate(yb, axis=0), wpb_ref[...]))
    x2_ref[...] = x1_ref[...] + _dot(mix.astype(BF16), wo_ref[...])

    @pl.when(t == pl.num_programs(1) - 1)
    def _():
        sout_ref[0] = s_scr[...]


def _sample_mixers(uv_ref, qk_ref, v_ref, gb_ref, la_ref, s_ref, ws0_ref, bs0_ref, ggla_ref,
                   ya_ref, yb_ref, sout_ref, cv_ref, o_scr, nb):
    vn = uv_ref[:, A_WIDTH:]
    cv_ref[...] = vn
    ya_ref[...] = uv_ref[:, :A_WIDTH] * (vn * ws0_ref[...] + bs0_ref[...])

    a = jnp.exp(la_ref[...])
    qk = qk_ref[...]
    qs = qk[:, :B_KW]
    k = qk[:, B_KW:]
    v = v_ref[...]
    pad = jnp.zeros((LANES - 3 * nb, B_DK), F32)
    for h in range(B_HEADS):
        kc = slice(h * B_DK, (h + 1) * B_DK)
        vc = slice(h * B_DV, (h + 1) * B_DV)
        xt = jnp.concatenate([a[:, kc], k[:, kc], qs[:, kc], pad], axis=0).T
        for n in range(nb):
            s_new = (s_ref[n, h] * xt[:, n:n + 1]
                     + xt[:, nb + n:nb + n + 1] * v[n:n + 1, vc])
            sout_ref[n, h] = s_new
            o_scr[n:n + 1, vc] = jnp.sum(xt[:, 2 * nb + n:2 * nb + n + 1] * s_new,
                                         axis=0, keepdims=True)
    gb = gb_ref[...]
    for h in range(B_HEADS):
        vc = slice(h * B_DV, (h + 1) * B_DV)
        yb_ref[:, vc] = _rms(o_scr[:, vc], ggla_ref[...]) * gb[:, vc]


def _merge_kernel(ya_ref, yb_ref, ga_ref, gb_ref, x1_ref, wpa_ref, wpb_ref, wo_ref, x2_ref):
    mix = (ga_ref[...].astype(F32) * _dot(ya_ref[...], wpa_ref[...])
           + gb_ref[...].astype(F32) * _dot(yb_ref[...], wpb_ref[...]))
    x2_ref[...] = x1_ref[...] + _dot(mix.astype(BF16), wo_ref[...])


def _ffn2_finish(x3, side, weights, outs, *, final_norm):
    (p_ref,), (y_ref,) = side, outs
    gple_ref, wpg_ref, wple_ref, gfin_ref = weights[3:]
    gate = jax.nn.sigmoid(_dot(_rms(x3, gple_ref[...]).astype(BF16), wpg_ref[...]))
    x4 = x3 + _dot(p_ref[...].astype(BF16), wple_ref[...]) * gate
    y_ref[...] = _rms(x4, gfin_ref[...]) if final_norm else x4


def _resident(shape):
    zeros = (0,) * len(shape)
    return pl.BlockSpec(shape, lambda *_: zeros, pipeline_mode=pl.Buffered(1))


def _rows(tm, width, colblk=0):
    return pl.BlockSpec((tm, width), lambda i: (i, colblk))


def _params(n_axes):
    return pltpu.CompilerParams(dimension_semantics=("arbitrary",) * n_axes,
                                vmem_limit_bytes=VMEM_LIMIT)


def _cast_specs(weights, n_steps, linear_step=lambda i: i):
    specs, shapes = [], []
    for w in weights:
        rows, cols = w.shape
        n_blocks = max(n for n in range(1, n_steps + 1) if rows % (n * BF16_SUBLANES) == 0)
        specs.append(pl.BlockSpec(
            (rows // n_blocks, cols),
            lambda *idx, n=n_blocks: (jnp.minimum(linear_step(*idx), n - 1), 0)))
        shapes.append(jax.ShapeDtypeStruct(w.shape, BF16))
    return specs, shapes


def _ffn_stage_call(name, finish, x, x_s, side, side_s, weights, outputs, to_cast=(),
                    own_weights=False, defer=True):
    m, ms = x.shape[0], x_s.shape[0]
    assert m % TOKEN_TILE == 0
    n = m // TOKEN_TILE
    weight_specs = [_resident(w.shape) for w in weights]
    scratch = [pltpu.VMEM((TOKEN_TILE, D_MODEL), F32)] if defer else []
    lag = 1 if defer else 0
    if own_weights:
        w_in, w_out = weights[1], weights[2]
        weight_specs[1] = weight_specs[2] = pl.BlockSpec(memory_space=pl.ANY)
        scratch += [pltpu.VMEM(w_in.shape, BF16), pltpu.VMEM(w_out.shape, BF16),
                    pltpu.VMEM((2, 2, w_in.shape[0], FF_CHUNK), F32),
                    pltpu.VMEM((2, FF_CHUNK, w_out.shape[1]), F32),
                    pltpu.SemaphoreType.DMA((2, 3))]

    def tile(cols, lag):
        return pl.BlockSpec((TOKEN_TILE, cols), lambda i: (jnp.clip(i - lag, 0, n - 1), 0))

    def s_spec(cols):
        return pl.BlockSpec((ms, cols), lambda i: (0, 0))

    cast_specs, cast_shapes = _cast_specs(to_cast, n)
    outs = pl.pallas_call(
        functools.partial(_ffn_stage_kernel, finish=finish, n_side=len(side), n_w=len(weights),
                          n_out=len(outputs), n_cast=len(to_cast), own_weights=own_weights,
                          defer=defer),
        grid=(n + 1,),
        in_specs=[tile(D_MODEL, 0)] + [tile(a.shape[1], lag) for a in side]
                 + [s_spec(D_MODEL)] + [s_spec(a.shape[1]) for a in side_s]
                 + weight_specs + cast_specs,
        out_specs=[tile(c, lag) for c, _ in outputs] + [s_spec(c) for c, _ in outputs] + cast_specs,
        out_shape=[jax.ShapeDtypeStruct((m, c), dt) for c, dt in outputs]
                  + [jax.ShapeDtypeStruct((ms, c), dt) for c, dt in outputs] + cast_shapes,
        scratch_shapes=scratch,
        compiler_params=_params(1),
        name=name,
    )(x, *side, x_s, *side_s, *weights, *to_cast)
    k = len(outputs)
    return outs[:k], outs[k:2 * k], outs[2 * k:]


def _mixer(h, x1, n_seq, seq, d_h, d_state, proj_w, ws, bsp_t, ggla, wpa, wpb, wo, ws0, bs0,
           to_cast=()):
    tt = MIXER_TILE
    nt = seq // tt
    m = n_seq * seq
    steps = n_seq * nt
    n_all = d_h.shape[0]
    assert n_all % steps == 0
    n_dec = n_all // steps
    assert n_dec <= BF16_SUBLANES
    d_h3 = jnp.pad(d_h.reshape(steps, n_dec, D_MODEL), ((0, 0), (0, BF16_SUBLANES - n_dec), (0, 0)))

    def seg(colblk, width=D_MODEL):
        return pl.BlockSpec((tt, width), lambda b, t: (b * nt + t, colblk))

    def d_seg(colblk, width=D_MODEL):
        return pl.BlockSpec((1, n_dec, width), lambda b, t: (b * nt + t, 0, colblk))

    d_state_spec = pl.BlockSpec((n_dec, B_HEADS, B_DK, B_DV), lambda b, t: (b * nt + t, 0, 0, 0))
    d_h_spec = pl.BlockSpec((1, BF16_SUBLANES, D_MODEL), lambda b, t: (b * nt + t, 0, 0))
    small = list(proj_w) + [ws, bsp_t, ggla, wpa, wpb, wo, ws0, bs0]
    cast_specs, cast_shapes = _cast_specs(to_cast, steps, lambda b, t: b * nt + t)
    x2, sp, ya, yb, ss, cv, gates, *cast = pl.pallas_call(
        functools.partial(_mixer_kernel, n_chunks=tt // CHUNK, n_dec=n_dec, n_cast=len(to_cast)),
        grid=(n_seq, nt),
        in_specs=[seg(0), seg(0), d_h_spec, d_state_spec]
                 + [_resident(w.shape) for w in small] + cast_specs,
        out_specs=[seg(0),
                   pl.BlockSpec((1, B_HEADS, B_DK, B_DV), lambda b, t: (b, 0, 0, 0)),
                   d_seg(0, A_WIDTH), d_seg(0, B_VW), d_state_spec, d_seg(0, A_WIDTH),
                   d_seg(0, 2 * D_MODEL)] + cast_specs,
        out_shape=[jax.ShapeDtypeStruct((m, D_MODEL), F32),
                   jax.ShapeDtypeStruct((n_seq, B_HEADS, B_DK, B_DV), F32),
                   jax.ShapeDtypeStruct((steps, n_dec, A_WIDTH), F32),
                   jax.ShapeDtypeStruct((steps, n_dec, B_VW), F32),
                   jax.ShapeDtypeStruct(d_state.shape, F32),
                   jax.ShapeDtypeStruct((steps, n_dec, A_WIDTH), F32),
                   jax.ShapeDtypeStruct((steps, n_dec, 2 * D_MODEL), F32)] + cast_shapes,
        scratch_shapes=[pltpu.VMEM((B_HEADS, B_DK, B_DV), F32), pltpu.VMEM((n_dec, B_VW), F32),
                        pltpu.VMEM((tt, P_COLS), BF16), pltpu.VMEM((tt, B_KW), F32),
                        pltpu.VMEM((n_dec, P_COLS), F32), pltpu.VMEM((n_dec, B_KW), F32)],
        compiler_params=_params(2),
        name="mixer",
    )(h, x1, d_h3, d_state, *small, *to_cast)
    return (x2, sp, ya.reshape(n_all, A_WIDTH), yb.reshape(n_all, B_VW), ss,
            cv.reshape(n_all, A_WIDTH), gates.reshape(n_all, 2 * D_MODEL), cast)


def _merge(ya, yb, gates, x1, wpa, wpb, wo):
    tm = m = x1.shape[0]
    return pl.pallas_call(
        _merge_kernel,
        grid=(m // tm,),
        in_specs=[_rows(tm, A_WIDTH), _rows(tm, B_VW), _rows(tm, D_MODEL, 0),
                  _rows(tm, D_MODEL, 1), _rows(tm, D_MODEL), _resident(wpa.shape),
                  _resident(wpb.shape), _resident(wo.shape)],
        out_specs=_rows(tm, D_MODEL),
        out_shape=jax.ShapeDtypeStruct((m, D_MODEL), F32),
        compiler_params=_params(1),
        name="merge",
    )(ya, yb, gates, gates, x1, wpa, wpb, wo)


def kernel(x_prompt, x_sample, p_prompt, p_sample, state_gla, g_ffn1, w_ffn1_in, w_ffn1_out, g_mix,
           w_in, ln_v_g, ln_v_b, w_spatial, b_spatial, w_gate_up, b_gate, g_gla_out, w_proj_a,
           w_proj_b, w_out, g_ffn2, w_ffn2_in, w_ffn2_out, g_ple, w_ple_gate, w_ple, g_final):
    depth = w_in.shape[0]
    n_seq, seq, _ = x_prompt.shape
    n_dec, dec_seq, _ = x_sample.shape
    assert dec_seq == 1 and seq % MIXER_TILE == 0

    def row(vec):
        return vec.reshape(1, -1).astype(F32)

    assert sum(IN_SIZES[:6]) == LR_LO and sum(IN_SIZES) == GATE_LO + 2 * D_MODEL
    xp = x_prompt.reshape(n_seq * seq, D_MODEL)
    xs = x_sample.reshape(n_dec, D_MODEL)
    gfin = row(g_final)
    sp_list, ss_list, vs_list = [], [], []
    for i in range(depth):
        wgu =jnp.pad(w_gate_up[i], ((0, LANES - GATE_RANK), (0, 0))).astype(BF16)
        g1, gmix, g2, gple = row(g_ffn1[i]), row(g_mix[i]), row(g_ffn2[i]), row(g_ple[i])
        lng, lnb, bg, ggla = row(ln_v_g[i]), row(ln_v_b[i]), row(b_gate[i]), row(g_gla_out[i])
        ws0 = row(jnp.repeat(w_spatial[i][:, 0, 0], A_GROUP_DIM))
        bs0 = row(jnp.repeat(b_spatial[i][:, 0], A_GROUP_DIM))

        wide, act = (D_MODEL, F32), (D_MODEL, BF16)
        (x1p, hp), (x1s, hs), (w_inb, wpa, wpb, wo) = _ffn_stage_call(
            "ffn1", _ffn1_finish, xp, xs, [], [], [g1, w_ffn1_in[i], w_ffn1_out[i], gmix],
            [wide, act], to_cast=[w_in[i].T, w_proj_a[i], w_proj_b[i], w_out[i]], own_weights=True,
            defer=False)
        x2p, sp, ya, yb, ss, cv, gates, (w2i, w2o, wpg, wple) = _mixer(
            hp, x1p, n_seq, seq, hs, state_gla[i], [w_inb, wgu, bg, lng, lnb], w_spatial[i],
            b_spatial[i].T, ggla, wpa, wpb, wo, ws0, bs0,
            to_cast=[w_ffn2_in[i], w_ffn2_out[i], w_ple_gate[i], w_ple[i]])
        x2s = _merge(ya.astype(BF16), yb.astype(BF16), gates, x1s, wpa, wpb, wo)

        ple_p = p_prompt[i].reshape(n_seq * seq, PLE_DIM)
        ple_s = p_sample[i].reshape(n_dec, PLE_DIM)
        (xp,), (xs,), _ = _ffn_stage_call(
            "ffn2", functools.partial(_ffn2_finish, final_norm=i == depth - 1), x2p, x2s,
            [ple_p], [ple_s], [g2, w2i, w2o, gple, wpg, wple, gfin], [wide])

        sp_list.append(sp)
        ss_list.append(ss)
        vs_list.append(cv.reshape(n_dec, dec_seq, A_WIDTH))
    return (xp.reshape(n_seq, seq, D_MODEL), xs.reshape(n_dec, dec_seq, D_MODEL),
            jnp.stack(sp_list), jnp.stack(ss_list), jnp.stack(vs_list))
```
